```python
import jax, jax.numpy as jnp
from jax import lax
import numpy as np

D_MODEL = 1024
BATCH = 1
SEQ = 16384
DEPTH = 4

HEAD_DIM = 64
MIX_WIDTH = D_MODEL
ATTN_WIDTH = MIX_WIDTH // 2
ATTN_Q_HEADS = ATTN_WIDTH // HEAD_DIM
ATTN_KV_HEADS = 2
KV_WIDTH = ATTN_KV_HEADS * HEAD_DIM
WINDOW = 128
ATTN_BLOCK = 128
RWKV_WIDTH = MIX_WIDTH // 4
RWKV_HEADS = RWKV_WIDTH // HEAD_DIM
DECAY_LORA = 64
ICLR_LORA = 64
RWKV_LN_EPS = 64e-5
GLA_VAL_WIDTH = MIX_WIDTH // 4
GLA_HEADS = 4
GLA_VAL_DIM = GLA_VAL_WIDTH // GLA_HEADS
GLA_KEY_WIDTH = GLA_VAL_WIDTH // 2
GLA_KEY_DIM = GLA_KEY_WIDTH // GLA_HEADS
GLA_GATE_LORA = 16
GLA_GATE_NORMALIZER = 16.0
GLA_CHUNK = 64
IN_SPLITS = (ATTN_WIDTH, KV_WIDTH, KV_WIDTH, ATTN_WIDTH,
             3 * RWKV_WIDTH, RWKV_WIDTH,
             GLA_KEY_WIDTH, GLA_KEY_WIDTH, GLA_VAL_WIDTH, GLA_VAL_WIDTH)
IN_WIDTH = 2 * ATTN_WIDTH + 2 * KV_WIDTH + 4 * RWKV_WIDTH + 2 * GLA_KEY_WIDTH + 2 * GLA_VAL_WIDTH
NORM_EPS = 1e-6

kernel_name = "hybrid_swa_rwkv7_gla_parallel_heads"


def _rmsnorm(x, g):
    xf = x.astype(jnp.float32)
    y = xf * lax.rsqrt(jnp.mean(xf * xf, axis=-1, keepdims=True) + NORM_EPS)
    return (y * g.astype(jnp.float32)).astype(x.dtype)


def _shift(z):
    return jnp.pad(z, ((0, 0), (1, 0), (0, 0)))[:, :-1]


def _sliding_window_attention(q, k, v, sinks):
    B, T, Hq, D = q.shape
    Hkv = k.shape[2]
    G = Hq // Hkv
    nb = T // ATTN_BLOCK
    qb = q.reshape(B, nb, ATTN_BLOCK, Hkv, G, D)

    def band(z):
        zb = z.reshape(B, nb, ATTN_BLOCK, Hkv, D)
        prev = jnp.concatenate([jnp.zeros_like(zb[:, :1]), zb[:, :-1]], axis=1)
        return jnp.concatenate([prev, zb], axis=2)

    kb, vb = band(k), band(v)
    s = jnp.einsum('bnqhgd,bnkhd->bnhgqk', qb, kb,
                   preferred_element_type=jnp.float32) * (D ** -0.5)
    qi = jnp.arange(ATTN_BLOCK)[:, None]
    kj = jnp.arange(2 * ATTN_BLOCK)[None, :]
    dist = qi - kj + ATTN_BLOCK
    key_pos = jnp.arange(nb)[:, None, None] * ATTN_BLOCK + kj[None] - ATTN_BLOCK
    valid = (dist >= 0) & (dist < WINDOW) & (key_pos >= 0)
    slopes = 2.0 ** (-8.0 * jnp.arange(1, Hq + 1, dtype=jnp.float32) / Hq)
    s = s - slopes.reshape(Hkv, G)[:, :, None, None] * dist.astype(jnp.float32)
    s = jnp.where(valid[None, :, None, None], s, -jnp.inf)
    sink = jnp.broadcast_to(sinks.astype(jnp.float32).reshape(Hkv, G)[:, :, None, None],
                            s.shape[:-1] + (1,))
    p = jax.nn.softmax(jnp.concatenate([s, sink], axis=-1), axis=-1)[..., :-1]
    o = jnp.einsum('bnhgqk,bnkhd->bnqhgd', p.astype(v.dtype), vb)
    return o.reshape(B, T, Hq * D)


def _rwkv7_mixer(h, r, k, v, mu_w, mu_a, w0, w1, w2, a0, a1, a2, k_k, k_a, r_k, ln_w, ln_b):
    B, T, _ = h.shape
    f32 = jnp.float32
    h_prev = _shift(h)
    xw = h + (h_prev - h) * mu_w
    xa = h + (h_prev - h) * mu_a
    w = -jax.nn.softplus(-(w0 + jnp.tanh(xw @ w1) @ w2).astype(f32)) - 0.5
    decay = jnp.exp(-jnp.exp(w))
    a = jax.nn.sigmoid((a0 + (xa @ a1) @ a2).astype(f32))
    r, k, v = r.astype(f32), k.astype(f32), v.astype(f32)
    heads = lambda z: z.reshape(B, T, RWKV_HEADS, HEAD_DIM)
    kk = heads(k * k_k.astype(f32))
    kk = kk / jnp.maximum(jnp.linalg.norm(kk, axis=-1, keepdims=True), 1e-12)
    k = k * (1.0 + (a - 1.0) * k_a.astype(f32))

    def step(S, inp):
        r_t, w_t, k_t, v_t, kk_t, a_t = inp
        sa = jnp.einsum('bhvk,bhk->bhv', S, -kk_t)
        S = (S * w_t[:, :, None, :] + sa[..., None] * (kk_t * a_t)[:, :, None, :]
             + v_t[..., None] * k_t[:, :, None, :])
        return S, jnp.einsum('bhvk,bhk->bhv', S, r_t)

    tm = lambda z: jnp.moveaxis(z, 1, 0)
    S0 = jnp.zeros((B, RWKV_HEADS, HEAD_DIM, HEAD_DIM), f32)
    _, y = lax.scan(step, S0, (tm(heads(r)), tm(heads(decay)), tm(heads(k)), tm(heads(v)),
                               tm(kk), tm(heads(a))))
    y = jnp.moveaxis(y, 0, 1)
    mean = jnp.mean(y, axis=-1, keepdims=True)
    var = jnp.mean(jnp.square(y - mean), axis=-1, keepdims=True)
    y = ((y - mean) * lax.rsqrt(var + RWKV_LN_EPS)).reshape(B, T, RWKV_WIDTH)
    y = y * ln_w.astype(f32) + ln_b.astype(f32)
    bonus = jnp.sum(heads(r) * heads(k) * r_k.astype(f32), axis=-1, keepdims=True) * heads(v)
    return (y + bonus.reshape(B, T, RWKV_WIDTH)).astype(h.dtype)


def _gla_mixer(h, q, k, v, gk1, gk2, gk_b, norm_w):
    B, T, _ = h.shape
    f32 = jnp.float32
    n = T // GLA_CHUNK
    gk = jax.nn.log_sigmoid(((h @ gk1) @ gk2 + gk_b).astype(f32)) / GLA_GATE_NORMALIZER

    def chunks(z, d):
        return z.reshape(B, n, GLA_CHUNK, GLA_HEADS, d).transpose(1, 0, 3, 2, 4).astype(f32)

    qc = chunks(q, GLA_KEY_DIM) * (GLA_KEY_DIM ** -0.5)
    kc = chunks(k, GLA_KEY_DIM)
    vc = chunks(v, GLA_VAL_DIM)
    gc = chunks(gk, GLA_KEY_DIM)
    causal = jnp.tril(jnp.ones((GLA_CHUNK, GLA_CHUNK), bool))

    def step(S, inp):
        q_c, k_c, v_c, g_c = inp
        b = jnp.cumsum(g_c, axis=2)
        diff = b[:, :, :, None, :] - b[:, :, None, :, :]
        diff = jnp.where(causal[:, :, None], diff, -jnp.inf)
        A = jnp.einsum('bhid,bhjd,bhijd->bhij', q_c, k_c, jnp.exp(diff))
        o = A @ v_c + jnp.einsum('bhid,bhdv->bhiv', q_c * jnp.exp(b), S)
        b_last = b[:, :, -1:, :]
        S = (jnp.exp(b_last[:, :, 0, :])[..., None] * S
             + jnp.einsum('bhjd,bhjv->bhdv', k_c * jnp.exp(b_last - b), v_c))
        return S, o

    S0 = jnp.zeros((B, GLA_HEADS, GLA_KEY_DIM, GLA_VAL_DIM), f32)
    _, o = lax.scan(step, S0, (qc, kc, vc, gc))
    o = o.transpose(1, 0, 3, 2, 4).reshape(B, T, GLA_HEADS, GLA_VAL_DIM)
    o = o * lax.rsqrt(jnp.mean(o * o, axis=-1, keepdims=True) + 1e-5) * norm_w.astype(f32)
    return o.reshape(B, T, GLA_VAL_WIDTH).astype(h.dtype)


def _hybrid_layer(x, c_act, ada_w, ada_b, g_pre, g_post, w_in, w_out, sinks,
                  mu_rkv, mu_w, mu_a, w0, w1, w2, a0, a1, a2, k_k, k_a, r_k, ln_w, ln_b,
                  gk1, gk2, gk_b, gla_norm_w):
    B, T, _ = x.shape
    shift, scale, gate = jnp.split(c_act @ ada_w + ada_b, 3, axis=-1)
    h = _rmsnorm(x, g_pre) * (1.0 + scale[:, None]) + shift[:, None]
    proj = h @ w_in
    aq, ak, av, ag, rkv, rg, gq, gkk, gv, gg = jnp.split(proj, list(np.cumsum(IN_SPLITS)[:-1]), axis=-1)
    attn = _sliding_window_attention(aq.reshape(B, T, ATTN_Q_HEADS, HEAD_DIM),
                                     ak.reshape(B, T, ATTN_KV_HEADS, HEAD_DIM),
                                     av.reshape(B, T, ATTN_KV_HEADS, HEAD_DIM), sinks)
    attn = attn * jax.nn.silu(ag)
    rkv = rkv + (_shift(rkv) - rkv) * mu_rkv
    rr, rk, rv = jnp.split(rkv, 3, axis=-1)
    rwkv = _rwkv7_mixer(h, rr, rk, rv, mu_w, mu_a, w0, w1, w2, a0, a1, a2, k_k, k_a, r_k, ln_w, ln_b)
    rwkv = rwkv * jax.nn.silu(rg)
    gla = _gla_mixer(h, gq, gkk, gv, gk1, gk2, gk_b, gla_norm_w) * jax.nn.silu(gg)
    y = jnp.concatenate([attn, rwkv, gla], axis=-1) @ w_out
    return x + gate[:, None] * _rmsnorm(y, g_post)


def setup_inputs(seed: int = 0) -> dict:
    key = jax.random.key(seed)
    ks = jax.random.split(key, 32)
    nrm = lambda k, shape, s: s * jax.random.normal(k, shape, jnp.float32)
    uni = lambda k, shape, lo, hi: jax.random.uniform(k, shape, jnp.float32, lo, hi)
    D, L = D_MODEL, DEPTH
    return {
        "x": nrm(ks[0], (BATCH, SEQ, D), 1.0),
        "c": nrm(ks[1], (BATCH, D), 1.0),
        "ada_w": nrm(ks[2], (L, D, 3 * D), 0.5 * D ** -0.5),
        "ada_b": nrm(ks[3], (L, 3 * D), 0.02),
        "norm_pre": 1.0 + nrm(ks[4], (L, D), 0.05),
        "norm_post": 1.0 + nrm(ks[5], (L, D), 0.05),
        "w_in": nrm(ks[6], (L, D, IN_WIDTH), D ** -0.5),
        "w_out": nrm(ks[7], (L, MIX_WIDTH, D), MIX_WIDTH ** -0.5),
        "attn_sinks": nrm(ks[8], (L, ATTN_Q_HEADS), 0.5),
        "rwkv_mu_rkv": uni(ks[9], (L, 3 * RWKV_WIDTH), 0.0, 1.0),
        "rwkv_mu_w": uni(ks[10], (L, D), 0.0, 1.0),
        "rwkv_mu_a": uni(ks[11], (L, D), 0.0, 1.0),
        "rwkv_w0": uni(ks[12], (L, RWKV_WIDTH), -4.0, 1.0),
        "rwkv_w1": nrm(ks[13], (L, D, DECAY_LORA), D ** -0.5),
        "rwkv_w2": nrm(ks[14], (L, DECAY_LORA, RWKV_WIDTH), 0.5 * DECAY_LORA ** -0.5),
        "rwkv_a0": nrm(ks[15], (L, RWKV_WIDTH), 0.1),
        "rwkv_a1": nrm(ks[16], (L, D, ICLR_LORA), D ** -0.5),
        "rwkv_a2": nrm(ks[17], (L, ICLR_LORA, RWKV_WIDTH), 0.5 * ICLR_LORA ** -0.5),
        "rwkv_k_k": 0.85 + nrm(ks[18], (L, RWKV_WIDTH), 0.05),
        "rwkv_k_a": 1.0 + nrm(ks[19], (L, RWKV_WIDTH), 0.05),
        "rwkv_r_k": nrm(ks[20], (L, RWKV_HEADS, HEAD_DIM), 0.1),
        "rwkv_ln_w": 1.0 + nrm(ks[21], (L, RWKV_WIDTH), 0.05),
        "rwkv_ln_b": nrm(ks[22], (L, RWKV_WIDTH), 0.02),
        "gla_gk1": nrm(ks[23], (L, D, GLA_GATE_LORA), D ** -0.5),
        "gla_gk2": nrm(ks[24], (L, GLA_GATE_LORA, GLA_KEY_WIDTH), GLA_GATE_LORA ** -0.5),
        "gla_gk_b": nrm(ks[25], (L, GLA_KEY_WIDTH), 0.1),
        "gla_norm_w": 1.0 + nrm(ks[26], (L, GLA_VAL_DIM), 0.05),
    }


def reference(x, c, ada_w, ada_b, norm_pre, norm_post, w_in, w_out, attn_sinks,
              rwkv_mu_rkv, rwkv_mu_w, rwkv_mu_a, rwkv_w0, rwkv_w1, rwkv_w2,
              rwkv_a0, rwkv_a1, rwkv_a2, rwkv_k_k, rwkv_k_a, rwkv_r_k, rwkv_ln_w, rwkv_ln_b,
              gla_gk1, gla_gk2, gla_gk_b, gla_norm_w):
    c_act = jax.nn.silu(c)
    for l in range(DEPTH):
        x = _hybrid_layer(x, c_act, ada_w[l], ada_b[l], norm_pre[l], norm_post[l], w_in[l], w_out[l],
                          attn_sinks[l], rwkv_mu_rkv[l], rwkv_mu_w[l], rwkv_mu_a[l],
                          rwkv_w0[l], rwkv_w1[l], rwkv_w2[l], rwkv_a0[l], rwkv_a1[l], rwkv_a2[l],
                          rwkv_k_k[l], rwkv_k_a[l], rwkv_r_k[l], rwkv_ln_w[l], rwkv_ln_b[l],
                          gla_gk1[l], gla_gk2[l], gla_gk_b[l], gla_norm_w[l])
    return x
```

```python
import numpy as np
import jax
import jax.numpy as jnp
from jax import lax
from jax.experimental import pallas as pl
from jax.experimental.pallas import tpu as pltpu

F32 = jnp.float32
MXU_DTYPE = jnp.bfloat16

D_MODEL = 1024
HEAD_DIM = 64
ATTN_WIDTH = 512
ATTN_Q_HEADS = 8
ATTN_KV_HEADS = 2
KV_WIDTH = 128
WINDOW = 128
RWKV_WIDTH = 256
RWKV_LN_EPS = 64e-5
LORA = 64
GLA_HEADS = 4
GLA_KEY_WIDTH = 128
GLA_KEY_DIM = 32
GLA_VAL_WIDTH = 256
GLA_GATE_LORA = 16
GLA_GATE_NORMALIZER = 16.0
NORM_EPS = 1e-6
IN_WIDTH = 3072

O_AQ, O_AK, O_AV, O_AG = 0, 512, 640, 768
O_RKV, O_RG = 1280, 2048
O_GQ, O_GK, O_GV, O_GG = 2304, 2432, 2560, 2816
O_X1, O_X2, O_X3 = 3072, 3200, 3328
AUG_WIDTH = 3456

TM = 512
SUB = 128
CH = 64
N_LEVELS = 6

VMEM_LIMIT_BYTES = 60000 * 1024


def _mm(a, b):
    return jnp.dot(a.astype(MXU_DTYPE), b.astype(MXU_DTYPE), preferred_element_type=F32)


def _mm_nt(a, b):
    return lax.dot_general(a.astype(MXU_DTYPE), b.astype(MXU_DTYPE), (((1,), (1,)), ((), ())),
                           preferred_element_type=F32)


def _mm_tn(a, b):
    return lax.dot_general(a.astype(MXU_DTYPE), b.astype(MXU_DTYPE), (((0,), (0,)), ((), ())),
                           preferred_element_type=F32)


def _split(x, n):
    parts, r = [], x
    for _ in range(n):
        p = r.astype(MXU_DTYPE)
        parts.append(p)
        r = r - p.astype(F32)
    return parts


def _sel_mm(sel, x, n=3):
    acc = None
    for p in _split(x, n):
        t = jnp.dot(sel, p, preferred_element_type=F32)
        acc = t if acc is None else acc + t
    return acc


def _mm_sel(x, sel, n=2):
    acc = None
    for p in _split(x, n):
        t = jnp.dot(p, sel, preferred_element_type=F32)
        acc = t if acc is None else acc + t
    return acc


def _mm_hi(a, b):
    ah, al = _split(a, 2)
    bh, bl = _split(b, 2)
    return (jnp.dot(ah, bh, preferred_element_type=F32) + jnp.dot(ah, bl, preferred_element_type=F32)
            + jnp.dot(al, bh, preferred_element_type=F32))


def _iota(shape, dim):
    return lax.broadcasted_iota(jnp.int32, shape, dim)


def _sigmoid(z):
    return 1.0 / (1.0 + jnp.exp(-z))


def _silu(z):
    return z * _sigmoid(z)


def _softplus(z):
    return jnp.maximum(z, 0.0) + jnp.log(1.0 + jnp.exp(-jnp.abs(z)))


def _block_diag(x, row_shift, col_shift):
    t = jnp.concatenate([x, x, x, x], axis=0)
    keep = (_iota(t.shape, 0) >> row_shift) == (_iota(t.shape, 1) >> col_shift)
    return jnp.where(keep, t, 0.0)


def _bd64(x):
    return _block_diag(x, 6, 6)


def _attention(q, k_cur, v_cur, k_prev, v_prev, key_offset, sinks_ref):
    kb = jnp.concatenate([k_prev, k_cur], axis=0)
    vb = jnp.concatenate([v_prev, v_cur], axis=0)
    lo_kv = _iota(kb.shape, 1) < HEAD_DIM
    kr = pltpu.roll(kb, HEAD_DIM, axis=1)
    vr = pltpu.roll(vb, HEAD_DIM, axis=1)
    k_dup = (jnp.where(lo_kv, kb, kr), jnp.where(lo_kv, kr, kb))
    v_dup = (jnp.where(lo_kv, vb, vr), jnp.where(lo_kv, vr, vb))

    row = _iota((SUB, 2 * SUB), 0)
    col = _iota((SUB, 2 * SUB), 1)
    dist = row - col + SUB
    valid = (dist >= 0) & (dist < WINDOW) & (col >= key_offset)
    distf = dist.astype(F32)
    lo_q = _iota((SUB, 128), 1) < HEAD_DIM
    group = ATTN_Q_HEADS // ATTN_KV_HEADS

    outs = []
    for hk in range(ATTN_KV_HEADS):
        base = hk * group * HEAD_DIM
        slab_a = q[:, base:base + 128]
        slab_b = q[:, base + 128:base + 256]
        lhs = jnp.concatenate([jnp.where(lo_q, slab_a, 0.0), jnp.where(lo_q, 0.0, slab_a),
                               jnp.where(lo_q, slab_b, 0.0), jnp.where(lo_q, 0.0, slab_b)], axis=0)
        s_all = _mm_nt(lhs, k_dup[hk])
        ps, invs = [], []
        for g in range(group):
            hq = hk * group + g
            slope = 2.0 ** (-8.0 * (hq + 1) / ATTN_Q_HEADS)
            s = s_all[g * SUB:(g + 1) * SUB] * (HEAD_DIM ** -0.5) - slope * distf
            s = jnp.where(valid, s, -jnp.inf)
            sink = sinks_ref[hq]
            m = jnp.maximum(jnp.max(s, axis=-1, keepdims=True), sink)
            p = jnp.exp(s - m)
            den = jnp.sum(p, axis=-1, keepdims=True) + jnp.exp(sink - m)
            ps.append(p)
            invs.append(1.0 / den)
        o_all = _mm(jnp.concatenate(ps, axis=0), v_dup[hk])
        o = [o_all[g * SUB:(g + 1) * SUB] * invs[g] for g in range(group)]
        outs.append(jnp.where(lo_q, o[0], o[1]))
        outs.append(jnp.where(lo_q, o[2], o[3]))
    return jnp.concatenate(outs, axis=1)


def _rwkv_chunk_intra(at, bt, kt, rt, v):
    row = _iota((CH, RWKV_WIDTH), 0)
    col = _iota((CH, RWKV_WIDTH), 1) & (HEAD_DIM - 1)
    strict = col < row
    incl = col <= row

    lhs = jnp.concatenate([at, rt], axis=0)
    g_b = _mm_nt(lhs, _bd64(bt))
    g_k = _mm_nt(lhs, _bd64(kt))
    a_ab = jnp.where(strict, g_b[:CH], 0.0)
    a_rb = jnp.where(incl, g_b[CH:], 0.0)
    a_ak = jnp.where(strict, g_k[:CH], 0.0)
    a_rk = jnp.where(incl, g_k[CH:], 0.0)

    p = _mm(a_ab, _bd64(a_ab))
    w = jnp.where(col == row, 1.0, 0.0) + a_ab
    for _ in range(N_LEVELS - 2):
        r = _mm(jnp.concatenate([p, w], axis=0), _bd64(p))
        w = w + r[CH:]
        p = r[:CH]
    t_inv = w + _mm(w, _bd64(p))

    akv = _mm(a_ak, _bd64(v))
    at2 = _mm(t_inv, _bd64(at))
    uv = _mm(t_inv, _bd64(akv))
    return a_rb, a_rk, at2, uv


def _rwkv_chunk_state(a_rb, a_rk, at2, uv, rt, v, bp, kp, gam, st):
    x = _mm_nt(jnp.concatenate([at2, rt], axis=0), st)
    u = uv + x[:CH]
    y = x[CH:] + _mm(a_rb, _bd64(u)) + _mm(a_rk, _bd64(v))
    upd = _mm_tn(jnp.concatenate([u, v], axis=0), jnp.concatenate([bp, kp], axis=0))
    keep = (_iota(upd.shape, 0) >> 6) == (_iota(upd.shape, 1) >> 6)
    st_new = gam * st + jnp.where(keep, upd, 0.0)
    return y, st_new


def _gla_chunk(qs, k, v, g, mg, lv, st):
    e_all = _sel_mm(mg, g, 3)
    b = e_all[:CH]
    keep_k = (_iota((4 * CH, GLA_KEY_WIDTH), 0) >> 6) == (_iota((4 * CH, GLA_KEY_WIDTH), 1) >> 5)

    def level_scores(ql, kl):
        kt = jnp.concatenate([kl, kl, kl, kl], axis=0)
        return _mm_nt(ql, jnp.where(keep_k, kt, 0.0))

    a = jnp.where(lv == 0, level_scores(qs, k), 0.0)
    for l in range(1, N_LEVELS + 1):
        f = jnp.exp(e_all[l * CH:(l + 1) * CH])
        a = jnp.where(lv == l, level_scores(qs * f, k * f), a)

    o_intra = _mm(a, _bd64(v))
    b_last = b[CH - 1:CH]
    o_inter = _mm_nt(qs * jnp.exp(b), st)
    upd = _mm_tn(v, k * jnp.exp(b_last - b))
    keep_s = (_iota(upd.shape, 0) >> 6) == (_iota(upd.shape, 1) >> 5)
    st_new = jnp.exp(b_last) * st + jnp.where(keep_s, upd, 0.0)
    return o_intra + o_inter, st_new


def _layer_kernel(sinks_ref, x_ref, mod_ref, gpre_ref, gpost_ref, wall_ref, wout_ref, w2cat_ref, gk2_ref,
                  mu_ref, rv_ref, gkb_ref, mg_ref, lv_ref, lt2_ref, ones_ref,
                  out_ref,
                  proj_ref, mix_ref, srt_ref, sgt_ref, kprev_ref, vprev_ref, prev_ref):
    i = pl.program_id(0)

    @pl.when(i == 0)
    def _init():
        srt_ref[...] = jnp.zeros(srt_ref.shape, F32)
        sgt_ref[...] = jnp.zeros(sgt_ref.shape, F32)
        kprev_ref[...] = jnp.zeros(kprev_ref.shape, F32)
        vprev_ref[...] = jnp.zeros(vprev_ref.shape, F32)
        prev_ref[...] = jnp.zeros(prev_ref.shape, F32)

    shift = mod_ref[:, 0:D_MODEL]
    scale = mod_ref[:, D_MODEL:2 * D_MODEL]
    gate = mod_ref[:, 2 * D_MODEL:3 * D_MODEL]

    x = x_ref[...]
    ms = jnp.mean(x * x, axis=-1, keepdims=True)
    h = ((x * lax.rsqrt(ms + NORM_EPS)) * gpre_ref[...]) * (1.0 + scale) + shift
    proj_ref[...] = jnp.dot(h.astype(MXU_DTYPE), wall_ref[...], preferred_element_type=F32)

    def sub_block(j, carry):
        r0 = pl.multiple_of(j * SUB, SUB)
        rows = pl.ds(r0, SUB)

        key_offset = jnp.where(jnp.logical_and(i == 0, j == 0), SUB, 0)
        k_cur = proj_ref[rows, O_AK:O_AK + KV_WIDTH]
        v_cur = proj_ref[rows, O_AV:O_AV + KV_WIDTH]
        attn = _attention(proj_ref[rows, O_AQ:O_AQ + ATTN_WIDTH], k_cur, v_cur,
                          kprev_ref[...], vprev_ref[...], key_offset, sinks_ref)
        kprev_ref[...] = k_cur
        vprev_ref[...] = v_cur
        mix_ref[rows, 0:ATTN_WIDTH] = (attn * _silu(proj_ref[rows, O_AG:O_AG + ATTN_WIDTH])).astype(mix_ref.dtype)

        cat = jnp.concatenate([proj_ref[rows, O_RKV:O_RKV + 3 * RWKV_WIDTH], proj_ref[rows, O_X2:O_X2 + 128]],
                              axis=1)
        rolled = pltpu.roll(cat, 1, axis=0)
        shifted = jnp.where(_iota(cat.shape, 0) == 0, prev_ref[0:1, :], rolled)
        prev_ref[0:1, :] = cat[SUB - 1:SUB, :]
        rkv = cat[:, :3 * RWKV_WIDTH]
        rkv = rkv + (shifted[:, :3 * RWKV_WIDTH] - rkv) * mu_ref[...]
        r = rkv[:, 0:RWKV_WIDTH]
        k = rkv[:, RWKV_WIDTH:2 * RWKV_WIDTH]
        v = rkv[:, 2 * RWKV_WIDTH:3 * RWKV_WIDTH]

        lin = proj_ref[rows, O_X1:O_X1 + 128] + shifted[:, 3 * RWKV_WIDTH:]
        z = jnp.where(_iota(lin.shape, 1) < LORA, jnp.tanh(lin), lin)
        lora = _mm_hi(z, w2cat_ref[...])
        w0, a0 = rv_ref[0:1, :], rv_ref[1:2, :]
        k_k, k_a, r_k = rv_ref[2:3, :], rv_ref[3:4, :], rv_ref[4:5, :]
        ln_w, ln_b, gnw = rv_ref[5:6, :], rv_ref[6:7, :], rv_ref[7:8, :]

        logw = -_softplus(-(lora[:, :RWKV_WIDTH] + w0)) - 0.5
        ld = -jnp.exp(logw)
        iclr = _sigmoid(lora[:, RWKV_WIDTH:] + a0)
        ones_bd = ones_ref[...]
        kk = k * k_k
        kk = kk / jnp.maximum(jnp.sqrt(_mm_sel(kk * kk, ones_bd)), 1e-12)
        kmod = k * (1.0 + (iclr - 1.0) * k_a)
        av = -kk
        bv = kk * iclr

        c = _sel_mm(lt2_ref[...], ld, 3)
        c_last = jnp.where(_iota(c.shape, 0) < CH, c[CH - 1:CH, :], c[2 * CH - 1:2 * CH, :])
        e_neg = jnp.exp(-c)
        e_rem = jnp.exp(c_last - c)
        at = av * jnp.exp(c - ld)
        bt = bv * e_neg
        kt = kmod * e_neg
        rt = r * jnp.exp(c)
        bp = bv * e_rem
        kp = kmod * e_rem
        gam = jnp.exp(c_last)

        ys = []
        st = srt_ref[...]
        for cc in range(SUB // CH):
            sl = slice(cc * CH, (cc + 1) * CH)
            a_rb, a_rk, at2, uv = _rwkv_chunk_intra(at[sl], bt[sl], kt[sl], rt[sl], v[sl])
            y_c, st = _rwkv_chunk_state(a_rb, a_rk, at2, uv, rt[sl], v[sl], bp[sl], kp[sl],
                                        gam[cc * CH:cc * CH + 1], st)
            ys.append(y_c)
        srt_ref[...] = st
        y = jnp.concatenate(ys, axis=0)

        mean = _mm_sel(y, ones_bd) * (1.0 / HEAD_DIM)
        yc = y - mean
        var = _mm_sel(yc * yc, ones_bd) * (1.0 / HEAD_DIM)
        yn = (yc * lax.rsqrt(var + RWKV_LN_EPS)) * ln_w + ln_b
        bonus = _mm_sel(r * kmod * r_k, ones_bd) * v
        rwkv = (yn + bonus) * _silu(proj_ref[rows, O_RG:O_RG + RWKV_WIDTH])
        mix_ref[rows, ATTN_WIDTH:ATTN_WIDTH + RWKV_WIDTH] = rwkv.astype(mix_ref.dtype)

        graw = _mm_hi(proj_ref[rows, O_X3:O_X3 + 128], gk2_ref[...]) + gkb_ref[...]
        g = -_softplus(-graw) * (1.0 / GLA_GATE_NORMALIZER)
        qs = proj_ref[rows, O_GQ:O_GQ + GLA_KEY_WIDTH] * (GLA_KEY_DIM ** -0.5)
        gk = proj_ref[rows, O_GK:O_GK + GLA_KEY_WIDTH]
        gv = proj_ref[rows, O_GV:O_GV + GLA_VAL_WIDTH]
        os_ = []
        sg = sgt_ref[...]
        for cc in range(SUB // CH):
            sl = slice(cc * CH, (cc + 1) * CH)
            o_c, sg = _gla_chunk(qs[sl], gk[sl], gv[sl], g[sl], mg_ref[...], lv_ref[...], sg)
            os_.append(o_c)
        sgt_ref[...] = sg
        o = jnp.concatenate(os_, axis=0)
        oms = _mm_sel(o * o, ones_bd) * (1.0 / HEAD_DIM)
        gla = (o * lax.rsqrt(oms + 1e-5)) * gnw * _silu(proj_ref[rows, O_GG:O_GG + GLA_VAL_WIDTH])
        mix_ref[rows, ATTN_WIDTH + RWKV_WIDTH:D_MODEL] = gla.astype(mix_ref.dtype)
        return carry

    lax.fori_loop(0, TM // SUB, sub_block, 0)

    y = jnp.dot(mix_ref[...], wout_ref[...], preferred_element_type=F32)
    ms2 = jnp.mean(y * y, axis=-1, keepdims=True)
    out_ref[...] = x_ref[...] + gate * ((y * lax.rsqrt(ms2 + NORM_EPS)) * gpost_ref[...])


def _ada_kernel(c_ref, w_ref, b_ref, o_ref):
    c = c_ref[...]
    c_act = jnp.broadcast_to(_silu(c), (8, D_MODEL))
    o_ref[0] = _mm_hi(c_act, w_ref[0])[0:1] + b_ref[0]


def _adaln(c, ada_w, ada_b):
    depth = ada_w.shape[0]
    tn = D_MODEL
    return pl.pallas_call(
        _ada_kernel,
        grid=(depth, 3 * D_MODEL // tn),
        in_specs=[pl.BlockSpec((1, D_MODEL), lambda l, n: (0, 0)),
                  pl.BlockSpec((1, D_MODEL, tn), lambda l, n: (l, 0, n)),
                  pl.BlockSpec((1, 1, tn), lambda l, n: (l, 0, n))],
        out_specs=pl.BlockSpec((1, 1, tn), lambda l, n: (l, 0, n)),
        out_shape=jax.ShapeDtypeStruct((depth, 1, 3 * D_MODEL), F32),
        compiler_params=pltpu.CompilerParams(dimension_semantics=("arbitrary", "arbitrary"),
                                             vmem_limit_bytes=VMEM_LIMIT_BYTES),
        name="adaln",
    )(c, ada_w, ada_b.reshape(depth, 1, 3 * D_MODEL))


def _gla_level_matrices():
    mg = np.zeros(((N_LEVELS + 1) * CH, CH), np.float32)
    for t in range(CH):
        mg[t, :t + 1] = 1.0
    for l in range(1, N_LEVELS + 1):
        s = 2 ** (l - 1)
        for t in range(CH):
            m = (t // (2 * s)) * 2 * s + s
            if t >= m:
                mg[l * CH + t, m:t + 1] = 1.0
            else:
                mg[l * CH + t, t + 1:m] = 1.0
    lvl = np.full((CH, CH), -1, np.int32)
    for t in range(CH):
        lvl[t, t] = 0
        for s_ in range(t):
            lvl[t, s_] = int(np.floor(np.log2(t ^ s_))) + 1
    return mg, np.tile(lvl, (1, GLA_HEADS))


def _constants():
    mg, lv = _gla_level_matrices()
    tri = np.tril(np.ones((CH, CH), np.float32))
    lt2 = np.kron(np.eye(SUB // CH, dtype=np.float32), tri)
    ones_bd = np.kron(np.eye(RWKV_WIDTH // HEAD_DIM, dtype=np.float32), np.ones((HEAD_DIM, HEAD_DIM), np.float32))
    return (jnp.asarray(mg, MXU_DTYPE), jnp.asarray(lv), jnp.asarray(lt2, MXU_DTYPE), jnp.asarray(ones_bd, MXU_DTYPE))


def _layer_call(x2d, sinks, mod, gpre, gpost, wall, wout, w2cat, gk2p, mu, rv, gkb, consts):
    t = x2d.shape[0]
    mg, lv, lt2, ones_bd = consts

    def full(a):
        return pl.BlockSpec(a.shape, lambda i: (0,) * a.ndim)

    row_spec = pl.BlockSpec((TM, D_MODEL), lambda i: (i, 0))
    operands = (mod, gpre, gpost, wall, wout, w2cat, gk2p, mu, rv, gkb, mg, lv, lt2, ones_bd)
    return pl.pallas_call(
        _layer_kernel,
        grid=(t // TM,),
        in_specs=[pl.BlockSpec(memory_space=pltpu.SMEM), row_spec] + [full(a) for a in operands],
        out_specs=row_spec,
        out_shape=jax.ShapeDtypeStruct((t, D_MODEL), F32),
        scratch_shapes=[
            pltpu.VMEM((TM, AUG_WIDTH), F32),
            pltpu.VMEM((TM, D_MODEL), MXU_DTYPE),
            pltpu.VMEM((RWKV_WIDTH, RWKV_WIDTH), F32),
            pltpu.VMEM((GLA_VAL_WIDTH, GLA_KEY_WIDTH), F32),
            pltpu.VMEM((SUB, KV_WIDTH), F32),
            pltpu.VMEM((SUB, KV_WIDTH), F32),
            pltpu.VMEM((8, 3 * RWKV_WIDTH + 128), F32),
        ],
        compiler_params=pltpu.CompilerParams(dimension_semantics=("arbitrary",),
                                             vmem_limit_bytes=VMEM_LIMIT_BYTES),
        name="hybrid_layer",
    )(sinks, x2d, *operands)


def kernel(x, c, ada_w, ada_b, norm_pre, norm_post, w_in, w_out, attn_sinks, rwkv_mu_rkv, rwkv_mu_w, rwkv_mu_a, rwkv_w0, rwkv_w1, rwkv_w2, rwkv_a0, rwkv_a1, rwkv_a2, rwkv_k_k, rwkv_k_a, rwkv_r_k, rwkv_ln_w, rwkv_ln_b, gla_gk1, gla_gk2, gla_gk_b, gla_norm_w):
    batch, seq, _ = x.shape
    depth = w_in.shape[0]
    assert batch == 1 and seq % TM == 0

    mods = _adaln(c, ada_w, ada_b)

    mu_w = rwkv_mu_w[:, :, None]
    mu_a = rwkv_mu_a[:, :, None]
    x1 = jnp.concatenate([(1.0 - mu_w) * rwkv_w1, (1.0 - mu_a) * rwkv_a1], axis=-1)
    x2 = jnp.concatenate([mu_w * rwkv_w1, mu_a * rwkv_a1], axis=-1)
    x3 = jnp.pad(gla_gk1, ((0, 0), (0, 0), (0, 128 - GLA_GATE_LORA)))
    wall = jnp.concatenate([w_in, x1, x2, x3], axis=-1).astype(MXU_DTYPE)
    wout = w_out.astype(MXU_DTYPE)
    zeros = jnp.zeros((depth, LORA, RWKV_WIDTH), F32)
    w2cat = jnp.concatenate([jnp.concatenate([rwkv_w2, zeros], axis=-1),
                             jnp.concatenate([zeros, rwkv_a2], axis=-1)], axis=1)
    gk2p = jnp.pad(gla_gk2, ((0, 0), (0, 128 - GLA_GATE_LORA), (0, 0)))
    rv = jnp.stack([rwkv_w0, rwkv_a0, rwkv_k_k, rwkv_k_a, rwkv_r_k.reshape(depth, RWKV_WIDTH),
                    rwkv_ln_w, rwkv_ln_b, jnp.tile(gla_norm_w, (1, GLA_HEADS))], axis=1)
    consts = _constants()

    xc = x[0]
    for l in range(depth):
        xc = _layer_call(xc, attn_sinks[l], mods[l], norm_pre[l][None], norm_post[l][None],
                         wall[l], wout[l], w2cat[l], gk2p[l], rwkv_mu_rkv[l][None], rv[l],
                         gla_gk_b[l][None], consts)
    return xc[None]
```

```python
import numpy as np
import jax
import jax.numpy as jnp
from jax import lax
from jax.experimental import pallas as pl
from jax.experimental.pallas import tpu as pltpu

F32 = jnp.float32
MXU_DTYPE = jnp.bfloat16

D_MODEL = 1024
HEAD_DIM = 64
ATTN_WIDTH = 512
ATTN_Q_HEADS = 8
ATTN_KV_HEADS = 2
KV_WIDTH = 128
WINDOW = 128
RWKV_WIDTH = 256
RWKV_LN_EPS = 64e-5
LORA = 64
GLA_HEADS = 4
GLA_KEY_WIDTH = 128
GLA_KEY_DIM = 32
GLA_VAL_WIDTH = 256
GLA_GATE_LORA = 16
GLA_GATE_NORMALIZER = 16.0
NORM_EPS = 1e-6
IN_WIDTH = 3072

O_AQ, O_AK, O_AV, O_AG = 0, 512, 640, 768
O_RKV, O_RG = 1280, 2048
O_GQ, O_GK, O_GV, O_GG = 2304, 2432, 2560, 2816
O_X1, O_X2, O_X3 = 3072, 3200, 3328
AUG_WIDTH = 3456

TM = 512
SUB = 128
CH = 64
N_LEVELS = 6

VMEM_LIMIT_BYTES = 60000 * 1024


def _mm(a, b):
    return jnp.dot(a.astype(MXU_DTYPE), b.astype(MXU_DTYPE), preferred_element_type=F32)


def _mm_nt(a, b):
    return lax.dot_general(a.astype(MXU_DTYPE), b.astype(MXU_DTYPE), (((1,), (1,)), ((), ())),
                           preferred_element_type=F32)


def _mm_tn(a, b):
    return lax.dot_general(a.astype(MXU_DTYPE), b.astype(MXU_DTYPE), (((0,), (0,)), ((), ())),
                           preferred_element_type=F32)


def _split(x, n):
    parts, r = [], x
    for _ in range(n):
        p = r.astype(MXU_DTYPE)
        parts.append(p)
        r = r - p.astype(F32)
    return parts


def _sel_mm(sel, x, n=3):
    acc = None
    for p in _split(x, n):
        t = jnp.dot(sel, p, preferred_element_type=F32)
        acc = t if acc is None else acc + t
    return acc


def _mm_sel(x, sel, n=2):
    acc = None
    for p in _split(x, n):
        t = jnp.dot(p, sel, preferred_element_type=F32)
        acc = t if acc is None else acc + t
    return acc


def _mm_hi(a, b):
    ah, al = _split(a, 2)
    bh, bl = _split(b, 2)
    return (jnp.dot(ah, bh, preferred_element_type=F32) + jnp.dot(ah, bl, preferred_element_type=F32)
            + jnp.dot(al, bh, preferred_element_type=F32))


def _iota(shape, dim):
    return lax.broadcasted_iota(jnp.int32, shape, dim)


def _sigmoid(z):
    return 1.0 / (1.0 + jnp.exp(-z))


def _silu(z):
    return z * _sigmoid(z)


def _softplus(z):
    return jnp.maximum(z, 0.0) + jnp.log(1.0 + jnp.exp(-jnp.abs(z)))


def _block_diag(x, row_shift, col_shift):
    t = jnp.concatenate([x, x, x, x], axis=0)
    keep = (_iota(t.shape, 0) >> row_shift) == (_iota(t.shape, 1) >> col_shift)
    return jnp.where(keep, t, 0.0)


def _bd64(x):
    return _block_diag(x, 6, 6)


def _interleave(streams):
    results = [None] * len(streams)
    live = list(enumerate(streams))
    while live:
        still = []
        for idx, gen in live:
            try:
                next(gen)
                still.append((idx, gen))
            except StopIteration as stop:
                results[idx] = stop.value
        live = still
    return results


def _attention_setup(k_cur, v_cur, k_prev, v_prev, key_offset):
    kb = jnp.concatenate([k_prev, k_cur], axis=0)
    vb = jnp.concatenate([v_prev, v_cur], axis=0)
    lo_kv = _iota(kb.shape, 1) < HEAD_DIM
    kr = pltpu.roll(kb, HEAD_DIM, axis=1)
    vr = pltpu.roll(vb, HEAD_DIM, axis=1)
    k_dup = (jnp.where(lo_kv, kb, kr), jnp.where(lo_kv, kr, kb))
    v_dup = (jnp.where(lo_kv, vb, vr), jnp.where(lo_kv, vr, vb))
    row = _iota((SUB, 2 * SUB), 0)
    col = _iota((SUB, 2 * SUB), 1)
    dist = row - col + SUB
    valid = (dist >= 0) & (dist < WINDOW) & (col >= key_offset)
    return k_dup, v_dup, valid, dist.astype(F32)


def _attention_group(hk, q, k_dup, v_dup, valid, distf, sinks_ref):
    lo_q = _iota((SUB, 128), 1) < HEAD_DIM
    group = ATTN_Q_HEADS // ATTN_KV_HEADS
    base = hk * group * HEAD_DIM
    slab_a = q[:, base:base + 128]
    slab_b = q[:, base + 128:base + 256]
    lhs = jnp.concatenate([jnp.where(lo_q, slab_a, 0.0), jnp.where(lo_q, 0.0, slab_a),
                           jnp.where(lo_q, slab_b, 0.0), jnp.where(lo_q, 0.0, slab_b)], axis=0)
    s_all = _mm_nt(lhs, k_dup[hk])
    yield
    ps, invs = [], []
    for g in range(group):
        hq = hk * group + g
        slope = 2.0 ** (-8.0 * (hq + 1) / ATTN_Q_HEADS)
        s = s_all[g * SUB:(g + 1) * SUB] * (HEAD_DIM ** -0.5) - slope * distf
        s = jnp.where(valid, s, -jnp.inf)
        sink = sinks_ref[hq]
        m = jnp.maximum(jnp.max(s, axis=-1, keepdims=True), sink)
        p = jnp.exp(s - m)
        den = jnp.sum(p, axis=-1, keepdims=True) + jnp.exp(sink - m)
        ps.append(p)
        invs.append(1.0 / den)
    o_all = _mm(jnp.concatenate(ps, axis=0), v_dup[hk])
    yield
    o = [o_all[g * SUB:(g + 1) * SUB] * invs[g] for g in range(group)]
    return jnp.concatenate([jnp.where(lo_q, o[0], o[1]), jnp.where(lo_q, o[2], o[3])], axis=1)


def _rwkv_chunk_intra(at, bt, kt, rt, v):
    row = _iota((CH, RWKV_WIDTH), 0)
    col = _iota((CH, RWKV_WIDTH), 1) & (HEAD_DIM - 1)
    strict = col < row
    incl = col <= row

    lhs = jnp.concatenate([at, rt], axis=0)
    g_b = _mm_nt(lhs, _bd64(bt))
    g_k = _mm_nt(lhs, _bd64(kt))
    yield
    a_ab = jnp.where(strict, g_b[:CH], 0.0)
    a_rb = jnp.where(incl, g_b[CH:], 0.0)
    a_ak = jnp.where(strict, g_k[:CH], 0.0)
    a_rk = jnp.where(incl, g_k[CH:], 0.0)

    p = _mm(a_ab, _bd64(a_ab))
    akv = _mm(a_ak, _bd64(v))
    w = jnp.where(col == row, 1.0, 0.0) + a_ab
    yield
    for _ in range(N_LEVELS - 2):
        r = _mm(jnp.concatenate([p, w], axis=0), _bd64(p))
        yield
        w = w + r[CH:]
        p = r[:CH]
    t_inv = w + _mm(w, _bd64(p))
    yield
    at2 = _mm(t_inv, _bd64(at))
    uv = _mm(t_inv, _bd64(akv))
    yield
    return a_rb, a_rk, at2, uv


def _rwkv_chunk_state(a_rb, a_rk, at2, uv, rt, v, bp, kp, gam, st):
    x = _mm_nt(jnp.concatenate([at2, rt], axis=0), st)
    u = uv + x[:CH]
    y = x[CH:] + _mm(a_rb, _bd64(u)) + _mm(a_rk, _bd64(v))
    upd = _mm_tn(jnp.concatenate([u, v], axis=0), jnp.concatenate([bp, kp], axis=0))
    keep = (_iota(upd.shape, 0) >> 6) == (_iota(upd.shape, 1) >> 6)
    st_new = gam * st + jnp.where(keep, upd, 0.0)
    return y, st_new


def _gla_chunk_intra(qs, k, v, g, mg, lv):
    e_all = _sel_mm(mg, g, 3)
    yield
    b = e_all[:CH]
    keep_k = (_iota((4 * CH, GLA_KEY_WIDTH), 0) >> 6) == (_iota((4 * CH, GLA_KEY_WIDTH), 1) >> 5)

    def level_scores(ql, kl):
        kt = jnp.concatenate([kl, kl, kl, kl], axis=0)
        return _mm_nt(ql, jnp.where(keep_k, kt, 0.0))

    scores = [level_scores(qs, k)]
    for l in range(1, N_LEVELS + 1):
        f = jnp.exp(e_all[l * CH:(l + 1) * CH])
        scores.append(level_scores(qs * f, k * f))
    b_last = b[CH - 1:CH]
    upd = _mm_tn(v, k * jnp.exp(b_last - b))
    yield
    a = jnp.where(lv == 0, scores[0], 0.0)
    for l in range(1, N_LEVELS + 1):
        a = jnp.where(lv == l, scores[l], a)
    o_intra = _mm(a, _bd64(v))
    keep_s = (_iota(upd.shape, 0) >> 6) == (_iota(upd.shape, 1) >> 5)
    yield
    return o_intra, qs * jnp.exp(b), jnp.exp(b_last), jnp.where(keep_s, upd, 0.0)


def _gla_chunk_state(o_intra, q_dec, decay, upd, st):
    return o_intra + _mm_nt(q_dec, st), decay * st + upd


def _layer_kernel(sinks_ref, x_ref, mod_ref, gpre_ref, gpost_ref, wall_ref, wout_ref, w2cat_ref, gk2_ref,
                  mu_ref, rv_ref, gkb_ref, mg_ref, lv_ref, lt2_ref, ones_ref,
                  out_ref,
                  proj_ref, mix_ref, srt_ref, sgt_ref, kprev_ref, vprev_ref, prev_ref):
    i = pl.program_id(0)

    @pl.when(i == 0)
    def _init():
        srt_ref[...] = jnp.zeros(srt_ref.shape, F32)
        sgt_ref[...] = jnp.zeros(sgt_ref.shape, F32)
        kprev_ref[...] = jnp.zeros(kprev_ref.shape, F32)
        vprev_ref[...] = jnp.zeros(vprev_ref.shape, F32)
        prev_ref[...] = jnp.zeros(prev_ref.shape, F32)

    shift = mod_ref[:, 0:D_MODEL]
    scale = mod_ref[:, D_MODEL:2 * D_MODEL]
    gate = mod_ref[:, 2 * D_MODEL:3 * D_MODEL]

    x = x_ref[...]
    ms = jnp.mean(x * x, axis=-1, keepdims=True)
    h = ((x * lax.rsqrt(ms + NORM_EPS)) * gpre_ref[...]) * (1.0 + scale) + shift
    proj_ref[...] = jnp.dot(h.astype(MXU_DTYPE), wall_ref[...], preferred_element_type=F32)

    def sub_block(j, carry):
        rows = pl.ds(j * SUB, SUB)

        key_offset = jnp.where(jnp.logical_and(i == 0, j == 0), SUB, 0)
        k_cur = proj_ref[rows, O_AK:O_AK + KV_WIDTH]
        v_cur = proj_ref[rows, O_AV:O_AV + KV_WIDTH]
        k_dup, v_dup, valid, distf = _attention_setup(k_cur, v_cur, kprev_ref[...], vprev_ref[...], key_offset)
        kprev_ref[...] = k_cur
        vprev_ref[...] = v_cur
        q_all = proj_ref[rows, O_AQ:O_AQ + ATTN_WIDTH]

        cat = jnp.concatenate([proj_ref[rows, O_RKV:O_RKV + 3 * RWKV_WIDTH], proj_ref[rows, O_X2:O_X2 + 128]],
                              axis=1)
        rolled = pltpu.roll(cat, 1, axis=0)
        shifted = jnp.where(_iota(cat.shape, 0) == 0, prev_ref[0:1, :], rolled)
        prev_ref[0:1, :] = cat[SUB - 1:SUB, :]
        rkv = cat[:, :3 * RWKV_WIDTH]
        rkv = rkv + (shifted[:, :3 * RWKV_WIDTH] - rkv) * mu_ref[...]
        r = rkv[:, 0:RWKV_WIDTH]
        k = rkv[:, RWKV_WIDTH:2 * RWKV_WIDTH]
        v = rkv[:, 2 * RWKV_WIDTH:3 * RWKV_WIDTH]

        lin = proj_ref[rows, O_X1:O_X1 + 128] + shifted[:, 3 * RWKV_WIDTH:]
        z = jnp.where(_iota(lin.shape, 1) < LORA, jnp.tanh(lin), lin)
        lora = _mm_hi(z, w2cat_ref[...])
        w0, a0 = rv_ref[0:1, :], rv_ref[1:2, :]
        k_k, k_a, r_k = rv_ref[2:3, :], rv_ref[3:4, :], rv_ref[4:5, :]
        ln_w, ln_b, gnw = rv_ref[5:6, :], rv_ref[6:7, :], rv_ref[7:8, :]

        logw = -_softplus(-(lora[:, :RWKV_WIDTH] + w0)) - 0.5
        ld = -jnp.exp(logw)
        iclr = _sigmoid(lora[:, RWKV_WIDTH:] + a0)
        ones_bd = ones_ref[...]
        kk = k * k_k
        kk = kk / jnp.maximum(jnp.sqrt(_mm_sel(kk * kk, ones_bd)), 1e-12)
        kmod = k * (1.0 + (iclr - 1.0) * k_a)
        av = -kk
        bv = kk * iclr

        c = _sel_mm(lt2_ref[...], ld, 3)
        c_last = jnp.where(_iota(c.shape, 0) < CH, c[CH - 1:CH, :], c[2 * CH - 1:2 * CH, :])
        e_neg = jnp.exp(-c)
        e_rem = jnp.exp(c_last - c)
        at = av * jnp.exp(c - ld)
        bt = bv * e_neg
        kt = kmod * e_neg
        rt = r * jnp.exp(c)
        bp = bv * e_rem
        kp = kmod * e_rem
        gam = jnp.exp(c_last)

        graw = _mm_hi(proj_ref[rows, O_X3:O_X3 + 128], gk2_ref[...]) + gkb_ref[...]
        g = -_softplus(-graw) * (1.0 / GLA_GATE_NORMALIZER)
        qs = proj_ref[rows, O_GQ:O_GQ + GLA_KEY_WIDTH] * (GLA_KEY_DIM ** -0.5)
        gk = proj_ref[rows, O_GK:O_GK + GLA_KEY_WIDTH]
        gv = proj_ref[rows, O_GV:O_GV + GLA_VAL_WIDTH]

        chunks = [slice(cc * CH, (cc + 1) * CH) for cc in range(SUB // CH)]
        streams = [_attention_group(hk, q_all, k_dup, v_dup, valid, distf, sinks_ref)
                   for hk in range(ATTN_KV_HEADS)]
        streams += [_rwkv_chunk_intra(at[sl], bt[sl], kt[sl], rt[sl], v[sl]) for sl in chunks]
        streams += [_gla_chunk_intra(qs[sl], gk[sl], gv[sl], g[sl], mg_ref[...], lv_ref[...]) for sl in chunks]
        done = _interleave(streams)
        attn = jnp.concatenate(done[:ATTN_KV_HEADS], axis=1)
        rwkv_intra = done[ATTN_KV_HEADS:ATTN_KV_HEADS + len(chunks)]
        gla_intra = done[ATTN_KV_HEADS + len(chunks):]
        mix_ref[rows, 0:ATTN_WIDTH] = (attn * _silu(proj_ref[rows, O_AG:O_AG + ATTN_WIDTH])).astype(mix_ref.dtype)

        ys, os_ = [], []
        st = srt_ref[...]
        sg = sgt_ref[...]
        for cc, sl in enumerate(chunks):
            a_rb, a_rk, at2, uv = rwkv_intra[cc]
            y_c, st = _rwkv_chunk_state(a_rb, a_rk, at2, uv, rt[sl], v[sl], bp[sl], kp[sl],
                                        gam[cc * CH:cc * CH + 1], st)
            ys.append(y_c)
            o_c, sg = _gla_chunk_state(*gla_intra[cc], sg)
            os_.append(o_c)
        srt_ref[...] = st
        sgt_ref[...] = sg
        y = jnp.concatenate(ys, axis=0)
        o = jnp.concatenate(os_, axis=0)

        mean = _mm_sel(y, ones_bd) * (1.0 / HEAD_DIM)
        yc = y - mean
        var = _mm_sel(yc * yc, ones_bd) * (1.0 / HEAD_DIM)
        yn = (yc * lax.rsqrt(var + RWKV_LN_EPS)) * ln_w + ln_b
        bonus = _mm_sel(r * kmod * r_k, ones_bd) * v
        rwkv = (yn + bonus) * _silu(proj_ref[rows, O_RG:O_RG + RWKV_WIDTH])
        mix_ref[rows, ATTN_WIDTH:ATTN_WIDTH + RWKV_WIDTH] = rwkv.astype(mix_ref.dtype)

        oms = _mm_sel(o * o, ones_bd) * (1.0 / HEAD_DIM)
        gla = (o * lax.rsqrt(oms + 1e-5)) * gnw * _silu(proj_ref[rows, O_GG:O_GG + GLA_VAL_WIDTH])
        mix_ref[rows, ATTN_WIDTH + RWKV_WIDTH:D_MODEL] = gla.astype(mix_ref.dtype)
        return carry

    for j in range(TM // SUB):
        sub_block(j, 0)

    y = jnp.dot(mix_ref[...], wout_ref[...], preferred_element_type=F32)
    ms2 = jnp.mean(y * y, axis=-1, keepdims=True)
    out_ref[...] = x_ref[...] + gate * ((y * lax.rsqrt(ms2 + NORM_EPS)) * gpost_ref[...])


def _ada_kernel(c_ref, w_ref, b_ref, o_ref):
    c = c_ref[...]
    c_act = jnp.broadcast_to(_silu(c), (8, D_MODEL))
    o_ref[0] = _mm_hi(c_act, w_ref[0])[0:1] + b_ref[0]


def _adaln(c, ada_w, ada_b):
    depth = ada_w.shape[0]
    tn = D_MODEL
    return pl.pallas_call(
        _ada_kernel,
        grid=(depth, 3 * D_MODEL // tn),
        in_specs=[pl.BlockSpec((1, D_MODEL), lambda l, n: (0, 0)),
                  pl.BlockSpec((1, D_MODEL, tn), lambda l, n: (l, 0, n)),
                  pl.BlockSpec((1, 1, tn), lambda l, n: (l, 0, n))],
        out_specs=pl.BlockSpec((1, 1, tn), lambda l, n: (l, 0, n)),
        out_shape=jax.ShapeDtypeStruct((depth, 1, 3 * D_MODEL), F32),
        compiler_params=pltpu.CompilerParams(dimension_semantics=("arbitrary", "arbitrary"),
                                             vmem_limit_bytes=VMEM_LIMIT_BYTES),
        name="adaln",
    )(c, ada_w, ada_b.reshape(depth, 1, 3 * D_MODEL))


def _gla_level_matrices():
    mg = np.zeros(((N_LEVELS + 1) * CH, CH), np.float32)
    for t in range(CH):
        mg[t, :t + 1] = 1.0
    for l in range(1, N_LEVELS + 1):
        s = 2 ** (l - 1)
        for t in range(CH):
            m = (t // (2 * s)) * 2 * s + s
            if t >= m:
                mg[l * CH + t, m:t + 1] = 1.0
            else:
                mg[l * CH + t, t + 1:m] = 1.0
    lvl = np.full((CH, CH), -1, np.int32)
    for t in range(CH):
        lvl[t, t] = 0
        for s_ in range(t):
            lvl[t, s_] = int(np.floor(np.log2(t ^ s_))) + 1
    return mg, np.tile(lvl, (1, GLA_HEADS))


def _constants():
    mg, lv = _gla_level_matrices()
    tri = np.tril(np.ones((CH, CH), np.float32))
    lt2 = np.kron(np.eye(SUB // CH, dtype=np.float32), tri)
    ones_bd = np.kron(np.eye(RWKV_WIDTH // HEAD_DIM, dtype=np.float32), np.ones((HEAD_DIM, HEAD_DIM), np.float32))
    return (jnp.asarray(mg, MXU_DTYPE), jnp.asarray(lv), jnp.asarray(lt2, MXU_DTYPE), jnp.asarray(ones_bd, MXU_DTYPE))


def _layer_call(x2d, sinks, mod, gpre, gpost, wall, wout, w2cat, gk2p, mu, rv, gkb, consts):
    t = x2d.shape[0]
    mg, lv, lt2, ones_bd = consts

    def full(a):
        return pl.BlockSpec(a.shape, lambda i: (0,) * a.ndim)

    row_spec = pl.BlockSpec((TM, D_MODEL), lambda i: (i, 0))
    operands = (mod, gpre, gpost, wall, wout, w2cat, gk2p, mu, rv, gkb, mg, lv, lt2, ones_bd)
    return pl.pallas_call(
        _layer_kernel,
        grid=(t // TM,),
        in_specs=[pl.BlockSpec(memory_space=pltpu.SMEM), row_spec] + [full(a) for a in operands],
        out_specs=row_spec,
        out_shape=jax.ShapeDtypeStruct((t, D_MODEL), F32),
        scratch_shapes=[
            pltpu.VMEM((TM, AUG_WIDTH), F32),
            pltpu.VMEM((TM, D_MODEL), MXU_DTYPE),
            pltpu.VMEM((RWKV_WIDTH, RWKV_WIDTH), F32),
            pltpu.VMEM((GLA_VAL_WIDTH, GLA_KEY_WIDTH), F32),
            pltpu.VMEM((SUB, KV_WIDTH), F32),
            pltpu.VMEM((SUB, KV_WIDTH), F32),
            pltpu.VMEM((8, 3 * RWKV_WIDTH + 128), F32),
        ],
        compiler_params=pltpu.CompilerParams(dimension_semantics=("arbitrary",),
                                             vmem_limit_bytes=VMEM_LIMIT_BYTES),
        name="hybrid_layer",
    )(sinks, x2d, *operands)


def kernel(x, c, ada_w, ada_b, norm_pre, norm_post, w_in, w_out, attn_sinks, rwkv_mu_rkv, rwkv_mu_w, rwkv_mu_a, rwkv_w0, rwkv_w1, rwkv_w2, rwkv_a0, rwkv_a1, rwkv_a2, rwkv_k_k, rwkv_k_a, rwkv_r_k, rwkv_ln_w, rwkv_ln_b, gla_gk1, gla_gk2, gla_gk_b, gla_norm_w):
    batch, seq, _ = x.shape
    depth = w_in.shape[0]
    assert batch == 1 and seq % TM == 0

    mods = _adaln(c, ada_w, ada_b)

    mu_w = rwkv_mu_w[:, :, None]
    mu_a = rwkv_mu_a[:, :, None]
    x1 = jnp.concatenate([(1.0 - mu_w) * rwkv_w1, (1.0 - mu_a) * rwkv_a1], axis=-1)
    x2 = jnp.concatenate([mu_w * rwkv_w1, mu_a * rwkv_a1], axis=-1)
    x3 = jnp.pad(gla_gk1, ((0, 0), (0, 0), (0, 128 - GLA_GATE_LORA)))
    wall = jnp.concatenate([w_in, x1, x2, x3], axis=-1).astype(MXU_DTYPE)
    wout = w_out.astype(MXU_DTYPE)
    zeros = jnp.zeros((depth, LORA, RWKV_WIDTH), F32)
    w2cat = jnp.concatenate([jnp.concatenate([rwkv_w2, zeros], axis=-1),
                             jnp.concatenate([zeros, rwkv_a2], axis=-1)], axis=1)
    gk2p = jnp.pad(gla_gk2, ((0, 0), (0, 128 - GLA_GATE_LORA), (0, 0)))
    rv = jnp.stack([rwkv_w0, rwkv_a0, rwkv_k_k, rwkv_k_a, rwkv_r_k.reshape(depth, RWKV_WIDTH),
                    rwkv_ln_w, rwkv_ln_b, jnp.tile(gla_norm_w, (1, GLA_HEADS))], axis=1)
    consts = _constants()

    xc = x[0]
    for l in range(depth):
        xc = _layer_call(xc, attn_sinks[l], mods[l], norm_pre[l][None], norm_post[l][None],
                         wall[l], wout[l], w2cat[l], gk2p[l], rwkv_mu_rkv[l][None], rv[l],
                         gla_gk_b[l][None], consts)
    return xc[None]
```

```python
import numpy as np
import jax
import jax.numpy as jnp
from jax import lax
from jax.experimental import pallas as pl
from jax.experimental.pallas import tpu as pltpu

F32 = jnp.float32
MXU_DTYPE = jnp.bfloat16

D_MODEL = 1024
HEAD_DIM = 64
ATTN_WIDTH = 512
ATTN_Q_HEADS = 8
ATTN_KV_HEADS = 2
KV_WIDTH = 128
WINDOW = 128
RWKV_WIDTH = 256
RWKV_LN_EPS = 64e-5
LORA = 64
GLA_HEADS = 4
GLA_KEY_WIDTH = 128
GLA_KEY_DIM = 32
GLA_VAL_WIDTH = 256
GLA_GATE_LORA = 16
GLA_GATE_NORMALIZER = 16.0
NORM_EPS = 1e-6
IN_WIDTH = 3072

O_AQ, O_AK, O_AV, O_AG = 0, 512, 640, 768
O_RKV, O_RG = 1280, 2048
O_GQ, O_GK, O_GV, O_GG = 2304, 2432, 2560, 2816
O_X1, O_X2, O_X3 = 3072, 3200, 3328
AUG_WIDTH = 3456

TM = 512
SUB = 128
CH = 64
N_LEVELS = 6
PROJ_TILE = 256

VMEM_LIMIT_BYTES = 60000 * 1024


def _mm(a, b):
    return jnp.dot(a.astype(MXU_DTYPE), b.astype(MXU_DTYPE), preferred_element_type=F32)


def _mm_nt(a, b):
    return lax.dot_general(a.astype(MXU_DTYPE), b.astype(MXU_DTYPE), (((1,), (1,)), ((), ())),
                           preferred_element_type=F32)


def _mm_tn(a, b):
    return lax.dot_general(a.astype(MXU_DTYPE), b.astype(MXU_DTYPE), (((0,), (0,)), ((), ())),
                           preferred_element_type=F32)


def _split(x, n):
    parts, r = [], x
    for _ in range(n):
        p = r.astype(MXU_DTYPE)
        parts.append(p)
        r = r - p.astype(F32)
    return parts


def _sel_mm2(sel2, x):
    hi, lo = _split(x, 2)
    return jnp.dot(sel2, jnp.concatenate([hi, lo], axis=0), preferred_element_type=F32)


def _mm_sel(x, sel, n):
    acc = None
    for p in _split(x, n):
        t = jnp.dot(p, sel, preferred_element_type=F32)
        acc = t if acc is None else acc + t
    return acc


def _mm_hi(a, b):
    ah, al = _split(a, 2)
    bh, bl = _split(b, 2)
    return (jnp.dot(ah, bh, preferred_element_type=F32) + jnp.dot(ah, bl, preferred_element_type=F32)
            + jnp.dot(al, bh, preferred_element_type=F32))


def _iota(shape, dim):
    return lax.broadcasted_iota(jnp.int32, shape, dim)


def _sigmoid(z):
    return 1.0 / (1.0 + jnp.exp(-z))


def _silu(z):
    return z * _sigmoid(z)


def _softplus(z):
    return jnp.maximum(z, 0.0) + jnp.log(1.0 + jnp.exp(-jnp.abs(z)))


def _block_diag(x, row_shift, col_shift):
    t = jnp.concatenate([x, x, x, x], axis=0)
    keep = (_iota(t.shape, 0) >> row_shift) == (_iota(t.shape, 1) >> col_shift)
    return jnp.where(keep, t, 0.0)


def _bd64(x):
    return _block_diag(x, 6, 6)


def _run_tasks(tasks):
    results, running, pending, rnd = {}, {}, list(tasks), 0
    while pending or running:
        for task in list(pending):
            name, make, deps, earliest = task
            if earliest <= rnd and all(d in results for d in deps):
                running[name] = make(results)
                pending.remove(task)
        for name in list(running):
            try:
                next(running[name])
            except StopIteration as stop:
                results[name] = stop.value
                del running[name]
        rnd += 1
        assert rnd < 1000, "task graph cannot make progress"
    return results


def _attention_setup(k_cur, v_cur, k_prev, v_prev, key_offset):
    kb = jnp.concatenate([k_prev, k_cur], axis=0)
    vb = jnp.concatenate([v_prev, v_cur], axis=0)
    lo_kv = _iota(kb.shape, 1) < HEAD_DIM
    kr = pltpu.roll(kb, HEAD_DIM, axis=1)
    vr = pltpu.roll(vb, HEAD_DIM, axis=1)
    k_dup = (jnp.where(lo_kv, kb, kr), jnp.where(lo_kv, kr, kb))
    v_dup = (jnp.where(lo_kv, vb, vr), jnp.where(lo_kv, vr, vb))
    row = _iota((SUB, 2 * SUB), 0)
    col = _iota((SUB, 2 * SUB), 1)
    dist = row - col + SUB
    valid = (dist >= 0) & (dist < WINDOW) & (col >= key_offset)
    return k_dup, v_dup, valid, dist.astype(F32)


def _attention_group(hk, q, k_dup, v_dup, valid, distf, sinks_ref):
    lo_q = _iota((SUB, 128), 1) < HEAD_DIM
    group = ATTN_Q_HEADS // ATTN_KV_HEADS
    base = hk * group * HEAD_DIM
    slab_a = q[:, base:base + 128]
    slab_b = q[:, base + 128:base + 256]
    lhs = jnp.concatenate([jnp.where(lo_q, slab_a, 0.0), jnp.where(lo_q, 0.0, slab_a),
                           jnp.where(lo_q, slab_b, 0.0), jnp.where(lo_q, 0.0, slab_b)], axis=0)
    s_all = _mm_nt(lhs, k_dup[hk])
    yield
    ps, invs = [], []
    for g in range(group):
        hq = hk * group + g
        slope = 2.0 ** (-8.0 * (hq + 1) / ATTN_Q_HEADS)
        s = s_all[g * SUB:(g + 1) * SUB] * (HEAD_DIM ** -0.5) - slope * distf
        s = jnp.where(valid, s, -jnp.inf)
        sink = sinks_ref[hq]
        m = jnp.maximum(jnp.max(s, axis=-1, keepdims=True), sink)
        p = jnp.exp(s - m)
        den = jnp.sum(p, axis=-1, keepdims=True) + jnp.exp(sink - m)
        ps.append(p)
        invs.append(1.0 / den)
    o_all = _mm(jnp.concatenate(ps, axis=0), v_dup[hk])
    yield
    o = [o_all[g * SUB:(g + 1) * SUB] * invs[g] for g in range(group)]
    return jnp.concatenate([jnp.where(lo_q, o[0], o[1]), jnp.where(lo_q, o[2], o[3])], axis=1)


def _rwkv_chunk_intra(at, bt, kt, rt, v):
    row = _iota((CH, RWKV_WIDTH), 0)
    col = _iota((CH, RWKV_WIDTH), 1) & (HEAD_DIM - 1)
    strict = col < row
    incl = col <= row

    lhs = jnp.concatenate([at, rt], axis=0)
    g_b = _mm_nt(lhs, _bd64(bt))
    g_k = _mm_nt(lhs, _bd64(kt))
    yield
    a_ab = jnp.where(strict, g_b[:CH], 0.0)
    a_rb = jnp.where(incl, g_b[CH:], 0.0)
    a_ak = jnp.where(strict, g_k[:CH], 0.0)
    a_rk = jnp.where(incl, g_k[CH:], 0.0)

    p = _mm(a_ab, _bd64(a_ab))
    akv = _mm(a_ak, _bd64(v))
    w = jnp.where(col == row, 1.0, 0.0) + a_ab
    yield
    for _ in range(N_LEVELS - 2):
        r = _mm(jnp.concatenate([p, w], axis=0), _bd64(p))
        yield
        w = w + r[CH:]
        p = r[:CH]
    t_inv = w + _mm(w, _bd64(p))
    yield
    at2 = _mm(t_inv, _bd64(at))
    uv = _mm(t_inv, _bd64(akv))
    ark_v = _mm(a_rk, _bd64(v))
    return a_rb, ark_v, at2, uv


def _rwkv_chunk_state(a_rb, ark_v, at2, uv, rt, v, bp, kp, gam, st):
    x = _mm_nt(jnp.concatenate([at2, rt], axis=0), st)
    yield
    u = uv + x[:CH]
    y = x[CH:] + ark_v + _mm(a_rb, _bd64(u))
    upd = _mm_tn(jnp.concatenate([u, v], axis=0), jnp.concatenate([bp, kp], axis=0))
    keep = (_iota(upd.shape, 0) >> 6) == (_iota(upd.shape, 1) >> 6)
    st_new = gam * st + jnp.where(keep, upd, 0.0)
    return y, st_new


def _gla_chunk_intra(qs, k, v, g, mg, lv):
    e_all = _sel_mm2(mg, g)
    yield
    b = e_all[:CH]
    keep_k = (_iota((4 * CH, GLA_KEY_WIDTH), 0) >> 6) == (_iota((4 * CH, GLA_KEY_WIDTH), 1) >> 5)

    def level_scores(ql, kl):
        kt = jnp.concatenate([kl, kl, kl, kl], axis=0)
        return _mm_nt(ql, jnp.where(keep_k, kt, 0.0))

    scores = [level_scores(qs, k)]
    for l in range(1, N_LEVELS + 1):
        f = jnp.exp(e_all[l * CH:(l + 1) * CH])
        scores.append(level_scores(qs * f, k * f))
    b_last = b[CH - 1:CH]
    upd = _mm_tn(v, k * jnp.exp(b_last - b))
    yield
    a = jnp.where(lv == 0, scores[0], 0.0)
    for l in range(1, N_LEVELS + 1):
        a = jnp.where(lv == l, scores[l], a)
    o_intra = _mm(a, _bd64(v))
    keep_s = (_iota(upd.shape, 0) >> 6) == (_iota(upd.shape, 1) >> 5)
    return o_intra, qs * jnp.exp(b), jnp.exp(b_last), jnp.where(keep_s, upd, 0.0)


def _gla_chunk_state(o_intra, q_dec, decay, upd, st):
    return o_intra + _mm_nt(q_dec, st), decay * st + upd


def _layer_kernel(sinks_ref, x_ref, xn_ref, mod_ref, gpre_ref, gpost_ref, wall_ref, wout_ref, w2a_ref, w2b_ref,
                  gk2_ref, mu_ref, rv_ref, gkb_ref, mg_ref, lv_ref, lt2_ref, ones_ref,
                  out_ref,
                  pbuf_ref, hn_ref, mix_ref, srt_ref, sgt_ref, kprev_ref, vprev_ref, prev_ref):
    i = pl.program_id(0)
    shift = mod_ref[:, 0:D_MODEL]
    scale = mod_ref[:, D_MODEL:2 * D_MODEL]
    gate = mod_ref[:, 2 * D_MODEL:3 * D_MODEL]
    g_mod = gpre_ref[...] * (1.0 + scale)

    def normed(x):
        ms = jnp.mean(x * x, axis=-1, keepdims=True)
        return ((x * lax.rsqrt(ms + NORM_EPS)) * g_mod + shift).astype(MXU_DTYPE)

    @pl.when(i == 0)
    def _first():
        srt_ref[...] = jnp.zeros(srt_ref.shape, F32)
        sgt_ref[...] = jnp.zeros(sgt_ref.shape, F32)
        kprev_ref[...] = jnp.zeros(kprev_ref.shape, F32)
        vprev_ref[...] = jnp.zeros(vprev_ref.shape, F32)
        prev_ref[...] = jnp.zeros(prev_ref.shape, F32)
        pbuf_ref[0] = jnp.dot(normed(x_ref[...]), wall_ref[...], preferred_element_type=F32)

    slot = i % 2
    proj_ref = pbuf_ref.at[slot]
    next_ref = pbuf_ref.at[1 - slot]

    def proj_task(res):
        hn_ref[...] = normed(xn_ref[...])
        yield
        for c0 in range(0, AUG_WIDTH, PROJ_TILE):
            cols = slice(c0, min(c0 + PROJ_TILE, AUG_WIDTH))
            next_ref[:, cols] = jnp.dot(hn_ref[...], wall_ref[:, cols], preferred_element_type=F32)
            yield
            yield

    n_sub = TM // SUB
    n_chunk = TM // CH
    per_sub = SUB // CH
    w0, a0 = rv_ref[0:1, :], rv_ref[1:2, :]
    k_k, k_a, r_k = rv_ref[2:3, :], rv_ref[3:4, :], rv_ref[4:5, :]
    ln_w, ln_b, gnw = rv_ref[5:6, :], rv_ref[6:7, :], rv_ref[7:8, :]

    def prep_task(j):
        def gen(res):
            rows = pl.ds(j * SUB, SUB)
            if j == 0:
                prev_row, k_prev, v_prev = prev_ref[0:1, :], kprev_ref[...], vprev_ref[...]
            else:
                before = res[("prep", j - 1)]
                prev_row, k_prev, v_prev = before["last_row"], before["k_cur"], before["v_cur"]
            k_cur = proj_ref[rows, O_AK:O_AK + KV_WIDTH]
            v_cur = proj_ref[rows, O_AV:O_AV + KV_WIDTH]
            cat = jnp.concatenate([proj_ref[rows, O_RKV:O_RKV + 3 * RWKV_WIDTH],
                                   proj_ref[rows, O_X2:O_X2 + 128]], axis=1)
            rolled = pltpu.roll(cat, 1, axis=0)
            shifted = jnp.where(_iota(cat.shape, 0) == 0, prev_row, rolled)
            last_row = cat[SUB - 1:SUB, :]
            if j == n_sub - 1:
                prev_ref[0:1, :] = last_row
                kprev_ref[...] = k_cur
                vprev_ref[...] = v_cur
            rkv = cat[:, :3 * RWKV_WIDTH]
            rkv = rkv + (shifted[:, :3 * RWKV_WIDTH] - rkv) * mu_ref[...]
            r = rkv[:, 0:RWKV_WIDTH]
            k = rkv[:, RWKV_WIDTH:2 * RWKV_WIDTH]
            v = rkv[:, 2 * RWKV_WIDTH:3 * RWKV_WIDTH]
            lin = proj_ref[rows, O_X1:O_X1 + 128] + shifted[:, 3 * RWKV_WIDTH:]
            z = jnp.where(_iota(lin.shape, 1) < LORA, jnp.tanh(lin), lin)
            z_hi, z_lo = _split(z, 2)
            lora = (jnp.dot(jnp.concatenate([z_hi, z_lo], axis=1), w2a_ref[...], preferred_element_type=F32)
                    + jnp.dot(z_hi, w2b_ref[...], preferred_element_type=F32))
            kk = k * k_k
            kk_ss = _mm_sel(kk * kk, ones_ref[...], 1)
            x3 = proj_ref[rows, O_X3:O_X3 + 128]
            x3_hi = x3.astype(MXU_DTYPE).astype(F32)
            x3_packed = x3_hi + pltpu.roll(x3 - x3_hi, GLA_GATE_LORA, axis=1) + pltpu.roll(x3_hi, 2 * GLA_GATE_LORA, axis=1)
            graw = jnp.dot(x3_packed.astype(MXU_DTYPE), gk2_ref[...], preferred_element_type=F32) + gkb_ref[...]
            yield
            logw = -_softplus(-(lora[:, :RWKV_WIDTH] + w0)) - 0.5
            ld = -jnp.exp(logw)
            iclr = _sigmoid(lora[:, RWKV_WIDTH:] + a0)
            kk = kk / jnp.maximum(jnp.sqrt(kk_ss), 1e-12)
            kmod = k * (1.0 + (iclr - 1.0) * k_a)
            av = -kk
            bv = kk * iclr
            c = _sel_mm2(lt2_ref[...], ld)
            g = -_softplus(-graw) * (1.0 / GLA_GATE_NORMALIZER)
            key_offset = jnp.where(jnp.logical_and(i == 0, j == 0), SUB, 0)
            k_dup, v_dup, valid, distf = _attention_setup(k_cur, v_cur, k_prev, v_prev, key_offset)
            yield
            c_last = jnp.where(_iota(c.shape, 0) < CH, c[CH - 1:CH, :], c[2 * CH - 1:2 * CH, :])
            e_neg = jnp.exp(-c)
            e_rem = jnp.exp(c_last - c)
            return dict(
                last_row=last_row, k_cur=k_cur, v_cur=v_cur,
                k_dup=k_dup, v_dup=v_dup, valid=valid, distf=distf,
                r=r, v=v, kmod=kmod,
                at=av * jnp.exp(c - ld), bt=bv * e_neg, kt=kmod * e_neg, rt=r * jnp.exp(c),
                bp=bv * e_rem, kp=kmod * e_rem, gam=jnp.exp(c_last),
                g=g,
                qs=proj_ref[rows, O_GQ:O_GQ + GLA_KEY_WIDTH] * (GLA_KEY_DIM ** -0.5),
                gk=proj_ref[rows, O_GK:O_GK + GLA_KEY_WIDTH],
                gv=proj_ref[rows, O_GV:O_GV + GLA_VAL_WIDTH])
        return gen

    def attn_task(j, hk):
        def gen(res):
            p = res[("prep", j)]
            q_all = proj_ref[pl.ds(j * SUB, SUB), O_AQ:O_AQ + ATTN_WIDTH]
            return (yield from _attention_group(hk, q_all, p["k_dup"], p["v_dup"], p["valid"], p["distf"],
                                                sinks_ref))
        return gen

    def chunk_rows(c):
        cc = c % per_sub
        return slice(cc * CH, (cc + 1) * CH)

    def rwkv_task(c):
        def gen(res):
            p, sl = res[("prep", c // per_sub)], chunk_rows(c)
            return (yield from _rwkv_chunk_intra(p["at"][sl], p["bt"][sl], p["kt"][sl], p["rt"][sl], p["v"][sl]))
        return gen

    def gla_task(c):
        def gen(res):
            p, sl = res[("prep", c // per_sub)], chunk_rows(c)
            return (yield from _gla_chunk_intra(p["qs"][sl], p["gk"][sl], p["gv"][sl], p["g"][sl],
                                                mg_ref[...], lv_ref[...]))
        return gen

    def state_task(c):
        def gen(res):
            if c == 0:
                st, sg = srt_ref[...], sgt_ref[...]
            else:
                st, sg = res[("state", c - 1)][2:]
            p, sl = res[("prep", c // per_sub)], chunk_rows(c)
            o_c, sg_new = _gla_chunk_state(*res[("gla", c)], sg)
            a_rb, ark_v, at2, uv = res[("rwkv", c)]
            y_c, st_new = yield from _rwkv_chunk_state(a_rb, ark_v, at2, uv, p["rt"][sl], p["v"][sl], p["bp"][sl],
                                                       p["kp"][sl], p["gam"][sl.start:sl.start + 1], st)
            if c == n_chunk - 1:
                srt_ref[...] = st_new
                sgt_ref[...] = sg_new
            return y_c, o_c, st_new, sg_new
        return gen

    def finish_task(j):
        def gen(res):
            rows = pl.ds(j * SUB, SUB)
            p = res[("prep", j)]
            parts = [res[("state", j * per_sub + cc)] for cc in range(per_sub)]
            y = jnp.concatenate([q[0] for q in parts], axis=0)
            o = jnp.concatenate([q[1] for q in parts], axis=0)
            attn = jnp.concatenate([res[("attn", j, hk)] for hk in range(ATTN_KV_HEADS)], axis=1)
            mix_ref[rows, 0:ATTN_WIDTH] = (attn * _silu(proj_ref[rows, O_AG:O_AG + ATTN_WIDTH])).astype(mix_ref.dtype)
            ones_bd = ones_ref[...]
            mean = _mm_sel(y, ones_bd, 1) * (1.0 / HEAD_DIM)
            bonus = _mm_sel(p["r"] * p["kmod"] * r_k, ones_bd, 1) * p["v"]
            oms = _mm_sel(o * o, ones_bd, 1) * (1.0 / HEAD_DIM)
            yield
            yc = y - mean
            var = _mm_sel(yc * yc, ones_bd, 1) * (1.0 / HEAD_DIM)
            gla = (o * lax.rsqrt(oms + 1e-5)) * gnw * _silu(proj_ref[rows, O_GG:O_GG + GLA_VAL_WIDTH])
            mix_ref[rows, ATTN_WIDTH + RWKV_WIDTH:D_MODEL] = gla.astype(mix_ref.dtype)
            yield
            yn = (yc * lax.rsqrt(var + RWKV_LN_EPS)) * ln_w + ln_b
            rwkv = (yn + bonus) * _silu(proj_ref[rows, O_RG:O_RG + RWKV_WIDTH])
            mix_ref[rows, ATTN_WIDTH:ATTN_WIDTH + RWKV_WIDTH] = rwkv.astype(mix_ref.dtype)
        return gen

    def out_task(half):
        def gen(res):
            rows = pl.ds(half * (TM // 2), TM // 2)
            y = jnp.dot(mix_ref[rows, :], wout_ref[...], preferred_element_type=F32)
            yield
            ms2 = jnp.mean(y * y, axis=-1, keepdims=True)
            out_ref[rows, :] = x_ref[rows, :] + gate * ((y * lax.rsqrt(ms2 + NORM_EPS)) * gpost_ref[...])
        return gen

    tasks = [("proj", proj_task, [], 0)]
    for j in range(n_sub):
        tasks.append((("prep", j), prep_task(j), [("prep", j - 1)] if j else [], 4 * j))
        for cc in range(per_sub):
            c = j * per_sub + cc
            tasks.append((("rwkv", c), rwkv_task(c), [("prep", j)], 3 + 2 * c))
        for hk in range(ATTN_KV_HEADS):
            tasks.append((("attn", j, hk), attn_task(j, hk), [("prep", j)], 4 * j + 3 + hk))
    for c in range(n_chunk):
        tasks.append((("gla", c), gla_task(c), [("prep", c // per_sub)], 8 + 2 * c))
    for c in range(n_chunk):
        deps = [("rwkv", c), ("gla", c)] + ([("state", c - 1)] if c else [])
        tasks.append((("state", c), state_task(c), deps, 0))
    for j in range(n_sub):
        deps = [("state", (j + 1) * per_sub - 1)] + [("attn", j, hk) for hk in range(ATTN_KV_HEADS)]
        tasks.append((("finish", j), finish_task(j), deps, 0))
    for half in range(2):
        deps = [("finish", j) for j in range(half * n_sub // 2, (half + 1) * n_sub // 2)]
        tasks.append((("out", half), out_task(half), deps, 0))
    _run_tasks(tasks)


def _ada_kernel(c_ref, w_ref, b_ref, o_ref):
    c = c_ref[...]
    c_act = jnp.broadcast_to(_silu(c), (8, D_MODEL))
    o_ref[0] = _mm_hi(c_act, w_ref[0])[0:1] + b_ref[0]


def _adaln(c, ada_w, ada_b):
    depth = ada_w.shape[0]
    tn = D_MODEL
    return pl.pallas_call(
        _ada_kernel,
        grid=(depth, 3 * D_MODEL // tn),
        in_specs=[pl.BlockSpec((1, D_MODEL), lambda l, n: (0, 0)),
                  pl.BlockSpec((1, D_MODEL, tn), lambda l, n: (l, 0, n)),
                  pl.BlockSpec((1, 1, tn), lambda l, n: (l, 0, n))],
        out_specs=pl.BlockSpec((1, 1, tn), lambda l, n: (l, 0, n)),
        out_shape=jax.ShapeDtypeStruct((depth, 1, 3 * D_MODEL), F32),
        compiler_params=pltpu.CompilerParams(dimension_semantics=("arbitrary", "arbitrary"),
                                             vmem_limit_bytes=VMEM_LIMIT_BYTES),
        name="adaln",
    )(c, ada_w, ada_b.reshape(depth, 1, 3 * D_MODEL))


def _gla_level_matrices():
    mg = np.zeros(((N_LEVELS + 1) * CH, CH), np.float32)
    for t in range(CH):
        mg[t, :t + 1] = 1.0
    for l in range(1, N_LEVELS + 1):
        s = 2 ** (l - 1)
        for t in range(CH):
            m = (t // (2 * s)) * 2 * s + s
            if t >= m:
                mg[l * CH + t, m:t + 1] = 1.0
            else:
                mg[l * CH + t, t + 1:m] = 1.0
    lvl = np.full((CH, CH), -1, np.int32)
    for t in range(CH):
        lvl[t, t] = 0
        for s_ in range(t):
            lvl[t, s_] = int(np.floor(np.log2(t ^ s_))) + 1
    return mg, np.tile(lvl, (1, GLA_HEADS))


def _constants():
    mg, lv = _gla_level_matrices()
    tri = np.tril(np.ones((CH, CH), np.float32))
    lt2 = np.kron(np.eye(SUB // CH, dtype=np.float32), tri)
    ones_bd = np.kron(np.eye(RWKV_WIDTH // HEAD_DIM, dtype=np.float32), np.ones((HEAD_DIM, HEAD_DIM), np.float32))
    twice = lambda m: np.concatenate([m, m], axis=1)
    return (jnp.asarray(twice(mg), MXU_DTYPE), jnp.asarray(lv), jnp.asarray(twice(lt2), MXU_DTYPE),
            jnp.asarray(ones_bd, MXU_DTYPE))


def _hi_lo(w):
    hi = w.astype(MXU_DTYPE)
    return hi, (w - hi.astype(F32)).astype(MXU_DTYPE)


def _layer_call(x2d, sinks, mod, gpre, gpost, wall, wout, w2a, w2b, gk2s, mu, rv, gkb, consts):
    t = x2d.shape[0]
    n_blocks = t // TM
    mg, lv, lt2, ones_bd = consts

    def full(a):
        return pl.BlockSpec(a.shape, lambda i: (0,) * a.ndim)

    row_spec = pl.BlockSpec((TM, D_MODEL), lambda i: (i, 0))
    next_spec = pl.BlockSpec((TM, D_MODEL), lambda i: (jnp.minimum(i + 1, n_blocks - 1), 0))
    operands = (mod, gpre, gpost, wall, wout, w2a, w2b, gk2s, mu, rv, gkb, mg, lv, lt2, ones_bd)
    return pl.pallas_call(
        _layer_kernel,
        grid=(n_blocks,),
        in_specs=[pl.BlockSpec(memory_space=pltpu.SMEM), row_spec, next_spec] + [full(a) for a in operands],
        out_specs=row_spec,
        out_shape=jax.ShapeDtypeStruct((t, D_MODEL), F32),
        scratch_shapes=[
            pltpu.VMEM((2, TM, AUG_WIDTH), F32),
            pltpu.VMEM((TM, D_MODEL), MXU_DTYPE),
            pltpu.VMEM((TM, D_MODEL), MXU_DTYPE),
            pltpu.VMEM((RWKV_WIDTH, RWKV_WIDTH), F32),
            pltpu.VMEM((GLA_VAL_WIDTH, GLA_KEY_WIDTH), F32),
            pltpu.VMEM((SUB, KV_WIDTH), F32),
            pltpu.VMEM((SUB, KV_WIDTH), F32),
            pltpu.VMEM((8, 3 * RWKV_WIDTH + 128), F32),
        ],
        compiler_params=pltpu.CompilerParams(dimension_semantics=("arbitrary",),
                                             vmem_limit_bytes=VMEM_LIMIT_BYTES),
        name="hybrid_layer",
    )(sinks, x2d, x2d, *operands)


def kernel(x, c, ada_w, ada_b, norm_pre, norm_post, w_in, w_out, attn_sinks, rwkv_mu_rkv, rwkv_mu_w, rwkv_mu_a, rwkv_w0, rwkv_w1, rwkv_w2, rwkv_a0, rwkv_a1, rwkv_a2, rwkv_k_k, rwkv_k_a, rwkv_r_k, rwkv_ln_w, rwkv_ln_b, gla_gk1, gla_gk2, gla_gk_b, gla_norm_w):
    batch, seq, _ = x.shape
    depth = w_in.shape[0]
    assert batch == 1 and seq % TM == 0

    mods = _adaln(c, ada_w, ada_b)

    mu_w = rwkv_mu_w[:, :, None]
    mu_a = rwkv_mu_a[:, :, None]
    x1 = jnp.concatenate([(1.0 - mu_w) * rwkv_w1, (1.0 - mu_a) * rwkv_a1], axis=-1)
    x2 = jnp.concatenate([mu_w * rwkv_w1, mu_a * rwkv_a1], axis=-1)
    x3 = jnp.pad(gla_gk1, ((0, 0), (0, 0), (0, 128 - GLA_GATE_LORA)))
    wall = jnp.concatenate([w_in, x1, x2, x3], axis=-1).astype(MXU_DTYPE)
    wout = w_out.astype(MXU_DTYPE)
    zeros = jnp.zeros((depth, LORA, RWKV_WIDTH), F32)
    w2cat = jnp.concatenate([jnp.concatenate([rwkv_w2, zeros], axis=-1),
                             jnp.concatenate([zeros, rwkv_a2], axis=-1)], axis=1)
    w2_hi, w2_lo = _hi_lo(w2cat)
    w2a = jnp.concatenate([w2_hi, w2_hi], axis=1)
    gk2_hi, gk2_lo = _hi_lo(gla_gk2)
    gk2s = jnp.pad(jnp.concatenate([gk2_hi, gk2_hi, gk2_lo], axis=1),
                   ((0, 0), (0, 128 - 3 * GLA_GATE_LORA), (0, 0)))
    rv = jnp.stack([rwkv_w0, rwkv_a0, rwkv_k_k, rwkv_k_a, rwkv_r_k.reshape(depth, RWKV_WIDTH),
                    rwkv_ln_w, rwkv_ln_b, jnp.tile(gla_norm_w, (1, GLA_HEADS))], axis=1)
    consts = _constants()

    xc = x[0]
    for l in range(depth):
        xc = _layer_call(xc, attn_sinks[l], mods[l], norm_pre[l][None], norm_post[l][None],
                         wall[l], wout[l], w2a[l], w2_lo[l], gk2s[l], rwkv_mu_rkv[l][None], rv[l],
                         gla_gk_b[l][None], consts)
    return xc[None]
```

```python
import numpy as np
import jax
import jax.numpy as jnp
from jax import lax
from jax.experimental import pallas as pl
from jax.experimental.pallas import tpu as pltpu

F32 = jnp.float32
MXU_DTYPE = jnp.bfloat16

D_MODEL = 1024
HEAD_DIM = 64
ATTN_WIDTH = 512
ATTN_Q_HEADS = 8
ATTN_KV_HEADS = 2
KV_WIDTH = 128
WINDOW = 128
RWKV_WIDTH = 256
RWKV_LN_EPS = 64e-5
LORA = 64
GLA_HEADS = 4
GLA_KEY_WIDTH = 128
GLA_KEY_DIM = 32
GLA_VAL_WIDTH = 256
GLA_GATE_LORA = 16
GLA_GATE_NORMALIZER = 16.0
NORM_EPS = 1e-6
IN_WIDTH = 3072

O_AQ, O_AK, O_AV, O_AG = 0, 512, 640, 768
O_RKV, O_RG = 1280, 2048
O_GQ, O_GK, O_GV, O_GG = 2304, 2432, 2560, 2816
O_X1, O_X2, O_X3 = 3072, 3200, 3328
AUG_WIDTH = 3456

TM = 512
SUB = 128
CH = 64
N_LEVELS = 6
PROJ_TILE = 256

VMEM_LIMIT_BYTES = 60000 * 1024


def _mm(a, b):
    return jnp.dot(a.astype(MXU_DTYPE), b.astype(MXU_DTYPE), preferred_element_type=F32)


def _mm_nt(a, b):
    return lax.dot_general(a.astype(MXU_DTYPE), b.astype(MXU_DTYPE), (((1,), (1,)), ((), ())),
                           preferred_element_type=F32)


def _mm_tn(a, b):
    return lax.dot_general(a.astype(MXU_DTYPE), b.astype(MXU_DTYPE), (((0,), (0,)), ((), ())),
                           preferred_element_type=F32)


def _split(x, n):
    parts, r = [], x
    for _ in range(n):
        p = r.astype(MXU_DTYPE)
        parts.append(p)
        r = r - p.astype(F32)
    return parts


def _sel_mm2(sel2, x):
    hi, lo = _split(x, 2)
    return jnp.dot(sel2, jnp.concatenate([hi, lo], axis=0), preferred_element_type=F32)


def _mm_sel(x, sel, n):
    acc = None
    for p in _split(x, n):
        t = jnp.dot(p, sel, preferred_element_type=F32)
        acc = t if acc is None else acc + t
    return acc


def _mm_hi(a, b):
    ah, al = _split(a, 2)
    bh, bl = _split(b, 2)
    return (jnp.dot(ah, bh, preferred_element_type=F32) + jnp.dot(ah, bl, preferred_element_type=F32)
            + jnp.dot(al, bh, preferred_element_type=F32))


def _iota(shape, dim):
    return lax.broadcasted_iota(jnp.int32, shape, dim)


def _sigmoid(z):
    return 1.0 / (1.0 + jnp.exp(-z))


def _silu(z):
    return z * _sigmoid(z)


def _softplus(z):
    return jnp.maximum(z, 0.0) + jnp.log(1.0 + jnp.exp(-jnp.abs(z)))


def _block_diag(x, row_shift, col_shift):
    t = jnp.concatenate([x, x, x, x], axis=0)
    keep = (_iota(t.shape, 0) >> row_shift) == (_iota(t.shape, 1) >> col_shift)
    return jnp.where(keep, t, 0.0)


def _bd64(x):
    return _block_diag(x, 6, 6)


def _run_tasks(tasks):
    results, running, pending, rnd = {}, {}, list(tasks), 0
    while pending or running:
        for task in list(pending):
            name, make, deps, earliest = task
            if earliest <= rnd and all(d in results for d in deps):
                running[name] = make(results)
                pending.remove(task)
        for name in list(running):
            try:
                next(running[name])
            except StopIteration as stop:
                results[name] = stop.value
                del running[name]
        rnd += 1
        assert rnd < 1000, "task graph cannot make progress"
    return results


def _attention_setup(k_cur, v_cur, k_prev, v_prev, key_offset):
    kb = jnp.concatenate([k_prev, k_cur], axis=0)
    vb = jnp.concatenate([v_prev, v_cur], axis=0)
    lo_kv = _iota(kb.shape, 1) < HEAD_DIM
    kr = pltpu.roll(kb, HEAD_DIM, axis=1)
    vr = pltpu.roll(vb, HEAD_DIM, axis=1)
    k_dup = (jnp.where(lo_kv, kb, kr), jnp.where(lo_kv, kr, kb))
    v_dup = (jnp.where(lo_kv, vb, vr), jnp.where(lo_kv, vr, vb))
    row = _iota((SUB, 2 * SUB), 0)
    col = _iota((SUB, 2 * SUB), 1)
    dist = row - col + SUB
    valid = (dist >= 0) & (dist < WINDOW) & (col >= key_offset)
    return k_dup, v_dup, valid, dist.astype(F32)


def _attention_group(hk, q, k_dup, v_dup, valid, distf, sinks_ref):
    lo_q = _iota((SUB, 128), 1) < HEAD_DIM
    group = ATTN_Q_HEADS // ATTN_KV_HEADS
    base = hk * group * HEAD_DIM
    slab_a = q[:, base:base + 128]
    slab_b = q[:, base + 128:base + 256]
    lhs = jnp.concatenate([jnp.where(lo_q, slab_a, 0.0), jnp.where(lo_q, 0.0, slab_a),
                           jnp.where(lo_q, slab_b, 0.0), jnp.where(lo_q, 0.0, slab_b)], axis=0)
    s_all = _mm_nt(lhs, k_dup[hk])
    yield
    ps, invs = [], []
    for g in range(group):
        hq = hk * group + g
        slope = 2.0 ** (-8.0 * (hq + 1) / ATTN_Q_HEADS)
        s = s_all[g * SUB:(g + 1) * SUB] * (HEAD_DIM ** -0.5) - slope * distf
        s = jnp.where(valid, s, -jnp.inf)
        sink = sinks_ref[hq]
        m = jnp.maximum(jnp.max(s, axis=-1, keepdims=True), sink)
        p = jnp.exp(s - m)
        den = jnp.sum(p, axis=-1, keepdims=True) + jnp.exp(sink - m)
        ps.append(p)
        invs.append(1.0 / den)
    o_all = _mm(jnp.concatenate(ps, axis=0), v_dup[hk])
    yield
    o = [o_all[g * SUB:(g + 1) * SUB] * invs[g] for g in range(group)]
    return jnp.concatenate([jnp.where(lo_q, o[0], o[1]), jnp.where(lo_q, o[2], o[3])], axis=1)


def _rwkv_chunk_intra(at, bt, kt, rt, v):
    row = _iota((CH, RWKV_WIDTH), 0)
    col = _iota((CH, RWKV_WIDTH), 1) & (HEAD_DIM - 1)
    strict = col < row
    incl = col <= row

    lhs = jnp.concatenate([at, rt], axis=0)
    g_b = _mm_nt(lhs, _bd64(bt))
    g_k = _mm_nt(lhs, _bd64(kt))
    yield
    a_ab = jnp.where(strict, g_b[:CH], 0.0)
    a_rb = jnp.where(incl, g_b[CH:], 0.0)
    a_ak = jnp.where(strict, g_k[:CH], 0.0)
    a_rk = jnp.where(incl, g_k[CH:], 0.0)

    p = _mm(a_ab, _bd64(a_ab))
    av2 = _mm(jnp.concatenate([a_ak, a_rk], axis=0), _bd64(v))
    akv, ark_v = av2[:CH], av2[CH:]
    w = jnp.where(col == row, 1.0, 0.0) + a_ab
    yield
    for _ in range(N_LEVELS - 2):
        r = _mm(jnp.concatenate([p, w], axis=0), _bd64(p))
        yield
        w = w + r[CH:]
        p = r[:CH]
    t_inv = w + _mm(w, _bd64(p))
    yield
    at2 = _mm(t_inv, _bd64(at))
    uv = _mm(t_inv, _bd64(akv))
    return a_rb, ark_v, at2, uv


def _rwkv_chunk_state(a_rb, ark_v, at2, uv, rt, v, bp, kp, gam, st):
    x = _mm_nt(jnp.concatenate([at2, rt], axis=0), st)
    yield
    u = uv + x[:CH]
    y = x[CH:] + ark_v + _mm(a_rb, _bd64(u))
    upd = _mm_tn(jnp.concatenate([u, v], axis=0), jnp.concatenate([bp, kp], axis=0))
    keep = (_iota(upd.shape, 0) >> 6) == (_iota(upd.shape, 1) >> 6)
    st_new = gam * st + jnp.where(keep, upd, 0.0)
    return y, st_new


def _gla_chunk_intra(qs, k, v, g, mg, lv):
    e_all = _sel_mm2(mg, g)
    yield
    b = e_all[:CH]
    keep_k = (_iota((4 * CH, GLA_KEY_WIDTH), 0) >> 6) == (_iota((4 * CH, GLA_KEY_WIDTH), 1) >> 5)

    def level_scores(ql, kl):
        kt = jnp.concatenate([kl, kl, kl, kl], axis=0)
        return _mm_nt(ql, jnp.where(keep_k, kt, 0.0))

    scores = [level_scores(qs, k)]
    for l in range(1, N_LEVELS + 1):
        f = jnp.exp(e_all[l * CH:(l + 1) * CH])
        scores.append(level_scores(qs * f, k * f))
    b_last = b[CH - 1:CH]
    upd = _mm_tn(v, k * jnp.exp(b_last - b))
    yield
    a = jnp.where(lv == 0, scores[0], 0.0)
    for l in range(1, N_LEVELS + 1):
        a = jnp.where(lv == l, scores[l], a)
    o_intra = _mm(a, _bd64(v))
    keep_s = (_iota(upd.shape, 0) >> 6) == (_iota(upd.shape, 1) >> 5)
    return o_intra, qs * jnp.exp(b), jnp.exp(b_last), jnp.where(keep_s, upd, 0.0)


def _gla_chunk_state(o_intra, q_dec, decay, upd, st):
    return o_intra + _mm_nt(q_dec, st), decay * st + upd


def _layer_kernel(sinks_ref, x_ref, xn_ref, mod_ref, gpre_ref, gpost_ref, win_ref, waux_ref, wout_ref, w2_ref,
                  gk2_ref, mu_ref, rv_ref, gkb_ref, mg_ref, lv_ref, lt2_ref, ones_ref,
                  out_ref,
                  pbuf_ref, hn_ref, mix_ref, srt_ref, sgt_ref, kprev_ref, vprev_ref, prev_ref):
    i = pl.program_id(0)
    shift = mod_ref[:, 0:D_MODEL]
    scale = mod_ref[:, D_MODEL:2 * D_MODEL]
    gate = mod_ref[:, 2 * D_MODEL:3 * D_MODEL]
    g_mod = gpre_ref[...] * (1.0 + scale)

    def normed(x):
        ms = jnp.mean(x * x, axis=-1, keepdims=True)
        return ((x * lax.rsqrt(ms + NORM_EPS)) * g_mod + shift).astype(MXU_DTYPE)

    @pl.when(i == 0)
    def _first():
        srt_ref[...] = jnp.zeros(srt_ref.shape, F32)
        sgt_ref[...] = jnp.zeros(sgt_ref.shape, F32)
        kprev_ref[...] = jnp.zeros(kprev_ref.shape, F32)
        vprev_ref[...] = jnp.zeros(vprev_ref.shape, F32)
        prev_ref[...] = jnp.zeros(prev_ref.shape, F32)
        h0 = normed(x_ref[...])
        pbuf_ref[0, :, 0:IN_WIDTH] = jnp.dot(h0, win_ref[...], preferred_element_type=F32)
        pbuf_ref[0, :, IN_WIDTH:AUG_WIDTH] = jnp.dot(h0, waux_ref[...], preferred_element_type=F32)

    slot = i % 2
    proj_ref = pbuf_ref.at[slot]
    next_ref = pbuf_ref.at[1 - slot]

    def proj_task(res):
        hn_ref[...] = normed(xn_ref[...])
        yield
        for w_ref, base in ((win_ref, 0), (waux_ref, IN_WIDTH)):
            width = w_ref.shape[1]
            for c0 in range(0, width, PROJ_TILE):
                c1 = min(c0 + PROJ_TILE, width)
                next_ref[:, base + c0:base + c1] = jnp.dot(hn_ref[...], w_ref[:, c0:c1],
                                                           preferred_element_type=F32)
                yield
                yield

    n_sub = TM // SUB
    n_chunk = TM // CH
    per_sub = SUB // CH
    w0, a0 = rv_ref[0:1, :], rv_ref[1:2, :]
    k_k, k_a, r_k = rv_ref[2:3, :], rv_ref[3:4, :], rv_ref[4:5, :]
    ln_w, ln_b, gnw = rv_ref[5:6, :], rv_ref[6:7, :], rv_ref[7:8, :]

    def prep_task(j):
        def gen(res):
            rows = pl.ds(j * SUB, SUB)
            if j == 0:
                prev_row, k_prev, v_prev = prev_ref[0:1, :], kprev_ref[...], vprev_ref[...]
            else:
                before = res[("prep", j - 1)]
                prev_row, k_prev, v_prev = before["last_row"], before["k_cur"], before["v_cur"]
            k_cur = proj_ref[rows, O_AK:O_AK + KV_WIDTH]
            v_cur = proj_ref[rows, O_AV:O_AV + KV_WIDTH]
            cat = jnp.concatenate([proj_ref[rows, O_RKV:O_RKV + 3 * RWKV_WIDTH],
                                   proj_ref[rows, O_X2:O_X2 + 128]], axis=1)
            rolled = pltpu.roll(cat, 1, axis=0)
            shifted = jnp.where(_iota(cat.shape, 0) == 0, prev_row, rolled)
            last_row = cat[SUB - 1:SUB, :]
            if j == n_sub - 1:
                prev_ref[0:1, :] = last_row
                kprev_ref[...] = k_cur
                vprev_ref[...] = v_cur
            rkv = cat[:, :3 * RWKV_WIDTH]
            rkv = rkv + (shifted[:, :3 * RWKV_WIDTH] - rkv) * mu_ref[...]
            r = rkv[:, 0:RWKV_WIDTH]
            k = rkv[:, RWKV_WIDTH:2 * RWKV_WIDTH]
            v = rkv[:, 2 * RWKV_WIDTH:3 * RWKV_WIDTH]
            lin = proj_ref[rows, O_X1:O_X1 + 128] + shifted[:, 3 * RWKV_WIDTH:]
            z = jnp.where(_iota(lin.shape, 1) < LORA, jnp.tanh(lin), lin)
            lora = _mm(z, w2_ref[...])
            kk = k * k_k
            kk_ss = _mm_sel(kk * kk, ones_ref[...], 1)
            graw = _mm(proj_ref[rows, O_X3:O_X3 + 128], gk2_ref[...]) + gkb_ref[...]
            yield
            logw = -_softplus(-(lora[:, :RWKV_WIDTH] + w0)) - 0.5
            ld = -jnp.exp(logw)
            iclr = _sigmoid(lora[:, RWKV_WIDTH:] + a0)
            kk = kk / jnp.maximum(jnp.sqrt(kk_ss), 1e-12)
            kmod = k * (1.0 + (iclr - 1.0) * k_a)
            av = -kk
            bv = kk * iclr
            c = _sel_mm2(lt2_ref[...], ld)
            g = -_softplus(-graw) * (1.0 / GLA_GATE_NORMALIZER)
            key_offset = jnp.where(jnp.logical_and(i == 0, j == 0), SUB, 0)
            k_dup, v_dup, valid, distf = _attention_setup(k_cur, v_cur, k_prev, v_prev, key_offset)
            yield
            c_last = jnp.where(_iota(c.shape, 0) < CH, c[CH - 1:CH, :], c[2 * CH - 1:2 * CH, :])
            e_neg = jnp.exp(-c)
            e_rem = jnp.exp(c_last - c)
            return dict(
                last_row=last_row, k_cur=k_cur, v_cur=v_cur,
                k_dup=k_dup, v_dup=v_dup, valid=valid, distf=distf,
                r=r, v=v, kmod=kmod,
                at=av * jnp.exp(c - ld), bt=bv * e_neg, kt=kmod * e_neg, rt=r * jnp.exp(c),
                bp=bv * e_rem, kp=kmod * e_rem, gam=jnp.exp(c_last),
                g=g,
                qs=proj_ref[rows, O_GQ:O_GQ + GLA_KEY_WIDTH] * (GLA_KEY_DIM ** -0.5),
                gk=proj_ref[rows, O_GK:O_GK + GLA_KEY_WIDTH],
                gv=proj_ref[rows, O_GV:O_GV + GLA_VAL_WIDTH])
        return gen

    def attn_task(j, hk):
        def gen(res):
            p = res[("prep", j)]
            q_all = proj_ref[pl.ds(j * SUB, SUB), O_AQ:O_AQ + ATTN_WIDTH]
            return (yield from _attention_group(hk, q_all, p["k_dup"], p["v_dup"], p["valid"], p["distf"],
                                                sinks_ref))
        return gen

    def chunk_rows(c):
        cc = c % per_sub
        return slice(cc * CH, (cc + 1) * CH)

    def rwkv_task(c):
        def gen(res):
            p, sl = res[("prep", c // per_sub)], chunk_rows(c)
            return (yield from _rwkv_chunk_intra(p["at"][sl], p["bt"][sl], p["kt"][sl], p["rt"][sl], p["v"][sl]))
        return gen

    def gla_task(c):
        def gen(res):
            p, sl = res[("prep", c // per_sub)], chunk_rows(c)
            return (yield from _gla_chunk_intra(p["qs"][sl], p["gk"][sl], p["gv"][sl], p["g"][sl],
                                                mg_ref[...], lv_ref[...]))
        return gen

    def state_task(c):
        def gen(res):
            if c == 0:
                st, sg = srt_ref[...], sgt_ref[...]
            else:
                st, sg = res[("state", c - 1)][2:]
            p, sl = res[("prep", c // per_sub)], chunk_rows(c)
            o_c, sg_new = _gla_chunk_state(*res[("gla", c)], sg)
            a_rb, ark_v, at2, uv = res[("rwkv", c)]
            y_c, st_new = yield from _rwkv_chunk_state(a_rb, ark_v, at2, uv, p["rt"][sl], p["v"][sl], p["bp"][sl],
                                                       p["kp"][sl], p["gam"][sl.start:sl.start + 1], st)
            if c == n_chunk - 1:
                srt_ref[...] = st_new
                sgt_ref[...] = sg_new
            return y_c, o_c, st_new, sg_new
        return gen

    def finish_task(j):
        def gen(res):
            rows = pl.ds(j * SUB, SUB)
            p = res[("prep", j)]
            parts = [res[("state", j * per_sub + cc)] for cc in range(per_sub)]
            y = jnp.concatenate([q[0] for q in parts], axis=0)
            o = jnp.concatenate([q[1] for q in parts], axis=0)
            attn = jnp.concatenate([res[("attn", j, hk)] for hk in range(ATTN_KV_HEADS)], axis=1)
            mix_ref[rows, 0:ATTN_WIDTH] = (attn * _silu(proj_ref[rows, O_AG:O_AG + ATTN_WIDTH])).astype(mix_ref.dtype)
            ones_bd = ones_ref[...]
            sums = _mm_sel(jnp.concatenate([y, p["r"] * p["kmod"] * r_k, o * o], axis=0), ones_bd, 1)
            mean = sums[0:SUB] * (1.0 / HEAD_DIM)
            bonus = sums[SUB:2 * SUB] * p["v"]
            oms = sums[2 * SUB:3 * SUB] * (1.0 / HEAD_DIM)
            yield
            yc = y - mean
            var = _mm_sel(yc * yc, ones_bd, 1) * (1.0 / HEAD_DIM)
            gla = (o * lax.rsqrt(oms + 1e-5)) * gnw * _silu(proj_ref[rows, O_GG:O_GG + GLA_VAL_WIDTH])
            mix_ref[rows, ATTN_WIDTH + RWKV_WIDTH:D_MODEL] = gla.astype(mix_ref.dtype)
            yield
            yn = (yc * lax.rsqrt(var + RWKV_LN_EPS)) * ln_w + ln_b
            rwkv = (yn + bonus) * _silu(proj_ref[rows, O_RG:O_RG + RWKV_WIDTH])
            mix_ref[rows, ATTN_WIDTH:ATTN_WIDTH + RWKV_WIDTH] = rwkv.astype(mix_ref.dtype)
        return gen

    def out_task(half):
        def gen(res):
            rows = pl.ds(half * (TM // 2), TM // 2)
            y = jnp.dot(mix_ref[rows, :], wout_ref[...], preferred_element_type=F32)
            yield
            ms2 = jnp.mean(y * y, axis=-1, keepdims=True)
            out_ref[rows, :] = x_ref[rows, :] + gate * ((y * lax.rsqrt(ms2 + NORM_EPS)) * gpost_ref[...])
        return gen

    tasks = [("proj", proj_task, [], 0)]
    for j in range(n_sub):
        tasks.append((("prep", j), prep_task(j), [("prep", j - 1)] if j else [], 4 * j))
        for cc in range(per_sub):
            c = j * per_sub + cc
            tasks.append((("rwkv", c), rwkv_task(c), [("prep", j)], 3 + 2 * c))
        for hk in range(ATTN_KV_HEADS):
            tasks.append((("attn", j, hk), attn_task(j, hk), [("prep", j)], 4 * j + 3 + hk))
    for c in range(n_chunk):
        tasks.append((("gla", c), gla_task(c), [("prep", c // per_sub)], 8 + 2 * c))
    for c in range(n_chunk):
        deps = [("rwkv", c), ("gla", c)] + ([("state", c - 1)] if c else [])
        tasks.append((("state", c), state_task(c), deps, 0))
    for j in range(n_sub):
        deps = [("state", (j + 1) * per_sub - 1)] + [("attn", j, hk) for hk in range(ATTN_KV_HEADS)]
        tasks.append((("finish", j), finish_task(j), deps, 0))
    for half in range(2):
        deps = [("finish", j) for j in range(half * n_sub // 2, (half + 1) * n_sub // 2)]
        tasks.append((("out", half), out_task(half), deps, 0))
    _run_tasks(tasks)


def _ada_kernel(c_ref, w_ref, b_ref, o_ref):
    c = c_ref[...]
    c_act = jnp.broadcast_to(_silu(c), (8, D_MODEL))
    o_ref[0] = _mm_hi(c_act, w_ref[0])[0:1] + b_ref[0]


def _adaln(c, ada_w, ada_b):
    depth = ada_w.shape[0]
    tn = D_MODEL
    return pl.pallas_call(
        _ada_kernel,
        grid=(depth, 3 * D_MODEL // tn),
        in_specs=[pl.BlockSpec((1, D_MODEL), lambda l, n: (0, 0)),
                  pl.BlockSpec((1, D_MODEL, tn), lambda l, n: (l, 0, n)),
                  pl.BlockSpec((1, 1, tn), lambda l, n: (l, 0, n))],
        out_specs=pl.BlockSpec((1, 1, tn), lambda l, n: (l, 0, n)),
        out_shape=jax.ShapeDtypeStruct((depth, 1, 3 * D_MODEL), F32),
        compiler_params=pltpu.CompilerParams(dimension_semantics=("arbitrary", "arbitrary"),
                                             vmem_limit_bytes=VMEM_LIMIT_BYTES),
        name="adaln",
    )(c, ada_w, ada_b.reshape(depth, 1, 3 * D_MODEL))


def _gla_level_matrices():
    mg = np.zeros(((N_LEVELS + 1) * CH, CH), np.float32)
    for t in range(CH):
        mg[t, :t + 1] = 1.0
    for l in range(1, N_LEVELS + 1):
        s = 2 ** (l - 1)
        for t in range(CH):
            m = (t // (2 * s)) * 2 * s + s
            if t >= m:
                mg[l * CH + t, m:t + 1] = 1.0
            else:
                mg[l * CH + t, t + 1:m] = 1.0
    lvl = np.full((CH, CH), -1, np.int32)
    for t in range(CH):
        lvl[t, t] = 0
        for s_ in range(t):
            lvl[t, s_] = int(np.floor(np.log2(t ^ s_))) + 1
    return mg, np.tile(lvl, (1, GLA_HEADS))


def _constants():
    mg, lv = _gla_level_matrices()
    tri = np.tril(np.ones((CH, CH), np.float32))
    lt2 = np.kron(np.eye(SUB // CH, dtype=np.float32), tri)
    ones_bd = np.kron(np.eye(RWKV_WIDTH // HEAD_DIM, dtype=np.float32), np.ones((HEAD_DIM, HEAD_DIM), np.float32))
    twice = lambda m: np.concatenate([m, m], axis=1)
    return (jnp.asarray(twice(mg), MXU_DTYPE), jnp.asarray(lv), jnp.asarray(twice(lt2), MXU_DTYPE),
            jnp.asarray(ones_bd, MXU_DTYPE))


def _hi_lo(w):
    hi = w.astype(MXU_DTYPE)
    return hi, (w - hi.astype(F32)).astype(MXU_DTYPE)


def _layer_call(layer, x2d, sinks, stacked, consts):
    t = x2d.shape[0]
    n_blocks = t // TM

    def full(a):
        return pl.BlockSpec(a.shape, lambda i: (0,) * a.ndim)

    def of_layer(a):
        return pl.BlockSpec((None,) + a.shape[1:], lambda i: (layer,) + (0,) * (a.ndim - 1))

    row_spec = pl.BlockSpec((TM, D_MODEL), lambda i: (i, 0))
    next_spec = pl.BlockSpec((TM, D_MODEL), lambda i: (jnp.minimum(i + 1, n_blocks - 1), 0))
    operands = tuple(stacked) + tuple(consts)
    return pl.pallas_call(
        _layer_kernel,
        grid=(n_blocks,),
        in_specs=([pl.BlockSpec(memory_space=pltpu.SMEM), row_spec, next_spec]
                  + [of_layer(a) for a in stacked] + [full(a) for a in consts]),
        out_specs=row_spec,
        out_shape=jax.ShapeDtypeStruct((t, D_MODEL), F32),
        scratch_shapes=[
            pltpu.VMEM((2, TM, AUG_WIDTH), F32),
            pltpu.VMEM((TM, D_MODEL), MXU_DTYPE),
            pltpu.VMEM((TM, D_MODEL), MXU_DTYPE),
            pltpu.VMEM((RWKV_WIDTH, RWKV_WIDTH), F32),
            pltpu.VMEM((GLA_VAL_WIDTH, GLA_KEY_WIDTH), F32),
            pltpu.VMEM((SUB, KV_WIDTH), F32),
            pltpu.VMEM((SUB, KV_WIDTH), F32),
            pltpu.VMEM((8, 3 * RWKV_WIDTH + 128), F32),
        ],
        compiler_params=pltpu.CompilerParams(dimension_semantics=("arbitrary",),
                                             vmem_limit_bytes=VMEM_LIMIT_BYTES),
        name="hybrid_layer",
    )(sinks, x2d, x2d, *operands)


def kernel(x, c, ada_w, ada_b, norm_pre, norm_post, w_in, w_out, attn_sinks, rwkv_mu_rkv, rwkv_mu_w, rwkv_mu_a, rwkv_w0, rwkv_w1, rwkv_w2, rwkv_a0, rwkv_a1, rwkv_a2, rwkv_k_k, rwkv_k_a, rwkv_r_k, rwkv_ln_w, rwkv_ln_b, gla_gk1, gla_gk2, gla_gk_b, gla_norm_w):
    batch, seq, _ = x.shape
    depth = w_in.shape[0]
    assert batch == 1 and seq % TM == 0

    mods = _adaln(c, ada_w, ada_b)

    mu_w = rwkv_mu_w[:, :, None]
    mu_a = rwkv_mu_a[:, :, None]
    x1 = jnp.concatenate([(1.0 - mu_w) * rwkv_w1, (1.0 - mu_a) * rwkv_a1], axis=-1)
    x2 = jnp.concatenate([mu_w * rwkv_w1, mu_a * rwkv_a1], axis=-1)
    x3 = jnp.pad(gla_gk1, ((0, 0), (0, 0), (0, 128 - GLA_GATE_LORA)))
    w_main = w_in.astype(MXU_DTYPE)
    w_aux = jnp.concatenate([x1, x2, x3], axis=-1).astype(MXU_DTYPE)
    wout = w_out.astype(MXU_DTYPE)
    zeros = jnp.zeros((depth, LORA, RWKV_WIDTH), F32)
    w2cat = jnp.concatenate([jnp.concatenate([rwkv_w2, zeros], axis=-1),
                             jnp.concatenate([zeros, rwkv_a2], axis=-1)], axis=1)
    w2cat = w2cat.astype(MXU_DTYPE)
    gk2p = jnp.pad(gla_gk2, ((0, 0), (0, 128 - GLA_GATE_LORA), (0, 0))).astype(MXU_DTYPE)
    rv = jnp.stack([rwkv_w0, rwkv_a0, rwkv_k_k, rwkv_k_a, rwkv_r_k.reshape(depth, RWKV_WIDTH),
                    rwkv_ln_w, rwkv_ln_b, jnp.tile(gla_norm_w, (1, GLA_HEADS))], axis=1)
    consts = _constants()
    stacked = (mods, norm_pre[:, None], norm_post[:, None], w_main, w_aux, wout, w2cat, gk2p,
               rwkv_mu_rkv[:, None], rv, gla_gk_b[:, None])

    xc = x[0]
    for l in range(depth):
        xc = _layer_call(l, xc, attn_sinks[l], stacked, consts)
    return xc[None]
```

```python
import numpy as np
import jax
import jax.numpy as jnp
from jax import lax
from jax.experimental import pallas as pl
from jax.experimental.pallas import tpu as pltpu

F32 = jnp.float32
MXU_DTYPE = jnp.bfloat16

D_MODEL = 1024
HEAD_DIM = 64
ATTN_WIDTH = 512
ATTN_Q_HEADS = 8
ATTN_KV_HEADS = 2
KV_WIDTH = 128
WINDOW = 128
RWKV_WIDTH = 256
RWKV_LN_EPS = 64e-5
LORA = 64
GLA_HEADS = 4
GLA_KEY_WIDTH = 128
GLA_KEY_DIM = 32
GLA_VAL_WIDTH = 256
GLA_GATE_LORA = 16
GLA_GATE_NORMALIZER = 16.0
NORM_EPS = 1e-6
IN_WIDTH = 3072

O_AQ, O_AK, O_AV, O_AG = 0, 512, 640, 768
O_RKV, O_RG = 1280, 2048
O_GQ, O_GK, O_GV, O_GG = 2304, 2432, 2560, 2816
O_X1, O_X2, O_X3 = 3072, 3200, 3328
AUG_WIDTH = 3456

TM = 512
SUB = 128
CH = 64
N_LEVELS = 6
PROJ_TILE = 256
GLA_SAFE_LOG = 60.0

VMEM_LIMIT_BYTES = 60000 * 1024


def _mm(a, b):
    return jnp.dot(a.astype(MXU_DTYPE), b.astype(MXU_DTYPE), preferred_element_type=F32)


def _mm_nt(a, b):
    return lax.dot_general(a.astype(MXU_DTYPE), b.astype(MXU_DTYPE), (((1,), (1,)), ((), ())),
                           preferred_element_type=F32)


def _mm_tn(a, b):
    return lax.dot_general(a.astype(MXU_DTYPE), b.astype(MXU_DTYPE), (((0,), (0,)), ((), ())),
                           preferred_element_type=F32)


def _split(x, n):
    parts, r = [], x
    for _ in range(n):
        p = r.astype(MXU_DTYPE)
        parts.append(p)
        r = r - p.astype(F32)
    return parts


def _sel_mm2(sel2, x):
    hi, lo = _split(x, 2)
    return jnp.dot(sel2, jnp.concatenate([hi, lo], axis=0), preferred_element_type=F32)


def _mm_sel(x, sel, n):
    acc = None
    for p in _split(x, n):
        t = jnp.dot(p, sel, preferred_element_type=F32)
        acc = t if acc is None else acc + t
    return acc


def _mm_hi(a, b):
    ah, al = _split(a, 2)
    bh, bl = _split(b, 2)
    return (jnp.dot(ah, bh, preferred_element_type=F32) + jnp.dot(ah, bl, preferred_element_type=F32)
            + jnp.dot(al, bh, preferred_element_type=F32))


def _iota(shape, dim):
    return lax.broadcasted_iota(jnp.int32, shape, dim)


def _sigmoid(z):
    return 1.0 / (1.0 + jnp.exp(-z))


def _silu(z):
    return z * _sigmoid(z)


def _softplus(z):
    return jnp.maximum(z, 0.0) + jnp.log(1.0 + jnp.exp(-jnp.abs(z)))


def _block_diag(x, row_shift, col_shift):
    t = jnp.concatenate([x, x, x, x], axis=0)
    keep = (_iota(t.shape, 0) >> row_shift) == (_iota(t.shape, 1) >> col_shift)
    return jnp.where(keep, t, 0.0)


def _bd64(x):
    return _block_diag(x, 6, 6)


def _run_tasks(tasks):
    results, running, pending, rnd = {}, {}, list(tasks), 0
    while pending or running:
        for task in list(pending):
            name, make, deps, earliest = task
            if earliest <= rnd and all(d in results for d in deps):
                running[name] = make(results)
                pending.remove(task)
        for name in list(running):
            try:
                next(running[name])
            except StopIteration as stop:
                results[name] = stop.value
                del running[name]
        rnd += 1
        assert rnd < 1000, "task graph cannot make progress"
    return results


def _attention_setup(k_cur, v_cur, k_prev, v_prev, key_offset):
    kb = jnp.concatenate([k_prev, k_cur], axis=0)
    vb = jnp.concatenate([v_prev, v_cur], axis=0)
    lo_kv = _iota(kb.shape, 1) < HEAD_DIM
    kr = pltpu.roll(kb, HEAD_DIM, axis=1)
    vr = pltpu.roll(vb, HEAD_DIM, axis=1)
    k_dup = (jnp.where(lo_kv, kb, kr), jnp.where(lo_kv, kr, kb))
    v_dup = (jnp.where(lo_kv, vb, vr), jnp.where(lo_kv, vr, vb))
    row = _iota((SUB, 2 * SUB), 0)
    col = _iota((SUB, 2 * SUB), 1)
    dist = row - col + SUB
    valid = (dist >= 0) & (dist < WINDOW) & (col >= key_offset)
    return k_dup, v_dup, valid, dist.astype(F32)


def _attention_group(hk, q, k_dup, v_dup, valid, distf, sinks_ref):
    lo_q = _iota((SUB, 128), 1) < HEAD_DIM
    group = ATTN_Q_HEADS // ATTN_KV_HEADS
    base = hk * group * HEAD_DIM
    slab_a = q[:, base:base + 128]
    slab_b = q[:, base + 128:base + 256]
    lhs = jnp.concatenate([jnp.where(lo_q, slab_a, 0.0), jnp.where(lo_q, 0.0, slab_a),
                           jnp.where(lo_q, slab_b, 0.0), jnp.where(lo_q, 0.0, slab_b)], axis=0)
    s_all = _mm_nt(lhs, k_dup[hk])
    yield
    ps, invs = [], []
    for g in range(group):
        hq = hk * group + g
        slope = 2.0 ** (-8.0 * (hq + 1) / ATTN_Q_HEADS)
        s = s_all[g * SUB:(g + 1) * SUB] * (HEAD_DIM ** -0.5) - slope * distf
        s = jnp.where(valid, s, -jnp.inf)
        sink = sinks_ref[hq]
        m = jnp.maximum(jnp.max(s, axis=-1, keepdims=True), sink)
        p = jnp.exp(s - m)
        den = jnp.sum(p, axis=-1, keepdims=True) + jnp.exp(sink - m)
        ps.append(p)
        invs.append(1.0 / den)
    o_all = _mm(jnp.concatenate(ps, axis=0), v_dup[hk])
    yield
    o = [o_all[g * SUB:(g + 1) * SUB] * invs[g] for g in range(group)]
    return jnp.concatenate([jnp.where(lo_q, o[0], o[1]), jnp.where(lo_q, o[2], o[3])], axis=1)


def _rwkv_chunk_intra(at, bt, kt, rt, v):
    row = _iota((CH, RWKV_WIDTH), 0)
    col = _iota((CH, RWKV_WIDTH), 1) & (HEAD_DIM - 1)
    strict = col < row
    incl = col <= row

    lhs = jnp.concatenate([at, rt], axis=0)
    g_b = _mm_nt(lhs, _bd64(bt))
    g_k = _mm_nt(lhs, _bd64(kt))
    yield
    a_ab = jnp.where(strict, g_b[:CH], 0.0)
    a_rb = jnp.where(incl, g_b[CH:], 0.0)
    a_ak = jnp.where(strict, g_k[:CH], 0.0)
    a_rk = jnp.where(incl, g_k[CH:], 0.0)

    p = _mm(a_ab, _bd64(a_ab))
    av2 = _mm(jnp.concatenate([a_ak, a_rk], axis=0), _bd64(v))
    akv, ark_v = av2[:CH], av2[CH:]
    w = jnp.where(col == row, 1.0, 0.0) + a_ab
    yield
    for _ in range(N_LEVELS - 2):
        r = _mm(jnp.concatenate([p, w], axis=0), _bd64(p))
        yield
        w = w + r[CH:]
        p = r[:CH]
    t_inv = w + _mm(w, _bd64(p))
    yield
    at2 = _mm(t_inv, _bd64(at))
    uv = _mm(t_inv, _bd64(akv))
    return a_rb, ark_v, at2, uv


def _rwkv_chunk_state(a_rb, ark_v, at2, uv, rt, v, bp, kp, gam, st):
    x = _mm_nt(jnp.concatenate([at2, rt], axis=0), st)
    yield
    u = uv + x[:CH]
    y = x[CH:] + ark_v + _mm(a_rb, _bd64(u))
    upd = _mm_tn(jnp.concatenate([u, v], axis=0), jnp.concatenate([bp, kp], axis=0))
    keep = (_iota(upd.shape, 0) >> 6) == (_iota(upd.shape, 1) >> 6)
    st_new = gam * st + jnp.where(keep, upd, 0.0)
    return y, st_new


def _gla_chunk_intra(qs, k, v, g, mg_ref, lv, bounded):
    keep_k = (_iota((4 * CH, GLA_KEY_WIDTH), 0) >> 6) == (_iota((4 * CH, GLA_KEY_WIDTH), 1) >> 5)

    def level_scores(ql, kl):
        kt = jnp.concatenate([kl, kl, kl, kl], axis=0)
        return _mm_nt(ql, jnp.where(keep_k, kt, 0.0))

    if bounded:
        b = _sel_mm2(mg_ref[0:CH, :], g)
        yield
        b_last = b[CH - 1:CH]
        scores = level_scores(qs * jnp.exp(b), k * jnp.exp(-b))
        upd = _mm_tn(v, k * jnp.exp(b_last - b))
        yield
        a = jnp.where(lv >= 0, scores, 0.0)
    else:
        e_all = _sel_mm2(mg_ref[...], g)
        yield
        b = e_all[:CH]
        b_last = b[CH - 1:CH]
        scores = [level_scores(qs, k)]
        for l in range(1, N_LEVELS + 1):
            f = jnp.exp(e_all[l * CH:(l + 1) * CH])
            scores.append(level_scores(qs * f, k * f))
        upd = _mm_tn(v, k * jnp.exp(b_last - b))
        yield
        a = jnp.where(lv == 0, scores[0], 0.0)
        for l in range(1, N_LEVELS + 1):
            a = jnp.where(lv == l, scores[l], a)
    o_intra = _mm(a, _bd64(v))
    keep_s = (_iota(upd.shape, 0) >> 6) == (_iota(upd.shape, 1) >> 5)
    return o_intra, qs * jnp.exp(b), jnp.exp(b_last), jnp.where(keep_s, upd, 0.0)


def _gla_chunk_state(o_intra, q_dec, decay, upd, st):
    return o_intra + _mm_nt(q_dec, st), decay * st + upd


def _layer_kernel(sinks_ref, x_ref, xn_ref, mod_ref, gpre_ref, gpost_ref, win_ref, waux_ref, wout_ref, w2_ref,
                  gk2_ref, mu_ref, rv_ref, gkb_ref, mg_ref, lv_ref, lt2_ref, ones_ref,
                  out_ref,
                  pbuf_ref, gbuf_ref, floor_ref, hn_ref, mix_ref, srt_ref, sgt_ref, kprev_ref, vprev_ref, prev_ref):
    i = pl.program_id(0)
    n_sub = TM // SUB
    n_chunk = TM // CH
    per_sub = SUB // CH
    shift = mod_ref[:, 0:D_MODEL]
    scale = mod_ref[:, D_MODEL:2 * D_MODEL]
    gate = mod_ref[:, 2 * D_MODEL:3 * D_MODEL]
    g_mod = gpre_ref[...] * (1.0 + scale)

    def normed(x):
        ms = jnp.mean(x * x, axis=-1, keepdims=True)
        return ((x * lax.rsqrt(ms + NORM_EPS)) * g_mod + shift).astype(MXU_DTYPE)

    def log_gates(p_ref):
        graw = _mm(p_ref[:, O_X3:O_X3 + 128], gk2_ref[...]) + gkb_ref[...]
        g = -_softplus(-graw) * (1.0 / GLA_GATE_NORMALIZER)
        lowest = jnp.sum(g[0:CH], axis=0, keepdims=True)
        for c in range(1, n_chunk):
            lowest = jnp.minimum(lowest, jnp.sum(g[c * CH:(c + 1) * CH], axis=0, keepdims=True))
        return g, jnp.min(lowest)

    @pl.when(i == 0)
    def _first():
        srt_ref[...] = jnp.zeros(srt_ref.shape, F32)
        sgt_ref[...] = jnp.zeros(sgt_ref.shape, F32)
        kprev_ref[...] = jnp.zeros(kprev_ref.shape, F32)
        vprev_ref[...] = jnp.zeros(vprev_ref.shape, F32)
        prev_ref[...] = jnp.zeros(prev_ref.shape, F32)
        h0 = normed(x_ref[...])
        pbuf_ref[0, :, 0:IN_WIDTH] = jnp.dot(h0, win_ref[...], preferred_element_type=F32)
        pbuf_ref[0, :, IN_WIDTH:AUG_WIDTH] = jnp.dot(h0, waux_ref[...], preferred_element_type=F32)
        g0, lowest0 = log_gates(pbuf_ref.at[0])
        gbuf_ref[0] = g0
        floor_ref[0] = lowest0

    slot = i % 2
    proj_ref = pbuf_ref.at[slot]
    next_ref = pbuf_ref.at[1 - slot]

    def proj_task(res):
        hn_ref[...] = normed(xn_ref[...])
        yield
        for w_ref, base in ((waux_ref, IN_WIDTH), (win_ref, 0)):
            width = w_ref.shape[1]
            for c0 in range(0, width, PROJ_TILE):
                c1 = min(c0 + PROJ_TILE, width)
                next_ref[:, base + c0:base + c1] = jnp.dot(hn_ref[...], w_ref[:, c0:c1],
                                                           preferred_element_type=F32)
                yield
                yield
            if w_ref is waux_ref:
                g_next, lowest = log_gates(next_ref)
                gbuf_ref[1 - slot] = g_next
                floor_ref[1 - slot] = lowest

    w0, a0 = rv_ref[0:1, :], rv_ref[1:2, :]
    k_k, k_a, r_k = rv_ref[2:3, :], rv_ref[3:4, :], rv_ref[4:5, :]
    ln_w, ln_b, gnw = rv_ref[5:6, :], rv_ref[6:7, :], rv_ref[7:8, :]

    def prep_task(j):
        def gen(res):
            rows = pl.ds(j * SUB, SUB)
            if j == 0:
                prev_row, k_prev, v_prev = prev_ref[0:1, :], kprev_ref[...], vprev_ref[...]
            else:
                before = res[("prep", j - 1)]
                prev_row, k_prev, v_prev = before["last_row"], before["k_cur"], before["v_cur"]
            k_cur = proj_ref[rows, O_AK:O_AK + KV_WIDTH]
            v_cur = proj_ref[rows, O_AV:O_AV + KV_WIDTH]
            cat = jnp.concatenate([proj_ref[rows, O_RKV:O_RKV + 3 * RWKV_WIDTH],
                                   proj_ref[rows, O_X2:O_X2 + 128]], axis=1)
            rolled = pltpu.roll(cat, 1, axis=0)
            shifted = jnp.where(_iota(cat.shape, 0) == 0, prev_row, rolled)
            last_row = cat[SUB - 1:SUB, :]
            if j == n_sub - 1:
                prev_ref[0:1, :] = last_row
                kprev_ref[...] = k_cur
                vprev_ref[...] = v_cur
            rkv = cat[:, :3 * RWKV_WIDTH]
            rkv = rkv + (shifted[:, :3 * RWKV_WIDTH] - rkv) * mu_ref[...]
            r = rkv[:, 0:RWKV_WIDTH]
            k = rkv[:, RWKV_WIDTH:2 * RWKV_WIDTH]
            v = rkv[:, 2 * RWKV_WIDTH:3 * RWKV_WIDTH]
            lin = proj_ref[rows, O_X1:O_X1 + 128] + shifted[:, 3 * RWKV_WIDTH:]
            z = jnp.where(_iota(lin.shape, 1) < LORA, jnp.tanh(lin), lin)
            lora = _mm(z, w2_ref[...])
            kk = k * k_k
            kk_ss = _mm_sel(kk * kk, ones_ref[...], 1)
            yield
            logw = -_softplus(-(lora[:, :RWKV_WIDTH] + w0)) - 0.5
            ld = -jnp.exp(logw)
            iclr = _sigmoid(lora[:, RWKV_WIDTH:] + a0)
            kk = kk / jnp.maximum(jnp.sqrt(kk_ss), 1e-12)
            kmod = k * (1.0 + (iclr - 1.0) * k_a)
            av = -kk
            bv = kk * iclr
            c = _sel_mm2(lt2_ref[...], ld)
            key_offset = jnp.where(jnp.logical_and(i == 0, j == 0), SUB, 0)
            k_dup, v_dup, valid, distf = _attention_setup(k_cur, v_cur, k_prev, v_prev, key_offset)
            yield
            c_last = jnp.where(_iota(c.shape, 0) < CH, c[CH - 1:CH, :], c[2 * CH - 1:2 * CH, :])
            e_neg = jnp.exp(-c)
            e_rem = jnp.exp(c_last - c)
            return dict(
                last_row=last_row, k_cur=k_cur, v_cur=v_cur,
                k_dup=k_dup, v_dup=v_dup, valid=valid, distf=distf,
                r=r, v=v, kmod=kmod,
                at=av * jnp.exp(c - ld), bt=bv * e_neg, kt=kmod * e_neg, rt=r * jnp.exp(c),
                bp=bv * e_rem, kp=kmod * e_rem, gam=jnp.exp(c_last),
                g=gbuf_ref[slot, rows, :],
                qs=proj_ref[rows, O_GQ:O_GQ + GLA_KEY_WIDTH] * (GLA_KEY_DIM ** -0.5),
                gk=proj_ref[rows, O_GK:O_GK + GLA_KEY_WIDTH],
                gv=proj_ref[rows, O_GV:O_GV + GLA_VAL_WIDTH])
        return gen

    def attn_task(j, hk):
        def gen(res):
            p = res[("prep", j)]
            q_all = proj_ref[pl.ds(j * SUB, SUB), O_AQ:O_AQ + ATTN_WIDTH]
            return (yield from _attention_group(hk, q_all, p["k_dup"], p["v_dup"], p["valid"], p["distf"],
                                                sinks_ref))
        return gen

    def chunk_rows(c):
        cc = c % per_sub
        return slice(cc * CH, (cc + 1) * CH)

    def rwkv_task(c):
        def gen(res):
            p, sl = res[("prep", c // per_sub)], chunk_rows(c)
            return (yield from _rwkv_chunk_intra(p["at"][sl], p["bt"][sl], p["kt"][sl], p["rt"][sl], p["v"][sl]))
        return gen

    def gla_task(c, bounded):
        def gen(res):
            p, sl = res[("prep", c // per_sub)], chunk_rows(c)
            return (yield from _gla_chunk_intra(p["qs"][sl], p["gk"][sl], p["gv"][sl], p["g"][sl],
                                                mg_ref, lv_ref[...], bounded))
        return gen

    def state_task(c):
        def gen(res):
            if c == 0:
                st, sg = srt_ref[...], sgt_ref[...]
            else:
                st, sg = res[("state", c - 1)][2:]
            p, sl = res[("prep", c // per_sub)], chunk_rows(c)
            o_c, sg_new = _gla_chunk_state(*res[("gla", c)], sg)
            a_rb, ark_v, at2, uv = res[("rwkv", c)]
            y_c, st_new = yield from _rwkv_chunk_state(a_rb, ark_v, at2, uv, p["rt"][sl], p["v"][sl], p["bp"][sl],
                                                       p["kp"][sl], p["gam"][sl.start:sl.start + 1], st)
            if c == n_chunk - 1:
                srt_ref[...] = st_new
                sgt_ref[...] = sg_new
            return y_c, o_c, st_new, sg_new
        return gen

    def finish_task(j):
        def gen(res):
            rows = pl.ds(j * SUB, SUB)
            p = res[("prep", j)]
            parts = [res[("state", j * per_sub + cc)] for cc in range(per_sub)]
            y = jnp.concatenate([q[0] for q in parts], axis=0)
            o = jnp.concatenate([q[1] for q in parts], axis=0)
            attn = jnp.concatenate([res[("attn", j, hk)] for hk in range(ATTN_KV_HEADS)], axis=1)
            mix_ref[rows, 0:ATTN_WIDTH] = (attn * _silu(proj_ref[rows, O_AG:O_AG + ATTN_WIDTH])).astype(mix_ref.dtype)
            ones_bd = ones_ref[...]
            sums = _mm_sel(jnp.concatenate([y, p["r"] * p["kmod"] * r_k, o * o], axis=0), ones_bd, 1)
            mean = sums[0:SUB] * (1.0 / HEAD_DIM)
            bonus = sums[SUB:2 * SUB] * p["v"]
            oms = sums[2 * SUB:3 * SUB] * (1.0 / HEAD_DIM)
            yield
            yc = y - mean
            var = _mm_sel(yc * yc, ones_bd, 1) * (1.0 / HEAD_DIM)
            gla = (o * lax.rsqrt(oms + 1e-5)) * gnw * _silu(proj_ref[rows, O_GG:O_GG + GLA_VAL_WIDTH])
            mix_ref[rows, ATTN_WIDTH + RWKV_WIDTH:D_MODEL] = gla.astype(mix_ref.dtype)
            yield
            yn = (yc * lax.rsqrt(var + RWKV_LN_EPS)) * ln_w + ln_b
            rwkv = (yn + bonus) * _silu(proj_ref[rows, O_RG:O_RG + RWKV_WIDTH])
            mix_ref[rows, ATTN_WIDTH:ATTN_WIDTH + RWKV_WIDTH] = rwkv.astype(mix_ref.dtype)
        return gen

    def out_task(half):
        def gen(res):
            rows = pl.ds(half * (TM // 2), TM // 2)
            y = jnp.dot(mix_ref[rows, :], wout_ref[...], preferred_element_type=F32)
            yield
            ms2 = jnp.mean(y * y, axis=-1, keepdims=True)
            out_ref[rows, :] = x_ref[rows, :] + gate * ((y * lax.rsqrt(ms2 + NORM_EPS)) * gpost_ref[...])
        return gen

    def run_pipeline(bounded):
        tasks = [("proj", proj_task, [], 0)]
        for j in range(n_sub):
            tasks.append((("prep", j), prep_task(j), [("prep", j - 1)] if j else [], 4 * j))
            for cc in range(per_sub):
                c = j * per_sub + cc
                tasks.append((("rwkv", c), rwkv_task(c), [("prep", j)], 3 + 2 * c))
            for hk in range(ATTN_KV_HEADS):
                tasks.append((("attn", j, hk), attn_task(j, hk), [("prep", j)], 4 * j + 3 + hk))
        for c in range(n_chunk):
            tasks.append((("gla", c), gla_task(c, bounded), [("prep", c // per_sub)], 8 + 2 * c))
        for c in range(n_chunk):
            deps = [("rwkv", c), ("gla", c)] + ([("state", c - 1)] if c else [])
            tasks.append((("state", c), state_task(c), deps, 0))
        for j in range(n_sub):
            deps = [("state", (j + 1) * per_sub - 1)] + [("attn", j, hk) for hk in range(ATTN_KV_HEADS)]
            tasks.append((("finish", j), finish_task(j), deps, 0))
        for half in range(2):
            deps = [("finish", j) for j in range(half * n_sub // 2, (half + 1) * n_sub // 2)]
            tasks.append((("out", half), out_task(half), deps, 0))
        _run_tasks(tasks)

    gates_bounded = floor_ref[slot] > -GLA_SAFE_LOG
    pl.when(gates_bounded)(lambda: run_pipeline(True))
    pl.when(jnp.logical_not(gates_bounded))(lambda: run_pipeline(False))


def _ada_kernel(c_ref, w_ref, b_ref, o_ref):
    c = c_ref[...]
    c_act = jnp.broadcast_to(_silu(c), (8, D_MODEL))
    o_ref[0] = _mm_hi(c_act, w_ref[0])[0:1] + b_ref[0]


def _adaln(c, ada_w, ada_b):
    depth = ada_w.shape[0]
    tn = D_MODEL
    return pl.pallas_call(
        _ada_kernel,
        grid=(depth, 3 * D_MODEL // tn),
        in_specs=[pl.BlockSpec((1, D_MODEL), lambda l, n: (0, 0)),
                  pl.BlockSpec((1, D_MODEL, tn), lambda l, n: (l, 0, n)),
                  pl.BlockSpec((1, 1, tn), lambda l, n: (l, 0, n))],
        out_specs=pl.BlockSpec((1, 1, tn), lambda l, n: (l, 0, n)),
        out_shape=jax.ShapeDtypeStruct((depth, 1, 3 * D_MODEL), F32),
        compiler_params=pltpu.CompilerParams(dimension_semantics=("arbitrary", "arbitrary"),
                                             vmem_limit_bytes=VMEM_LIMIT_BYTES),
        name="adaln",
    )(c, ada_w, ada_b.reshape(depth, 1, 3 * D_MODEL))


def _gla_level_matrices():
    mg = np.zeros(((N_LEVELS + 1) * CH, CH), np.float32)
    for t in range(CH):
        mg[t, :t + 1] = 1.0
    for l in range(1, N_LEVELS + 1):
        s = 2 ** (l - 1)
        for t in range(CH):
            m = (t // (2 * s)) * 2 * s + s
            if t >= m:
                mg[l * CH + t, m:t + 1] = 1.0
            else:
                mg[l * CH + t, t + 1:m] = 1.0
    lvl = np.full((CH, CH), -1, np.int32)
    for t in range(CH):
        lvl[t, t] = 0
        for s_ in range(t):
            lvl[t, s_] = int(np.floor(np.log2(t ^ s_))) + 1
    return mg, np.tile(lvl, (1, GLA_HEADS))


def _constants():
    mg, lv = _gla_level_matrices()
    tri = np.tril(np.ones((CH, CH), np.float32))
    lt2 = np.kron(np.eye(SUB // CH, dtype=np.float32), tri)
    ones_bd = np.kron(np.eye(RWKV_WIDTH // HEAD_DIM, dtype=np.float32), np.ones((HEAD_DIM, HEAD_DIM), np.float32))
    twice = lambda m: np.concatenate([m, m], axis=1)
    return (jnp.asarray(twice(mg), MXU_DTYPE), jnp.asarray(lv), jnp.asarray(twice(lt2), MXU_DTYPE),
            jnp.asarray(ones_bd, MXU_DTYPE))


def _hi_lo(w):
    hi = w.astype(MXU_DTYPE)
    return hi, (w - hi.astype(F32)).astype(MXU_DTYPE)


def _layer_call(layer, x2d, sinks, stacked, consts):
    t = x2d.shape[0]
    n_blocks = t // TM

    def full(a):
        return pl.BlockSpec(a.shape, lambda i: (0,) * a.ndim)

    def of_layer(a):
        return pl.BlockSpec((None,) + a.shape[1:], lambda i: (layer,) + (0,) * (a.ndim - 1))

    row_spec = pl.BlockSpec((TM, D_MODEL), lambda i: (i, 0))
    next_spec = pl.BlockSpec((TM, D_MODEL), lambda i: (jnp.minimum(i + 1, n_blocks - 1), 0))
    operands = tuple(stacked) + tuple(consts)
    return pl.pallas_call(
        _layer_kernel,
        grid=(n_blocks,),
        in_specs=([pl.BlockSpec(memory_space=pltpu.SMEM), row_spec, next_spec]
                  + [of_layer(a) for a in stacked] + [full(a) for a in consts]),
        out_specs=row_spec,
        out_shape=jax.ShapeDtypeStruct((t, D_MODEL), F32),
        scratch_shapes=[
            pltpu.VMEM((2, TM, AUG_WIDTH), F32),
            pltpu.VMEM((2, TM, GLA_KEY_WIDTH), F32),
            pltpu.SMEM((2,), F32),
            pltpu.VMEM((TM, D_MODEL), MXU_DTYPE),
            pltpu.VMEM((TM, D_MODEL), MXU_DTYPE),
            pltpu.VMEM((RWKV_WIDTH, RWKV_WIDTH), F32),
            pltpu.VMEM((GLA_VAL_WIDTH, GLA_KEY_WIDTH), F32),
            pltpu.VMEM((SUB, KV_WIDTH), F32),
            pltpu.VMEM((SUB, KV_WIDTH), F32),
            pltpu.VMEM((8, 3 * RWKV_WIDTH + 128), F32),
        ],
        compiler_params=pltpu.CompilerParams(dimension_semantics=("arbitrary",),
                                             vmem_limit_bytes=VMEM_LIMIT_BYTES),
        name="hybrid_layer",
    )(sinks, x2d, x2d, *operands)


def kernel(x, c, ada_w, ada_b, norm_pre, norm_post, w_in, w_out, attn_sinks, rwkv_mu_rkv, rwkv_mu_w, rwkv_mu_a, rwkv_w0, rwkv_w1, rwkv_w2, rwkv_a0, rwkv_a1, rwkv_a2, rwkv_k_k, rwkv_k_a, rwkv_r_k, rwkv_ln_w, rwkv_ln_b, gla_gk1, gla_gk2, gla_gk_b, gla_norm_w):
    batch, seq, _ = x.shape
    depth = w_in.shape[0]
    assert batch == 1 and seq % TM == 0

    mods = _adaln(c, ada_w, ada_b)

    mu_w = rwkv_mu_w[:, :, None]
    mu_a = rwkv_mu_a[:, :, None]
    x1 = jnp.concatenate([(1.0 - mu_w) * rwkv_w1, (1.0 - mu_a) * rwkv_a1], axis=-1)
    x2 = jnp.concatenate([mu_w * rwkv_w1, mu_a * rwkv_a1], axis=-1)
    x3 = jnp.pad(gla_gk1, ((0, 0), (0, 0), (0, 128 - GLA_GATE_LORA)))
    w_main = w_in.astype(MXU_DTYPE)
    w_aux = jnp.concatenate([x1, x2, x3], axis=-1).astype(MXU_DTYPE)
    wout = w_out.astype(MXU_DTYPE)
    zeros = jnp.zeros((depth, LORA, RWKV_WIDTH), F32)
    w2cat = jnp.concatenate([jnp.concatenate([rwkv_w2, zeros], axis=-1),
                             jnp.concatenate([zeros, rwkv_a2], axis=-1)], axis=1)
    w2cat = w2cat.astype(MXU_DTYPE)
    gk2p = jnp.pad(gla_gk2, ((0, 0), (0, 128 - GLA_GATE_LORA), (0, 0))).astype(MXU_DTYPE)
    rv = jnp.stack([rwkv_w0, rwkv_a0, rwkv_k_k, rwkv_k_a, rwkv_r_k.reshape(depth, RWKV_WIDTH),
                    rwkv_ln_w, rwkv_ln_b, jnp.tile(gla_norm_w, (1, GLA_HEADS))], axis=1)
    consts = _constants()
    stacked = (mods, norm_pre[:, None], norm_post[:, None], w_main, w_aux, wout, w2cat, gk2p,
               rwkv_mu_rkv[:, None], rv, gla_gk_b[:, None])

    xc = x[0]
    for l in range(depth):
        xc = _layer_call(l, xc, attn_sinks[l], stacked, consts)
    return xc[None]
```

```python
import numpy as np
import jax
import jax.numpy as jnp
from jax import lax
from jax.experimental import pallas as pl
from jax.experimental.pallas import tpu as pltpu

F32 = jnp.float32
MXU_DTYPE = jnp.bfloat16

D_MODEL = 1024
HEAD_DIM = 64
ATTN_WIDTH = 512
ATTN_Q_HEADS = 8
ATTN_KV_HEADS = 2
KV_WIDTH = 128
WINDOW = 128
RWKV_WIDTH = 256
RWKV_LN_EPS = 64e-5
LORA = 64
GLA_HEADS = 4
GLA_KEY_WIDTH = 128
GLA_KEY_DIM = 32
GLA_VAL_WIDTH = 256
GLA_GATE_LORA = 16
GLA_GATE_NORMALIZER = 16.0
NORM_EPS = 1e-6
IN_WIDTH = 3072

O_AQ, O_AK, O_AV, O_AG = 0, 512, 640, 768
O_RKV, O_RG = 1280, 2048
O_GQ, O_GK, O_GV, O_GG = 2304, 2432, 2560, 2816
O_X1, O_X2, O_X3 = 3072, 3200, 3328
AUG_WIDTH = 3456

TM = 512
SUB = 128
CH = 64
N_LEVELS = 6
PROJ_TILE = 256
GLA_SAFE_LOG = 60.0

VMEM_LIMIT_BYTES = 60000 * 1024


def _mm(a, b):
    return jnp.dot(a.astype(MXU_DTYPE), b.astype(MXU_DTYPE), preferred_element_type=F32)


def _mm_nt(a, b):
    return lax.dot_general(a.astype(MXU_DTYPE), b.astype(MXU_DTYPE), (((1,), (1,)), ((), ())),
                           preferred_element_type=F32)


def _mm_tn(a, b):
    return lax.dot_general(a.astype(MXU_DTYPE), b.astype(MXU_DTYPE), (((0,), (0,)), ((), ())),
                           preferred_element_type=F32)


def _split(x, n):
    parts, r = [], x
    for _ in range(n):
        p = r.astype(MXU_DTYPE)
        parts.append(p)
        r = r - p.astype(F32)
    return parts


def _sel_mm2(sel2, x):
    hi, lo = _split(x, 2)
    return jnp.dot(sel2, jnp.concatenate([hi, lo], axis=0), preferred_element_type=F32)


def _mm_sel(x, sel, n):
    acc = None
    for p in _split(x, n):
        t = jnp.dot(p, sel, preferred_element_type=F32)
        acc = t if acc is None else acc + t
    return acc


def _mm_hi(a, b):
    ah, al = _split(a, 2)
    bh, bl = _split(b, 2)
    return (jnp.dot(ah, bh, preferred_element_type=F32) + jnp.dot(ah, bl, preferred_element_type=F32)
            + jnp.dot(al, bh, preferred_element_type=F32))


def _iota(shape, dim):
    return lax.broadcasted_iota(jnp.int32, shape, dim)


def _sigmoid(z):
    return 1.0 / (1.0 + jnp.exp(-z))


def _silu(z):
    return z * _sigmoid(z)


def _softplus(z):
    return jnp.maximum(z, 0.0) + jnp.log(1.0 + jnp.exp(-jnp.abs(z)))


def _block_diag(x, row_shift, col_shift):
    t = jnp.concatenate([x, x, x, x], axis=0)
    keep = (_iota(t.shape, 0) >> row_shift) == (_iota(t.shape, 1) >> col_shift)
    return jnp.where(keep, t, 0.0)


def _bd64(x):
    return _block_diag(x, 6, 6)


def _run_tasks(tasks):
    results, running, pending, rnd = {}, {}, list(tasks), 0
    while pending or running:
        for task in list(pending):
            name, make, deps, earliest = task
            if earliest <= rnd and all(d in results for d in deps):
                running[name] = make(results)
                pending.remove(task)
        for name in list(running):
            try:
                next(running[name])
            except StopIteration as stop:
                results[name] = stop.value
                del running[name]
        rnd += 1
        assert rnd < 1000, "task graph cannot make progress"
    return results


def _attention_setup(k_cur, v_cur, k_prev, v_prev, key_offset):
    kb = jnp.concatenate([k_prev, k_cur], axis=0)
    vb = jnp.concatenate([v_prev, v_cur], axis=0)
    lo_kv = _iota(kb.shape, 1) < HEAD_DIM
    kr = pltpu.roll(kb, HEAD_DIM, axis=1)
    vr = pltpu.roll(vb, HEAD_DIM, axis=1)
    k_dup = (jnp.where(lo_kv, kb, kr), jnp.where(lo_kv, kr, kb))
    v_dup = (jnp.where(lo_kv, vb, vr), jnp.where(lo_kv, vr, vb))
    row = _iota((SUB, 2 * SUB), 0)
    col = _iota((SUB, 2 * SUB), 1)
    dist = row - col + SUB
    valid = (dist >= 0) & (dist < WINDOW) & (col >= key_offset)
    return k_dup, v_dup, valid, dist.astype(F32)


def _attention_group(hk, q, k_dup, v_dup, valid, distf, sinks_ref):
    lo_q = _iota((SUB, 128), 1) < HEAD_DIM
    group = ATTN_Q_HEADS // ATTN_KV_HEADS
    base = hk * group * HEAD_DIM
    slab_a = q[:, base:base + 128]
    slab_b = q[:, base + 128:base + 256]
    lhs = jnp.concatenate([jnp.where(lo_q, slab_a, 0.0), jnp.where(lo_q, 0.0, slab_a),
                           jnp.where(lo_q, slab_b, 0.0), jnp.where(lo_q, 0.0, slab_b)], axis=0)
    s_all = _mm_nt(lhs, k_dup[hk])
    yield
    ps, invs = [], []
    for g in range(group):
        hq = hk * group + g
        slope = 2.0 ** (-8.0 * (hq + 1) / ATTN_Q_HEADS)
        s = s_all[g * SUB:(g + 1) * SUB] * (HEAD_DIM ** -0.5) - slope * distf
        s = jnp.where(valid, s, -jnp.inf)
        sink = sinks_ref[hq]
        m = jnp.maximum(jnp.max(s, axis=-1, keepdims=True), sink)
        p = jnp.exp(s - m)
        den = jnp.sum(p, axis=-1, keepdims=True) + jnp.exp(sink - m)
        ps.append(p)
        invs.append(1.0 / den)
    o_all = _mm(jnp.concatenate(ps, axis=0), v_dup[hk])
    yield
    o = [o_all[g * SUB:(g + 1) * SUB] * invs[g] for g in range(group)]
    return jnp.concatenate([jnp.where(lo_q, o[0], o[1]), jnp.where(lo_q, o[2], o[3])], axis=1)


def _rwkv_chunk_intra(at, bt, kt, rt, v):
    row = _iota((CH, RWKV_WIDTH), 0)
    col = _iota((CH, RWKV_WIDTH), 1) & (HEAD_DIM - 1)
    strict = col < row
    incl = col <= row

    lhs = jnp.concatenate([at, rt], axis=0)
    g_b = _mm_nt(lhs, _bd64(bt))
    g_k = _mm_nt(lhs, _bd64(kt))
    yield
    a_ab = jnp.where(strict, g_b[:CH], 0.0)
    a_rb = jnp.where(incl, g_b[CH:], 0.0)
    a_ak = jnp.where(strict, g_k[:CH], 0.0)
    a_rk = jnp.where(incl, g_k[CH:], 0.0)

    p = _mm(a_ab, _bd64(a_ab))
    av2 = _mm(jnp.concatenate([a_ak, a_rk], axis=0), _bd64(v))
    akv, ark_v = av2[:CH], av2[CH:]
    w = jnp.where(col == row, 1.0, 0.0) + a_ab
    yield
    for _ in range(N_LEVELS - 2):
        r = _mm(jnp.concatenate([p, w], axis=0), _bd64(p))
        yield
        w = w + r[CH:]
        p = r[:CH]
    t_inv = w + _mm(w, _bd64(p))
    yield
    at2 = _mm(t_inv, _bd64(at))
    uv = _mm(t_inv, _bd64(akv))
    return a_rb, ark_v, at2, uv


def _rwkv_chunk_state(a_rb, ark_v, at2, uv, rt, v, bp, kp, gam, st):
    x = _mm_nt(jnp.concatenate([at2, rt], axis=0), st)
    yield
    u = uv + x[:CH]
    y = x[CH:] + ark_v + _mm(a_rb, _bd64(u))
    upd = _mm_tn(jnp.concatenate([u, v], axis=0), jnp.concatenate([bp, kp], axis=0))
    keep = (_iota(upd.shape, 0) >> 6) == (_iota(upd.shape, 1) >> 6)
    st_new = gam * st + jnp.where(keep, upd, 0.0)
    return y, st_new


def _gla_level_scores(ql, kl):
    keep_k = (_iota((4 * CH, GLA_KEY_WIDTH), 0) >> 6) == (_iota((4 * CH, GLA_KEY_WIDTH), 1) >> 5)
    kt = jnp.concatenate([kl, kl, kl, kl], axis=0)
    return _mm_nt(ql, jnp.where(keep_k, kt, 0.0))


def _gla_state_update(v, k, b, b_last):
    upd = _mm_tn(v, k * jnp.exp(b_last - b))
    keep_s = (_iota(upd.shape, 0) >> 6) == (_iota(upd.shape, 1) >> 5)
    return jnp.where(keep_s, upd, 0.0)


def _gla_chunk_intra(qs, k, v, g, mg_ref, lv):
    b = _sel_mm2(mg_ref[0:CH, :], g)
    yield
    b_last = b[CH - 1:CH]
    scores = _gla_level_scores(qs * jnp.exp(b), k * jnp.exp(-jnp.maximum(b, -GLA_SAFE_LOG)))
    upd = _gla_state_update(v, k, b, b_last)
    yield
    o_intra = _mm(jnp.where(lv >= 0, scores, 0.0), _bd64(v))
    return o_intra, qs * jnp.exp(b), jnp.exp(b_last), upd


def _gla_chunk_state(o_intra, q_dec, decay, upd, st):
    return o_intra + _mm_nt(q_dec, st), decay * st + upd


def _gla_chunk_general(qs, k, v, g, mg_ref, lv, st):
    e_all = _sel_mm2(mg_ref[...], g)
    b = e_all[:CH]
    b_last = b[CH - 1:CH]
    a = jnp.where(lv == 0, _gla_level_scores(qs, k), 0.0)
    for l in range(1, N_LEVELS + 1):
        f = jnp.exp(e_all[l * CH:(l + 1) * CH])
        a = jnp.where(lv == l, _gla_level_scores(qs * f, k * f), a)
    o = _mm(a, _bd64(v)) + _mm_nt(qs * jnp.exp(b), st)
    return o, jnp.exp(b_last) * st + _gla_state_update(v, k, b, b_last)


def _layer_kernel(sinks_ref, x_ref, xn_ref, mod_ref, gpre_ref, gpost_ref, win_ref, waux_ref, wout_ref, w2_ref,
                  gk2_ref, mu_ref, rv_ref, gkb_ref, mg_ref, lv_ref, lt2_ref, ones_ref,
                  out_ref,
                  pbuf_ref, gbuf_ref, floor_ref, galt_ref, sgalt_ref, hn_ref, mix_ref, srt_ref, sgt_ref,
                  kprev_ref, vprev_ref, prev_ref):
    i = pl.program_id(0)
    n_sub = TM // SUB
    n_chunk = TM // CH
    per_sub = SUB // CH
    shift = mod_ref[:, 0:D_MODEL]
    scale = mod_ref[:, D_MODEL:2 * D_MODEL]
    gate = mod_ref[:, 2 * D_MODEL:3 * D_MODEL]
    g_mod = gpre_ref[...] * (1.0 + scale)

    def normed(x):
        ms = jnp.mean(x * x, axis=-1, keepdims=True)
        return ((x * lax.rsqrt(ms + NORM_EPS)) * g_mod + shift).astype(MXU_DTYPE)

    def log_gates(p_ref):
        graw = _mm(p_ref[:, O_X3:O_X3 + 128], gk2_ref[...]) + gkb_ref[...]
        g = -_softplus(-graw) * (1.0 / GLA_GATE_NORMALIZER)
        lowest = jnp.sum(g[0:CH], axis=0, keepdims=True)
        for c in range(1, n_chunk):
            lowest = jnp.minimum(lowest, jnp.sum(g[c * CH:(c + 1) * CH], axis=0, keepdims=True))
        return g, jnp.min(lowest)

    @pl.when(i == 0)
    def _first():
        srt_ref[...] = jnp.zeros(srt_ref.shape, F32)
        sgt_ref[...] = jnp.zeros(sgt_ref.shape, F32)
        kprev_ref[...] = jnp.zeros(kprev_ref.shape, F32)
        vprev_ref[...] = jnp.zeros(vprev_ref.shape, F32)
        prev_ref[...] = jnp.zeros(prev_ref.shape, F32)
        galt_ref[...] = jnp.zeros(galt_ref.shape, F32)
        sgalt_ref[...] = jnp.zeros(sgalt_ref.shape, F32)
        h0 = normed(x_ref[...])
        pbuf_ref[0, :, 0:IN_WIDTH] = jnp.dot(h0, win_ref[...], preferred_element_type=F32)
        pbuf_ref[0, :, IN_WIDTH:AUG_WIDTH] = jnp.dot(h0, waux_ref[...], preferred_element_type=F32)
        g0, lowest0 = log_gates(pbuf_ref.at[0])
        gbuf_ref[0] = g0
        floor_ref[0] = lowest0

    slot = i % 2
    proj_ref = pbuf_ref.at[slot]
    next_ref = pbuf_ref.at[1 - slot]

    gates_bounded = floor_ref[slot] > -GLA_SAFE_LOG

    @pl.when(jnp.logical_not(gates_bounded))
    def _general_gla():
        sgalt_ref[...] = sgt_ref[...]

        def chunk(c, carry):
            rows = pl.ds(pl.multiple_of(c * CH, CH), CH)
            o, st_new = _gla_chunk_general(
                proj_ref[rows, O_GQ:O_GQ + GLA_KEY_WIDTH] * (GLA_KEY_DIM ** -0.5),
                proj_ref[rows, O_GK:O_GK + GLA_KEY_WIDTH], proj_ref[rows, O_GV:O_GV + GLA_VAL_WIDTH],
                gbuf_ref[slot, rows, :], mg_ref, lv_ref[...], sgalt_ref[...])
            galt_ref[rows, :] = o
            sgalt_ref[...] = st_new
            return carry

        lax.fori_loop(0, n_chunk, chunk, 0)

    def proj_task(res):
        hn_ref[...] = normed(xn_ref[...])
        yield
        for w_ref, base in ((waux_ref, IN_WIDTH), (win_ref, 0)):
            width = w_ref.shape[1]
            for c0 in range(0, width, PROJ_TILE):
                c1 = min(c0 + PROJ_TILE, width)
                next_ref[:, base + c0:base + c1] = jnp.dot(hn_ref[...], w_ref[:, c0:c1],
                                                           preferred_element_type=F32)
                yield
                yield
            if w_ref is waux_ref:
                g_next, lowest = log_gates(next_ref)
                gbuf_ref[1 - slot] = g_next
                floor_ref[1 - slot] = lowest

    w0, a0 = rv_ref[0:1, :], rv_ref[1:2, :]
    k_k, k_a, r_k = rv_ref[2:3, :], rv_ref[3:4, :], rv_ref[4:5, :]
    ln_w, ln_b, gnw = rv_ref[5:6, :], rv_ref[6:7, :], rv_ref[7:8, :]

    def prep_task(j):
        def gen(res):
            rows = pl.ds(j * SUB, SUB)
            if j == 0:
                prev_row, k_prev, v_prev = prev_ref[0:1, :], kprev_ref[...], vprev_ref[...]
            else:
                before = res[("prep", j - 1)]
                prev_row, k_prev, v_prev = before["last_row"], before["k_cur"], before["v_cur"]
            k_cur = proj_ref[rows, O_AK:O_AK + KV_WIDTH]
            v_cur = proj_ref[rows, O_AV:O_AV + KV_WIDTH]
            cat = jnp.concatenate([proj_ref[rows, O_RKV:O_RKV + 3 * RWKV_WIDTH],
                                   proj_ref[rows, O_X2:O_X2 + 128]], axis=1)
            rolled = pltpu.roll(cat, 1, axis=0)
            shifted = jnp.where(_iota(cat.shape, 0) == 0, prev_row, rolled)
            last_row = cat[SUB - 1:SUB, :]
            if j == n_sub - 1:
                prev_ref[0:1, :] = last_row
                kprev_ref[...] = k_cur
                vprev_ref[...] = v_cur
            rkv = cat[:, :3 * RWKV_WIDTH]
            rkv = rkv + (shifted[:, :3 * RWKV_WIDTH] - rkv) * mu_ref[...]
            r = rkv[:, 0:RWKV_WIDTH]
            k = rkv[:, RWKV_WIDTH:2 * RWKV_WIDTH]
            v = rkv[:, 2 * RWKV_WIDTH:3 * RWKV_WIDTH]
            lin = proj_ref[rows, O_X1:O_X1 + 128] + shifted[:, 3 * RWKV_WIDTH:]
            z = jnp.where(_iota(lin.shape, 1) < LORA, jnp.tanh(lin), lin)
            lora = _mm(z, w2_ref[...])
            kk = k * k_k
            kk_ss = _mm_sel(kk * kk, ones_ref[...], 1)
            yield
            logw = -_softplus(-(lora[:, :RWKV_WIDTH] + w0)) - 0.5
            ld = -jnp.exp(logw)
            iclr = _sigmoid(lora[:, RWKV_WIDTH:] + a0)
            kk = kk / jnp.maximum(jnp.sqrt(kk_ss), 1e-12)
            kmod = k * (1.0 + (iclr - 1.0) * k_a)
            av = -kk
            bv = kk * iclr
            c = _sel_mm2(lt2_ref[...], ld)
            key_offset = jnp.where(jnp.logical_and(i == 0, j == 0), SUB, 0)
            k_dup, v_dup, valid, distf = _attention_setup(k_cur, v_cur, k_prev, v_prev, key_offset)
            yield
            c_last = jnp.where(_iota(c.shape, 0) < CH, c[CH - 1:CH, :], c[2 * CH - 1:2 * CH, :])
            e_neg = jnp.exp(-c)
            e_rem = jnp.exp(c_last - c)
            return dict(
                last_row=last_row, k_cur=k_cur, v_cur=v_cur,
                k_dup=k_dup, v_dup=v_dup, valid=valid, distf=distf,
                r=r, v=v, kmod=kmod,
                at=av * jnp.exp(c - ld), bt=bv * e_neg, kt=kmod * e_neg, rt=r * jnp.exp(c),
                bp=bv * e_rem, kp=kmod * e_rem, gam=jnp.exp(c_last),
                g=gbuf_ref[slot, rows, :],
                qs=proj_ref[rows, O_GQ:O_GQ + GLA_KEY_WIDTH] * (GLA_KEY_DIM ** -0.5),
                gk=proj_ref[rows, O_GK:O_GK + GLA_KEY_WIDTH],
                gv=proj_ref[rows, O_GV:O_GV + GLA_VAL_WIDTH])
        return gen

    def attn_task(j, hk):
        def gen(res):
            p = res[("prep", j)]
            q_all = proj_ref[pl.ds(j * SUB, SUB), O_AQ:O_AQ + ATTN_WIDTH]
            return (yield from _attention_group(hk, q_all, p["k_dup"], p["v_dup"], p["valid"], p["distf"],
                                                sinks_ref))
        return gen

    def chunk_rows(c):
        cc = c % per_sub
        return slice(cc * CH, (cc + 1) * CH)

    def rwkv_task(c):
        def gen(res):
            p, sl = res[("prep", c // per_sub)], chunk_rows(c)
            return (yield from _rwkv_chunk_intra(p["at"][sl], p["bt"][sl], p["kt"][sl], p["rt"][sl], p["v"][sl]))
        return gen

    def gla_task(c):
        def gen(res):
            p, sl = res[("prep", c // per_sub)], chunk_rows(c)
            return (yield from _gla_chunk_intra(p["qs"][sl], p["gk"][sl], p["gv"][sl], p["g"][sl],
                                                mg_ref, lv_ref[...]))
        return gen

    def state_task(c):
        def gen(res):
            if c == 0:
                st, sg = srt_ref[...], sgt_ref[...]
            else:
                st, sg = res[("state", c - 1)][2:]
            p, sl = res[("prep", c // per_sub)], chunk_rows(c)
            o_c, sg_new = _gla_chunk_state(*res[("gla", c)], sg)
            a_rb, ark_v, at2, uv = res[("rwkv", c)]
            y_c, st_new = yield from _rwkv_chunk_state(a_rb, ark_v, at2, uv, p["rt"][sl], p["v"][sl], p["bp"][sl],
                                                       p["kp"][sl], p["gam"][sl.start:sl.start + 1], st)
            if c == n_chunk - 1:
                srt_ref[...] = st_new
                sgt_ref[...] = jnp.where(gates_bounded, sg_new, sgalt_ref[...])
            return y_c, o_c, st_new, sg_new
        return gen

    def finish_task(j):
        def gen(res):
            rows = pl.ds(j * SUB, SUB)
            p = res[("prep", j)]
            parts = [res[("state", j * per_sub + cc)] for cc in range(per_sub)]
            y = jnp.concatenate([q[0] for q in parts], axis=0)
            o = jnp.where(gates_bounded, jnp.concatenate([q[1] for q in parts], axis=0), galt_ref[rows, :])
            attn =jnp.concatenate([res[("attn", j, hk)] for hk in range(ATTN_KV_HEADS)], axis=1)
            mix_ref[rows, 0:ATTN_WIDTH] = (attn * _silu(proj_ref[rows, O_AG:O_AG + ATTN_WIDTH])).astype(mix_ref.dtype)
            ones_bd = ones_ref[...]
            sums = _mm_sel(jnp.concatenate([y, p["r"] * p["kmod"] * r_k, o * o], axis=0), ones_bd, 1)
            mean = sums[0:SUB] * (1.0 / HEAD_DIM)
            bonus = sums[SUB:2 * SUB] * p["v"]
            oms = sums[2 * SUB:3 * SUB] * (1.0 / HEAD_DIM)
            yield
            yc = y - mean
            var = _mm_sel(yc * yc, ones_bd, 1) * (1.0 / HEAD_DIM)
            gla = (o * lax.rsqrt(oms + 1e-5)) * gnw * _silu(proj_ref[rows, O_GG:O_GG + GLA_VAL_WIDTH])
            mix_ref[rows, ATTN_WIDTH + RWKV_WIDTH:D_MODEL] = gla.astype(mix_ref.dtype)
            yield
            yn = (yc * lax.rsqrt(var + RWKV_LN_EPS)) * ln_w + ln_b
            rwkv = (yn + bonus) * _silu(proj_ref[rows, O_RG:O_RG + RWKV_WIDTH])
            mix_ref[rows, ATTN_WIDTH:ATTN_WIDTH + RWKV_WIDTH] = rwkv.astype(mix_ref.dtype)
        return gen

    def out_task(half):
        def gen(res):
            rows = pl.ds(half * (TM // 2), TM // 2)
            y = jnp.dot(mix_ref[rows, :], wout_ref[...], preferred_element_type=F32)
            yield
            ms2 = jnp.mean(y * y, axis=-1, keepdims=True)
            out_ref[rows, :] = x_ref[rows, :] + gate * ((y * lax.rsqrt(ms2 + NORM_EPS)) * gpost_ref[...])
        return gen

    def run_pipeline():
        tasks = [("proj", proj_task, [], 0)]
        for j in range(n_sub):
            tasks.append((("prep", j), prep_task(j), [("prep", j - 1)] if j else [], 4 * j))
            for cc in range(per_sub):
                c = j * per_sub + cc
                tasks.append((("rwkv", c), rwkv_task(c), [("prep", j)], 3 + 2 * c))
            for hk in range(ATTN_KV_HEADS):
                tasks.append((("attn", j, hk), attn_task(j, hk), [("prep", j)], 4 * j + 3 + hk))
        for c in range(n_chunk):
            tasks.append((("gla", c), gla_task(c), [("prep", c // per_sub)], 8 + 2 * c))
        for c in range(n_chunk):
            deps = [("rwkv", c), ("gla", c)] + ([("state", c - 1)] if c else [])
            tasks.append((("state", c), state_task(c), deps, 0))
        for j in range(n_sub):
            deps = [("state", (j + 1) * per_sub - 1)] + [("attn", j, hk) for hk in range(ATTN_KV_HEADS)]
            tasks.append((("finish", j), finish_task(j), deps, 0))
        for half in range(2):
            deps = [("finish", j) for j in range(half * n_sub // 2, (half + 1) * n_sub // 2)]
            tasks.append((("out", half), out_task(half), deps, 0))
        _run_tasks(tasks)

    run_pipeline()


def _ada_kernel(c_ref, w_ref, b_ref, o_ref):
    c = c_ref[...]
    c_act = jnp.broadcast_to(_silu(c), (8, D_MODEL))
    o_ref[0] = _mm_hi(c_act, w_ref[0])[0:1] + b_ref[0]


def _adaln(c, ada_w, ada_b):
    depth = ada_w.shape[0]
    tn = D_MODEL
    return pl.pallas_call(
        _ada_kernel,
        grid=(depth, 3 * D_MODEL // tn),
        in_specs=[pl.BlockSpec((1, D_MODEL), lambda l, n: (0, 0)),
                  pl.BlockSpec((1, D_MODEL, tn), lambda l, n: (l, 0, n)),
                  pl.BlockSpec((1, 1, tn), lambda l, n: (l, 0, n))],
        out_specs=pl.BlockSpec((1, 1, tn), lambda l, n: (l, 0, n)),
        out_shape=jax.ShapeDtypeStruct((depth, 1, 3 * D_MODEL), F32),
        compiler_params=pltpu.CompilerParams(dimension_semantics=("arbitrary", "arbitrary"),
                                             vmem_limit_bytes=VMEM_LIMIT_BYTES),
        name="adaln",
    )(c, ada_w, ada_b.reshape(depth, 1, 3 * D_MODEL))


def _gla_level_matrices():
    mg = np.zeros(((N_LEVELS + 1) * CH, CH), np.float32)
    for t in range(CH):
        mg[t, :t + 1] = 1.0
    for l in range(1, N_LEVELS + 1):
        s = 2 ** (l - 1)
        for t in range(CH):
            m = (t // (2 * s)) * 2 * s + s
            if t >= m:
                mg[l * CH + t, m:t + 1] = 1.0
            else:
                mg[l * CH + t, t + 1:m] = 1.0
    lvl = np.full((CH, CH), -1, np.int32)
    for t in range(CH):
        lvl[t, t] = 0
        for s_ in range(t):
            lvl[t, s_] = int(np.floor(np.log2(t ^ s_))) + 1
    return mg, np.tile(lvl, (1, GLA_HEADS))


def _constants():
    mg, lv = _gla_level_matrices()
    tri = np.tril(np.ones((CH, CH), np.float32))
    lt2 = np.kron(np.eye(SUB // CH, dtype=np.float32), tri)
    ones_bd = np.kron(np.eye(RWKV_WIDTH // HEAD_DIM, dtype=np.float32), np.ones((HEAD_DIM, HEAD_DIM), np.float32))
    twice = lambda m: np.concatenate([m, m], axis=1)
    return (jnp.asarray(twice(mg), MXU_DTYPE), jnp.asarray(lv), jnp.asarray(twice(lt2), MXU_DTYPE),
            jnp.asarray(ones_bd, MXU_DTYPE))


def _hi_lo(w):
    hi = w.astype(MXU_DTYPE)
    return hi, (w - hi.astype(F32)).astype(MXU_DTYPE)


def _layer_call(layer, x2d, sinks, stacked, consts):
    t = x2d.shape[0]
    n_blocks = t // TM

    def full(a):
        return pl.BlockSpec(a.shape, lambda i: (0,) * a.ndim)

    def of_layer(a):
        return pl.BlockSpec((None,) + a.shape[1:], lambda i: (layer,) + (0,) * (a.ndim - 1))

    row_spec = pl.BlockSpec((TM, D_MODEL), lambda i: (i, 0))
    next_spec = pl.BlockSpec((TM, D_MODEL), lambda i: (jnp.minimum(i + 1, n_blocks - 1), 0))
    operands = tuple(stacked) + tuple(consts)
    return pl.pallas_call(
        _layer_kernel,
        grid=(n_blocks,),
        in_specs=([pl.BlockSpec(memory_space=pltpu.SMEM), row_spec, next_spec]
                  + [of_layer(a) for a in stacked] + [full(a) for a in consts]),
        out_specs=row_spec,
        out_shape=jax.ShapeDtypeStruct((t, D_MODEL), F32),
        scratch_shapes=[
            pltpu.VMEM((2, TM, AUG_WIDTH), F32),
            pltpu.VMEM((2, TM, GLA_KEY_WIDTH), F32),
            pltpu.SMEM((2,), F32),
            pltpu.VMEM((TM, GLA_VAL_WIDTH), F32),
            pltpu.VMEM((GLA_VAL_WIDTH, GLA_KEY_WIDTH), F32),
            pltpu.VMEM((TM, D_MODEL), MXU_DTYPE),
            pltpu.VMEM((TM, D_MODEL), MXU_DTYPE),
            pltpu.VMEM((RWKV_WIDTH, RWKV_WIDTH), F32),
            pltpu.VMEM((GLA_VAL_WIDTH, GLA_KEY_WIDTH), F32),
            pltpu.VMEM((SUB, KV_WIDTH), F32),
            pltpu.VMEM((SUB, KV_WIDTH), F32),
            pltpu.VMEM((8, 3 * RWKV_WIDTH + 128), F32),
        ],
        compiler_params=pltpu.CompilerParams(dimension_semantics=("arbitrary",),
                                             vmem_limit_bytes=VMEM_LIMIT_BYTES),
        name="hybrid_layer",
    )(sinks, x2d, x2d, *operands)


def kernel(x, c, ada_w, ada_b, norm_pre, norm_post, w_in, w_out, attn_sinks, rwkv_mu_rkv, rwkv_mu_w, rwkv_mu_a, rwkv_w0, rwkv_w1, rwkv_w2, rwkv_a0, rwkv_a1, rwkv_a2, rwkv_k_k, rwkv_k_a, rwkv_r_k, rwkv_ln_w, rwkv_ln_b, gla_gk1, gla_gk2, gla_gk_b, gla_norm_w):
    batch, seq, _ = x.shape
    depth = w_in.shape[0]
    assert batch == 1 and seq % TM == 0

    mods = _adaln(c, ada_w, ada_b)

    mu_w = rwkv_mu_w[:, :, None]
    mu_a = rwkv_mu_a[:, :, None]
    x1 = jnp.concatenate([(1.0 - mu_w) * rwkv_w1, (1.0 - mu_a) * rwkv_a1], axis=-1)
    x2 = jnp.concatenate([mu_w * rwkv_w1, mu_a * rwkv_a1], axis=-1)
    x3 = jnp.pad(gla_gk1, ((0, 0), (0, 0), (0, 128 - GLA_GATE_LORA)))
    w_main = w_in.astype(MXU_DTYPE)
    w_aux = jnp.concatenate([x1, x2, x3], axis=-1).astype(MXU_DTYPE)
    wout = w_out.astype(MXU_DTYPE)
    zeros = jnp.zeros((depth, LORA, RWKV_WIDTH), F32)
    w2cat = jnp.concatenate([jnp.concatenate([rwkv_w2, zeros], axis=-1),
                             jnp.concatenate([zeros, rwkv_a2], axis=-1)], axis=1)
    w2cat = w2cat.astype(MXU_DTYPE)
    gk2p = jnp.pad(gla_gk2, ((0, 0), (0, 128 - GLA_GATE_LORA), (0, 0))).astype(MXU_DTYPE)
    rv = jnp.stack([rwkv_w0, rwkv_a0, rwkv_k_k, rwkv_k_a, rwkv_r_k.reshape(depth, RWKV_WIDTH),
                    rwkv_ln_w, rwkv_ln_b, jnp.tile(gla_norm_w, (1, GLA_HEADS))], axis=1)
    consts = _constants()
    stacked = (mods, norm_pre[:, None], norm_post[:, None], w_main, w_aux, wout, w2cat, gk2p,
               rwkv_mu_rkv[:, None], rv, gla_gk_b[:, None])

    xc = x[0]
    for l in range(depth):
        xc = _layer_call(l, xc, attn_sinks[l], stacked, consts)
    return xc[None]
```

```python
import numpy as np
import jax
import jax.numpy as jnp
from jax import lax
from jax.experimental import pallas as pl
from jax.experimental.pallas import tpu as pltpu

F32 = jnp.float32
MXU_DTYPE = jnp.bfloat16

D_MODEL = 1024
HEAD_DIM = 64
ATTN_WIDTH = 512
ATTN_Q_HEADS = 8
ATTN_KV_HEADS = 2
KV_WIDTH = 128
WINDOW = 128
RWKV_WIDTH = 256
RWKV_LN_EPS = 64e-5
LORA = 64
GLA_HEADS = 4
GLA_KEY_WIDTH = 128
GLA_KEY_DIM = 32
GLA_VAL_WIDTH = 256
GLA_GATE_LORA = 16
GLA_GATE_NORMALIZER = 16.0
NORM_EPS = 1e-6
IN_WIDTH = 3072

O_AQ, O_AK, O_AV, O_AG = 0, 512, 640, 768
O_RKV, O_RG = 1280, 2048
O_GQ, O_GK, O_GV, O_GG = 2304, 2432, 2560, 2816
O_X1, O_X2, O_X3 = 3072, 3200, 3328
AUG_WIDTH = 3456

TM = 512
SUB = 128
CH = 64
N_LEVELS = 6
PROJ_TILE = 256
GLA_SAFE_LOG = 60.0

VMEM_LIMIT_BYTES = 60000 * 1024


def _mm(a, b):
    return jnp.dot(a.astype(MXU_DTYPE), b.astype(MXU_DTYPE), preferred_element_type=F32)


def _mm_nt(a, b):
    return lax.dot_general(a.astype(MXU_DTYPE), b.astype(MXU_DTYPE), (((1,), (1,)), ((), ())),
                           preferred_element_type=F32)


def _mm_tn(a, b):
    return lax.dot_general(a.astype(MXU_DTYPE), b.astype(MXU_DTYPE), (((0,), (0,)), ((), ())),
                           preferred_element_type=F32)


def _split(x, n):
    parts, r = [], x
    for _ in range(n):
        p = r.astype(MXU_DTYPE)
        parts.append(p)
        r = r - p.astype(F32)
    return parts


def _sel_mm2(sel2, x):
    hi, lo = _split(x, 2)
    return jnp.dot(sel2, jnp.concatenate([hi, lo], axis=0), preferred_element_type=F32)


def _mm_sel(x, sel, n):
    acc = None
    for p in _split(x, n):
        t = jnp.dot(p, sel, preferred_element_type=F32)
        acc = t if acc is None else acc + t
    return acc


def _mm_hi(a, b):
    ah, al = _split(a, 2)
    bh, bl = _split(b, 2)
    return (jnp.dot(ah, bh, preferred_element_type=F32) + jnp.dot(ah, bl, preferred_element_type=F32)
            + jnp.dot(al, bh, preferred_element_type=F32))


def _iota(shape, dim):
    return lax.broadcasted_iota(jnp.int32, shape, dim)


def _sigmoid(z):
    return 1.0 / (1.0 + jnp.exp(-z))


def _silu(z):
    return z * _sigmoid(z)


def _softplus(z):
    return jnp.maximum(z, 0.0) + jnp.log(1.0 + jnp.exp(-jnp.abs(z)))


def _block_diag(x, row_shift, col_shift):
    t = jnp.concatenate([x, x, x, x], axis=0)
    keep = (_iota(t.shape, 0) >> row_shift) == (_iota(t.shape, 1) >> col_shift)
    return jnp.where(keep, t, 0.0)


def _bd64(x):
    return _block_diag(x, 6, 6)


def _run_tasks(tasks):
    results, running, pending, rnd = {}, {}, list(tasks), 0
    while pending or running:
        for task in list(pending):
            name, make, deps, earliest = task
            if earliest <= rnd and all(d in results for d in deps):
                running[name] = make(results)
                pending.remove(task)
        for name in list(running):
            try:
                next(running[name])
            except StopIteration as stop:
                results[name] = stop.value
                del running[name]
        rnd += 1
        assert rnd < 1000, "task graph cannot make progress"
    return results


def _attention_setup(k_cur, v_cur, k_prev, v_prev, key_offset):
    kb = jnp.concatenate([k_prev, k_cur], axis=0)
    vb = jnp.concatenate([v_prev, v_cur], axis=0)
    lo_kv = _iota(kb.shape, 1) < HEAD_DIM
    kr = pltpu.roll(kb, HEAD_DIM, axis=1)
    vr = pltpu.roll(vb, HEAD_DIM, axis=1)
    k_dup = (jnp.where(lo_kv, kb, kr), jnp.where(lo_kv, kr, kb))
    v_dup = (jnp.where(lo_kv, vb, vr), jnp.where(lo_kv, vr, vb))
    row = _iota((SUB, 2 * SUB), 0)
    col = _iota((SUB, 2 * SUB), 1)
    dist = row - col + SUB
    valid = (dist >= 0) & (dist < WINDOW) & (col >= key_offset)
    return k_dup, v_dup, valid, dist.astype(F32)


def _attention_group(hk, q, k_dup, v_dup, valid, distf, sinks_ref):
    lo_q = _iota((SUB, 128), 1) < HEAD_DIM
    group = ATTN_Q_HEADS // ATTN_KV_HEADS
    base = hk * group * HEAD_DIM
    slab_a = q[:, base:base + 128]
    slab_b = q[:, base + 128:base + 256]
    lhs = jnp.concatenate([jnp.where(lo_q, slab_a, 0.0), jnp.where(lo_q, 0.0, slab_a),
                           jnp.where(lo_q, slab_b, 0.0), jnp.where(lo_q, 0.0, slab_b)], axis=0)
    s_all = _mm_nt(lhs, k_dup[hk])
    yield
    ps, invs = [], []
    for g in range(group):
        hq = hk * group + g
        slope = 2.0 ** (-8.0 * (hq + 1) / ATTN_Q_HEADS)
        s = s_all[g * SUB:(g + 1) * SUB] * (HEAD_DIM ** -0.5) - slope * distf
        s = jnp.where(valid, s, -jnp.inf)
        sink = sinks_ref[hq]
        m = jnp.maximum(jnp.max(s, axis=-1, keepdims=True), sink)
        p = jnp.exp(s - m)
        den = jnp.sum(p, axis=-1, keepdims=True) + jnp.exp(sink - m)
        ps.append(p)
        invs.append(1.0 / den)
    o_all = _mm(jnp.concatenate(ps, axis=0), v_dup[hk])
    yield
    o = [o_all[g * SUB:(g + 1) * SUB] * invs[g] for g in range(group)]
    return jnp.concatenate([jnp.where(lo_q, o[0], o[1]), jnp.where(lo_q, o[2], o[3])], axis=1)


def _rwkv_chunk_intra(at, bt, kt, rt, v):
    row = _iota((CH, RWKV_WIDTH), 0)
    col = _iota((CH, RWKV_WIDTH), 1) & (HEAD_DIM - 1)
    strict = col < row
    incl = col <= row

    lhs = jnp.concatenate([at, rt], axis=0)
    g_b = _mm_nt(lhs, _bd64(bt))
    g_k = _mm_nt(lhs, _bd64(kt))
    yield
    a_ab = jnp.where(strict, g_b[:CH], 0.0)
    a_rb = jnp.where(incl, g_b[CH:], 0.0)
    a_ak = jnp.where(strict, g_k[:CH], 0.0)
    a_rk = jnp.where(incl, g_k[CH:], 0.0)

    p = _mm(a_ab, _bd64(a_ab))
    av2 = _mm(jnp.concatenate([a_ak, a_rk], axis=0), _bd64(v))
    akv, ark_v = av2[:CH], av2[CH:]
    w = jnp.where(col == row, 1.0, 0.0) + a_ab
    yield
    for _ in range(N_LEVELS - 2):
        r = _mm(jnp.concatenate([p, w], axis=0), _bd64(p))
        yield
        w = w + r[CH:]
        p = r[:CH]
    t_inv = w + _mm(w, _bd64(p))
    yield
    at2 = _mm(t_inv, _bd64(at))
    uv = _mm(t_inv, _bd64(akv))
    return a_rb, ark_v, at2, uv


def _rwkv_chunk_state(a_rb, ark_v, at2, uv, rt, v, bp, kp, gam, st):
    x = _mm_nt(jnp.concatenate([at2, rt], axis=0), st)
    yield
    u = uv + x[:CH]
    y = x[CH:] + ark_v + _mm(a_rb, _bd64(u))
    upd = _mm_tn(jnp.concatenate([u, v], axis=0), jnp.concatenate([bp, kp], axis=0))
    keep = (_iota(upd.shape, 0) >> 6) == (_iota(upd.shape, 1) >> 6)
    st_new = gam * st + jnp.where(keep, upd, 0.0)
    return y, st_new


def _gla_level_scores(ql, kl):
    keep_k = (_iota((4 * CH, GLA_KEY_WIDTH), 0) >> 6) == (_iota((4 * CH, GLA_KEY_WIDTH), 1) >> 5)
    kt = jnp.concatenate([kl, kl, kl, kl], axis=0)
    return _mm_nt(ql, jnp.where(keep_k, kt, 0.0))


def _gla_state_update(v, k, b, b_last):
    upd = _mm_tn(v, k * jnp.exp(b_last - b))
    keep_s = (_iota(upd.shape, 0) >> 6) == (_iota(upd.shape, 1) >> 5)
    return jnp.where(keep_s, upd, 0.0)


def _gla_chunk_intra(qs, k, v, g, mg_ref, lv):
    b = _sel_mm2(mg_ref[0:CH, :], g)
    yield
    b_last = b[CH - 1:CH]
    scores = _gla_level_scores(qs * jnp.exp(b), k * jnp.exp(-jnp.maximum(b, -GLA_SAFE_LOG)))
    upd = _gla_state_update(v, k, b, b_last)
    yield
    o_intra = _mm(jnp.where(lv >= 0, scores, 0.0), _bd64(v))
    return o_intra, qs * jnp.exp(b), jnp.exp(b_last), upd


def _gla_chunk_state(o_intra, q_dec, decay, upd, st):
    return o_intra + _mm_nt(q_dec, st), decay * st + upd


def _gla_chunk_general(qs, k, v, g, mg_ref, lv, st):
    e_all = _sel_mm2(mg_ref[...], g)
    b = e_all[:CH]
    b_last = b[CH - 1:CH]
    a = jnp.where(lv == 0, _gla_level_scores(qs, k), 0.0)
    for l in range(1, N_LEVELS + 1):
        f = jnp.exp(e_all[l * CH:(l + 1) * CH])
        a = jnp.where(lv == l, _gla_level_scores(qs * f, k * f), a)
    o = _mm(a, _bd64(v)) + _mm_nt(qs * jnp.exp(b), st)
    return o, jnp.exp(b_last) * st + _gla_state_update(v, k, b, b_last)


def _layer_kernel(sinks_ref, x_ref, xn_ref, mod_ref, gpre_ref, gpost_ref, win_ref, waux_ref, wout_ref, w2_ref,
                  gk2_ref, mu_ref, rv_ref, gkb_ref, mg_ref, lv_ref, lt2_ref, ones_ref,
                  out_ref,
                  pbuf_ref, gbuf_ref, floor_ref, galt_ref, sgalt_ref, hn_ref, mix_ref, srt_ref, sgt_ref,
                  kprev_ref, vprev_ref, prev_ref):
    i = pl.program_id(0)
    n_sub = TM // SUB
    n_chunk = TM // CH
    per_sub = SUB // CH
    shift = mod_ref[:, 0:D_MODEL]
    scale = mod_ref[:, D_MODEL:2 * D_MODEL]
    gate = mod_ref[:, 2 * D_MODEL:3 * D_MODEL]
    g_mod = gpre_ref[...] * (1.0 + scale)

    def normed(x):
        ms = jnp.mean(x * x, axis=-1, keepdims=True)
        return ((x * lax.rsqrt(ms + NORM_EPS)) * g_mod + shift).astype(MXU_DTYPE)

    def log_gates(p_ref):
        graw = _mm(p_ref[:, O_X3:O_X3 + 128], gk2_ref[...]) + gkb_ref[...]
        g = -_softplus(-graw) * (1.0 / GLA_GATE_NORMALIZER)
        lowest = jnp.sum(g[0:CH], axis=0, keepdims=True)
        for c in range(1, n_chunk):
            lowest = jnp.minimum(lowest, jnp.sum(g[c * CH:(c + 1) * CH], axis=0, keepdims=True))
        return g, jnp.min(lowest)

    @pl.when(i == 0)
    def _first():
        srt_ref[...] = jnp.zeros(srt_ref.shape, F32)
        sgt_ref[...] = jnp.zeros(sgt_ref.shape, F32)
        kprev_ref[...] = jnp.zeros(kprev_ref.shape, F32)
        vprev_ref[...] = jnp.zeros(vprev_ref.shape, F32)
        prev_ref[...] = jnp.zeros(prev_ref.shape, F32)
        galt_ref[...] = jnp.zeros(galt_ref.shape, F32)
        sgalt_ref[...] = jnp.zeros(sgalt_ref.shape, F32)
        hn_ref[...] = normed(x_ref[...])

        def tile(t, carry):
            cols = pl.ds(pl.multiple_of(t * PROJ_TILE, PROJ_TILE), PROJ_TILE)
            pbuf_ref[0, :, cols] = jnp.dot(hn_ref[...], win_ref[:, cols], preferred_element_type=F32)
            return carry

        lax.fori_loop(0, IN_WIDTH // PROJ_TILE, tile, 0)
        pbuf_ref[0, :, IN_WIDTH:AUG_WIDTH] = jnp.dot(hn_ref[...], waux_ref[...], preferred_element_type=F32)
        g0, lowest0 = log_gates(pbuf_ref.at[0])
        gbuf_ref[0] = g0
        floor_ref[0] = lowest0

    slot = i % 2
    proj_ref = pbuf_ref.at[slot]
    next_ref = pbuf_ref.at[1 - slot]

    gates_bounded = floor_ref[slot] > -GLA_SAFE_LOG

    @pl.when(jnp.logical_not(gates_bounded))
    def _general_gla():
        sgalt_ref[...] = sgt_ref[...]

        def chunk(c, carry):
            rows = pl.ds(pl.multiple_of(c * CH, CH), CH)
            o, st_new = _gla_chunk_general(
                proj_ref[rows, O_GQ:O_GQ + GLA_KEY_WIDTH] * (GLA_KEY_DIM ** -0.5),
                proj_ref[rows, O_GK:O_GK + GLA_KEY_WIDTH], proj_ref[rows, O_GV:O_GV + GLA_VAL_WIDTH],
                gbuf_ref[slot, rows, :], mg_ref, lv_ref[...], sgalt_ref[...])
            galt_ref[rows, :] = o
            sgalt_ref[...] = st_new
            return carry

        lax.fori_loop(0, n_chunk, chunk, 0)

    def proj_task(res):
        hn_ref[...] = normed(xn_ref[...])
        yield
        for w_ref, base in ((waux_ref, IN_WIDTH), (win_ref, 0)):
            width = w_ref.shape[1]
            for c0 in range(0, width, PROJ_TILE):
                c1 = min(c0 + PROJ_TILE, width)
                next_ref[:, base + c0:base + c1] = jnp.dot(hn_ref[...], w_ref[:, c0:c1],
                                                           preferred_element_type=F32)
                yield
                yield
            if w_ref is waux_ref:
                g_next, lowest = log_gates(next_ref)
                gbuf_ref[1 - slot] = g_next
                floor_ref[1 - slot] = lowest

    w0, a0 = rv_ref[0:1, :], rv_ref[1:2, :]
    k_k, k_a, r_k = rv_ref[2:3, :], rv_ref[3:4, :], rv_ref[4:5, :]
    ln_w, ln_b, gnw = rv_ref[5:6, :], rv_ref[6:7, :], rv_ref[7:8, :]

    def prep_task(j):
        def gen(res):
            rows = pl.ds(j * SUB, SUB)
            if j == 0:
                prev_row, k_prev, v_prev = prev_ref[0:1, :], kprev_ref[...], vprev_ref[...]
            else:
                before = res[("prep", j - 1)]
                prev_row, k_prev, v_prev = before["last_row"], before["k_cur"], before["v_cur"]
            k_cur = proj_ref[rows, O_AK:O_AK + KV_WIDTH]
            v_cur = proj_ref[rows, O_AV:O_AV + KV_WIDTH]
            cat = jnp.concatenate([proj_ref[rows, O_RKV:O_RKV + 3 * RWKV_WIDTH],
                                   proj_ref[rows, O_X2:O_X2 + 128]], axis=1)
            rolled = pltpu.roll(cat, 1, axis=0)
            shifted = jnp.where(_iota(cat.shape, 0) == 0, prev_row, rolled)
            last_row = cat[SUB - 1:SUB, :]
            if j == n_sub - 1:
                prev_ref[0:1, :] = last_row
                kprev_ref[...] = k_cur
                vprev_ref[...] = v_cur
            rkv = cat[:, :3 * RWKV_WIDTH]
            rkv = rkv + (shifted[:, :3 * RWKV_WIDTH] - rkv) * mu_ref[...]
            r = rkv[:, 0:RWKV_WIDTH]
            k = rkv[:, RWKV_WIDTH:2 * RWKV_WIDTH]
            v = rkv[:, 2 * RWKV_WIDTH:3 * RWKV_WIDTH]
            lin = proj_ref[rows, O_X1:O_X1 + 128] + shifted[:, 3 * RWKV_WIDTH:]
            z = jnp.where(_iota(lin.shape, 1) < LORA, jnp.tanh(lin), lin)
            lora = _mm(z, w2_ref[...])
            kk = k * k_k
            kk_ss = _mm_sel(kk * kk, ones_ref[...], 1)
            yield
            logw = -_softplus(-(lora[:, :RWKV_WIDTH] + w0)) - 0.5
            ld = -jnp.exp(logw)
            iclr = _sigmoid(lora[:, RWKV_WIDTH:] + a0)
            kk = kk / jnp.maximum(jnp.sqrt(kk_ss), 1e-12)
            kmod = k * (1.0 + (iclr - 1.0) * k_a)
            av = -kk
            bv = kk * iclr
            c = _sel_mm2(lt2_ref[...], ld)
            key_offset = jnp.where(jnp.logical_and(i == 0, j == 0), SUB, 0)
            k_dup, v_dup, valid, distf = _attention_setup(k_cur, v_cur, k_prev, v_prev, key_offset)
            yield
            c_last = jnp.where(_iota(c.shape, 0) < CH, c[CH - 1:CH, :], c[2 * CH - 1:2 * CH, :])
            e_neg = jnp.exp(-c)
            e_rem = jnp.exp(c_last - c)
            return dict(
                last_row=last_row, k_cur=k_cur, v_cur=v_cur,
                k_dup=k_dup, v_dup=v_dup, valid=valid, distf=distf,
                r=r, v=v, kmod=kmod,
                at=av * jnp.exp(c - ld), bt=bv * e_neg, kt=kmod * e_neg, rt=r * jnp.exp(c),
                bp=bv * e_rem, kp=kmod * e_rem, gam=jnp.exp(c_last),
                g=gbuf_ref[slot, rows, :],
                qs=proj_ref[rows, O_GQ:O_GQ + GLA_KEY_WIDTH] * (GLA_KEY_DIM ** -0.5),
                gk=proj_ref[rows, O_GK:O_GK + GLA_KEY_WIDTH],
                gv=proj_ref[rows, O_GV:O_GV + GLA_VAL_WIDTH])
        return gen

    def attn_task(j, hk):
        def gen(res):
            p = res[("prep", j)]
            q_all = proj_ref[pl.ds(j * SUB, SUB), O_AQ:O_AQ + ATTN_WIDTH]
            return (yield from _attention_group(hk, q_all, p["k_dup"], p["v_dup"], p["valid"], p["distf"],
                                                sinks_ref))
        return gen

    def chunk_rows(c):
        cc = c % per_sub
        return slice(cc * CH, (cc + 1) * CH)

    def rwkv_task(c):
        def gen(res):
            p, sl = res[("prep", c // per_sub)], chunk_rows(c)
            return (yield from _rwkv_chunk_intra(p["at"][sl], p["bt"][sl], p["kt"][sl], p["rt"][sl], p["v"][sl]))
        return gen

    def gla_task(c):
        def gen(res):
            p, sl = res[("prep", c // per_sub)], chunk_rows(c)
            return (yield from _gla_chunk_intra(p["qs"][sl], p["gk"][sl], p["gv"][sl], p["g"][sl],
                                                mg_ref, lv_ref[...]))
        return gen

    def state_task(c):
        def gen(res):
            if c == 0:
                st, sg = srt_ref[...], sgt_ref[...]
            else:
                st, sg = res[("state", c - 1)][2:]
            p, sl = res[("prep", c // per_sub)], chunk_rows(c)
            o_c, sg_new = _gla_chunk_state(*res[("gla", c)], sg)
            a_rb, ark_v, at2, uv = res[("rwkv", c)]
            y_c, st_new = yield from _rwkv_chunk_state(a_rb, ark_v, at2, uv, p["rt"][sl], p["v"][sl], p["bp"][sl],
                                                       p["kp"][sl], p["gam"][sl.start:sl.start + 1], st)
            if c == n_chunk - 1:
                srt_ref[...] = st_new
                sgt_ref[...] = jnp.where(gates_bounded, sg_new, sgalt_ref[...])
            return y_c, o_c, st_new, sg_new
        return gen

    def finish_task(j):
        def gen(res):
            rows = pl.ds(j * SUB, SUB)
            p = res[("prep", j)]
            parts = [res[("state", j * per_sub + cc)] for cc in range(per_sub)]
            y = jnp.concatenate([q[0] for q in parts], axis=0)
            o = jnp.where(gates_bounded, jnp.concatenate([q[1] for q in parts], axis=0), galt_ref[rows, :])
            attn =jnp.concatenate([res[("attn", j, hk)] for hk in range(ATTN_KV_HEADS)], axis=1)
            mix_ref[rows, 0:ATTN_WIDTH] = (attn * _silu(proj_ref[rows, O_AG:O_AG + ATTN_WIDTH])).astype(mix_ref.dtype)
            ones_bd = ones_ref[...]
            sums = _mm_sel(jnp.concatenate([y, p["r"] * p["kmod"] * r_k, o * o], axis=0), ones_bd, 1)
            mean = sums[0:SUB] * (1.0 / HEAD_DIM)
            bonus = sums[SUB:2 * SUB] * p["v"]
            oms = sums[2 * SUB:3 * SUB] * (1.0 / HEAD_DIM)
            yield
            yc = y - mean
            var = _mm_sel(yc * yc, ones_bd, 1) * (1.0 / HEAD_DIM)
            gla = (o * lax.rsqrt(oms + 1e-5)) * gnw * _silu(proj_ref[rows, O_GG:O_GG + GLA_VAL_WIDTH])
            mix_ref[rows, ATTN_WIDTH + RWKV_WIDTH:D_MODEL] = gla.astype(mix_ref.dtype)
            yield
            yn = (yc * lax.rsqrt(var + RWKV_LN_EPS)) * ln_w + ln_b
            rwkv = (yn + bonus) * _silu(proj_ref[rows, O_RG:O_RG + RWKV_WIDTH])
            mix_ref[rows, ATTN_WIDTH:ATTN_WIDTH + RWKV_WIDTH] = rwkv.astype(mix_ref.dtype)
        return gen

    def out_task(half):
        def gen(res):
            rows = pl.ds(half * (TM // 2), TM // 2)
            y = jnp.dot(mix_ref[rows, :], wout_ref[...], preferred_element_type=F32)
            yield
            ms2 = jnp.mean(y * y, axis=-1, keepdims=True)
            out_ref[rows, :] = x_ref[rows, :] + gate * ((y * lax.rsqrt(ms2 + NORM_EPS)) * gpost_ref[...])
        return gen

    def run_pipeline():
        tasks = [("proj", proj_task, [], 0)]
        for j in range(n_sub):
            tasks.append((("prep", j), prep_task(j), [("prep", j - 1)] if j else [], 4 * j))
            for cc in range(per_sub):
                c = j * per_sub + cc
                tasks.append((("rwkv", c), rwkv_task(c), [("prep", j)], 3 + 2 * c))
            for hk in range(ATTN_KV_HEADS):
                tasks.append((("attn", j, hk), attn_task(j, hk), [("prep", j)], 4 * j + 3 + hk))
        for c in range(n_chunk):
            tasks.append((("gla", c), gla_task(c), [("prep", c // per_sub)], 8 + 2 * c))
        for c in range(n_chunk):
            deps = [("rwkv", c), ("gla", c)] + ([("state", c - 1)] if c else [])
            tasks.append((("state", c), state_task(c), deps, 0))
        for j in range(n_sub):
            deps = [("state", (j + 1) * per_sub - 1)] + [("attn", j, hk) for hk in range(ATTN_KV_HEADS)]
            tasks.append((("finish", j), finish_task(j), deps, 0))
        for half in range(2):
            deps = [("finish", j) for j in range(half * n_sub // 2, (half + 1) * n_sub // 2)]
            tasks.append((("out", half), out_task(half), deps, 0))
        _run_tasks(tasks)

    run_pipeline()


def _ada_kernel(c_ref, w_ref, b_ref, o_ref):
    c = c_ref[...]
    c_act = jnp.broadcast_to(_silu(c), (8, D_MODEL))
    o_ref[0] = _mm_hi(c_act, w_ref[0])[0:1] + b_ref[0]


def _adaln(c, ada_w, ada_b):
    depth = ada_w.shape[0]
    tn = D_MODEL
    return pl.pallas_call(
        _ada_kernel,
        grid=(depth, 3 * D_MODEL // tn),
        in_specs=[pl.BlockSpec((1, D_MODEL), lambda l, n: (0, 0)),
                  pl.BlockSpec((1, D_MODEL, tn), lambda l, n: (l, 0, n)),
                  pl.BlockSpec((1, 1, tn), lambda l, n: (l, 0, n))],
        out_specs=pl.BlockSpec((1, 1, tn), lambda l, n: (l, 0, n)),
        out_shape=jax.ShapeDtypeStruct((depth, 1, 3 * D_MODEL), F32),
        compiler_params=pltpu.CompilerParams(dimension_semantics=("arbitrary", "arbitrary"),
                                             vmem_limit_bytes=VMEM_LIMIT_BYTES),
        name="adaln",
    )(c, ada_w, ada_b.reshape(depth, 1, 3 * D_MODEL))


def _gla_level_matrices():
    mg = np.zeros(((N_LEVELS + 1) * CH, CH), np.float32)
    for t in range(CH):
        mg[t, :t + 1] = 1.0
    for l in range(1, N_LEVELS + 1):
        s = 2 ** (l - 1)
        for t in range(CH):
            m = (t // (2 * s)) * 2 * s + s
            if t >= m:
                mg[l * CH + t, m:t + 1] = 1.0
            else:
                mg[l * CH + t, t + 1:m] = 1.0
    lvl = np.full((CH, CH), -1, np.int32)
    for t in range(CH):
        lvl[t, t] = 0
        for s_ in range(t):
            lvl[t, s_] = int(np.floor(np.log2(t ^ s_))) + 1
    return mg, np.tile(lvl, (1, GLA_HEADS))


def _constants():
    mg, lv = _gla_level_matrices()
    tri = np.tril(np.ones((CH, CH), np.float32))
    lt2 = np.kron(np.eye(SUB // CH, dtype=np.float32), tri)
    ones_bd = np.kron(np.eye(RWKV_WIDTH // HEAD_DIM, dtype=np.float32), np.ones((HEAD_DIM, HEAD_DIM), np.float32))
    twice = lambda m: np.concatenate([m, m], axis=1)
    return (jnp.asarray(twice(mg), MXU_DTYPE), jnp.asarray(lv), jnp.asarray(twice(lt2), MXU_DTYPE),
            jnp.asarray(ones_bd, MXU_DTYPE))


def _hi_lo(w):
    hi = w.astype(MXU_DTYPE)
    return hi, (w - hi.astype(F32)).astype(MXU_DTYPE)


def _layer_call(layer, x2d, sinks, stacked, consts):
    t = x2d.shape[0]
    n_blocks = t // TM

    def full(a):
        return pl.BlockSpec(a.shape, lambda i: (0,) * a.ndim)

    def of_layer(a):
        return pl.BlockSpec((None,) + a.shape[1:], lambda i: (layer,) + (0,) * (a.ndim - 1))

    row_spec = pl.BlockSpec((TM, D_MODEL), lambda i: (i, 0))
    next_spec = pl.BlockSpec((TM, D_MODEL), lambda i: (jnp.minimum(i + 1, n_blocks - 1), 0))
    operands = tuple(stacked) + tuple(consts)
    return pl.pallas_call(
        _layer_kernel,
        grid=(n_blocks,),
        in_specs=([pl.BlockSpec(memory_space=pltpu.SMEM), row_spec, next_spec]
                  + [of_layer(a) for a in stacked] + [full(a) for a in consts]),
        out_specs=row_spec,
        out_shape=jax.ShapeDtypeStruct((t, D_MODEL), F32),
        scratch_shapes=[
            pltpu.VMEM((2, TM, AUG_WIDTH), F32),
            pltpu.VMEM((2, TM, GLA_KEY_WIDTH), F32),
            pltpu.SMEM((2,), F32),
            pltpu.VMEM((TM, GLA_VAL_WIDTH), F32),
            pltpu.VMEM((GLA_VAL_WIDTH, GLA_KEY_WIDTH), F32),
            pltpu.VMEM((TM, D_MODEL), MXU_DTYPE),
            pltpu.VMEM((TM, D_MODEL), MXU_DTYPE),
            pltpu.VMEM((RWKV_WIDTH, RWKV_WIDTH), F32),
            pltpu.VMEM((GLA_VAL_WIDTH, GLA_KEY_WIDTH), F32),
            pltpu.VMEM((SUB, KV_WIDTH), F32),
            pltpu.VMEM((SUB, KV_WIDTH), F32),
            pltpu.VMEM((8, 3 * RWKV_WIDTH + 128), F32),
        ],
        compiler_params=pltpu.CompilerParams(dimension_semantics=("arbitrary",),
                                             vmem_limit_bytes=VMEM_LIMIT_BYTES),
        name="hybrid_layer",
    )(sinks, x2d, x2d, *operands)


def kernel(x, c, ada_w, ada_b, norm_pre, norm_post, w_in, w_out, attn_sinks, rwkv_mu_rkv, rwkv_mu_w, rwkv_mu_a, rwkv_w0, rwkv_w1, rwkv_w2, rwkv_a0, rwkv_a1, rwkv_a2, rwkv_k_k, rwkv_k_a, rwkv_r_k, rwkv_ln_w, rwkv_ln_b, gla_gk1, gla_gk2, gla_gk_b, gla_norm_w):
    batch, seq, _ = x.shape
    depth = w_in.shape[0]
    assert batch == 1 and seq % TM == 0

    mods = _adaln(c, ada_w, ada_b)

    mu_w = rwkv_mu_w[:, :, None]
    mu_a = rwkv_mu_a[:, :, None]
    x1 = jnp.concatenate([(1.0 - mu_w) * rwkv_w1, (1.0 - mu_a) * rwkv_a1], axis=-1)
    x2 = jnp.concatenate([mu_w * rwkv_w1, mu_a * rwkv_a1], axis=-1)
    x3 = jnp.pad(gla_gk1, ((0, 0), (0, 0), (0, 128 - GLA_GATE_LORA)))
    w_main = w_in.astype(MXU_DTYPE)
    w_aux = jnp.concatenate([x1, x2, x3], axis=-1).astype(MXU_DTYPE)
    wout = w_out.astype(MXU_DTYPE)
    zeros = jnp.zeros((depth, LORA, RWKV_WIDTH), F32)
    w2cat = jnp.concatenate([jnp.concatenate([rwkv_w2, zeros], axis=-1),
                             jnp.concatenate([zeros, rwkv_a2], axis=-1)], axis=1)
    w2cat = w2cat.astype(MXU_DTYPE)
    gk2p = jnp.pad(gla_gk2, ((0, 0), (0, 128 - GLA_GATE_LORA), (0, 0))).astype(MXU_DTYPE)
    rv = jnp.stack([rwkv_w0, rwkv_a0, rwkv_k_k, rwkv_k_a, rwkv_r_k.reshape(depth, RWKV_WIDTH),
                    rwkv_ln_w, rwkv_ln_b, jnp.tile(gla_norm_w, (1, GLA_HEADS))], axis=1)
    consts = _constants()
    stacked = (mods, norm_pre[:, None], norm_post[:, None], w_main, w_aux, wout, w2cat, gk2p,
               rwkv_mu_rkv[:, None], rv, gla_gk_b[:, None])

    xc = x[0]
    for l in range(depth):
        xc = _layer_call(l, xc, attn_sinks[l], stacked, consts)
    return xc[None]
```

```python
import numpy as np
import jax
import jax.numpy as jnp
from jax import lax
from jax.experimental import pallas as pl
from jax.experimental.pallas import tpu as pltpu

F32 = jnp.float32
MXU_DTYPE = jnp.bfloat16

D_MODEL = 1024
HEAD_DIM = 64
ATTN_WIDTH = 512
ATTN_Q_HEADS = 8
ATTN_KV_HEADS = 2
KV_WIDTH = 128
WINDOW = 128
RWKV_WIDTH = 256
RWKV_LN_EPS = 64e-5
LORA = 64
GLA_HEADS = 4
GLA_KEY_WIDTH = 128
GLA_KEY_DIM = 32
GLA_VAL_WIDTH = 256
GLA_GATE_LORA = 16
GLA_GATE_NORMALIZER = 16.0
NORM_EPS = 1e-6
IN_WIDTH = 3072

O_AQ, O_AK, O_AV, O_AG = 0, 512, 640, 768
O_RKV, O_RG = 1280, 2048
O_GQ, O_GK, O_GV, O_GG = 2304, 2432, 2560, 2816
O_X1, O_X2, O_X3 = 3072, 3200, 3328
AUG_WIDTH = 3456

TM = 512
SUB = 128
CH = 64
N_LEVELS = 6
PROJ_TILE = 256
GLA_SAFE_LOG = 60.0

VMEM_LIMIT_BYTES = 60000 * 1024


def _mm(a, b):
    return jnp.dot(a.astype(MXU_DTYPE), b.astype(MXU_DTYPE), preferred_element_type=F32)


def _mm_nt(a, b):
    return lax.dot_general(a.astype(MXU_DTYPE), b.astype(MXU_DTYPE), (((1,), (1,)), ((), ())),
                           preferred_element_type=F32)


def _mm_tn(a, b):
    return lax.dot_general(a.astype(MXU_DTYPE), b.astype(MXU_DTYPE), (((0,), (0,)), ((), ())),
                           preferred_element_type=F32)


def _split(x, n):
    parts, r = [], x
    for _ in range(n):
        p = r.astype(MXU_DTYPE)
        parts.append(p)
        r = r - p.astype(F32)
    return parts


def _sel_mm2(sel2, x):
    hi, lo = _split(x, 2)
    return jnp.dot(sel2, jnp.concatenate([hi, lo], axis=0), preferred_element_type=F32)


def _mm_sel(x, sel, n):
    acc = None
    for p in _split(x, n):
        t = jnp.dot(p, sel, preferred_element_type=F32)
        acc = t if acc is None else acc + t
    return acc


def _iota(shape, dim):
    return lax.broadcasted_iota(jnp.int32, shape, dim)


def _sigmoid(z):
    return 1.0 / (1.0 + jnp.exp(-z))


def _silu(z):
    return z * _sigmoid(z)


def _softplus(z):
    return jnp.maximum(z, 0.0) + jnp.log(1.0 + jnp.exp(-jnp.abs(z)))


def _block_diag(x, row_shift, col_shift):
    t = jnp.concatenate([x, x, x, x], axis=0)
    keep = (_iota(t.shape, 0) >> row_shift) == (_iota(t.shape, 1) >> col_shift)
    return jnp.where(keep, t, 0.0)


def _bd64(x):
    return _block_diag(x, 6, 6)


def _run_tasks(tasks):
    results, running, pending, rnd = {}, {}, list(tasks), 0
    while pending or running:
        for task in list(pending):
            name, make, deps, earliest = task
            if earliest <= rnd and all(d in results for d in deps):
                running[name] = make(results)
                pending.remove(task)
        for name in list(running):
            try:
                next(running[name])
            except StopIteration as stop:
                results[name] = stop.value
                del running[name]
        rnd += 1
        assert rnd < 1000, "task graph cannot make progress"
    return results


def _attention_setup(k_cur, v_cur, k_prev, v_prev, key_offset):
    kb = jnp.concatenate([k_prev, k_cur], axis=0)
    vb = jnp.concatenate([v_prev, v_cur], axis=0)
    lo_kv = _iota(kb.shape, 1) < HEAD_DIM
    kr = pltpu.roll(kb, HEAD_DIM, axis=1)
    vr = pltpu.roll(vb, HEAD_DIM, axis=1)
    k_dup = (jnp.where(lo_kv, kb, kr), jnp.where(lo_kv, kr, kb))
    v_dup = (jnp.where(lo_kv, vb, vr), jnp.where(lo_kv, vr, vb))
    row = _iota((SUB, 2 * SUB), 0)
    col = _iota((SUB, 2 * SUB), 1)
    dist = row - col + SUB
    valid = (dist >= 0) & (dist < WINDOW) & (col >= key_offset)
    return k_dup, v_dup, valid, dist.astype(F32)


def _attention_group(hk, q, k_dup, v_dup, valid, distf, sinks_ref):
    lo_q = _iota((SUB, 128), 1) < HEAD_DIM
    group = ATTN_Q_HEADS // ATTN_KV_HEADS
    base = hk * group * HEAD_DIM
    slab_a = q[:, base:base + 128]
    slab_b = q[:, base + 128:base + 256]
    lhs = jnp.concatenate([jnp.where(lo_q, slab_a, 0.0), jnp.where(lo_q, 0.0, slab_a),
                           jnp.where(lo_q, slab_b, 0.0), jnp.where(lo_q, 0.0, slab_b)], axis=0)
    s_all = _mm_nt(lhs, k_dup[hk])
    yield
    ps, invs = [], []
    for g in range(group):
        hq = hk * group + g
        slope = 2.0 ** (-8.0 * (hq + 1) / ATTN_Q_HEADS)
        s = s_all[g * SUB:(g + 1) * SUB] * (HEAD_DIM ** -0.5) - slope * distf
        s = jnp.where(valid, s, -jnp.inf)
        sink = sinks_ref[hq]
        m = jnp.maximum(jnp.max(s, axis=-1, keepdims=True), sink)
        p = jnp.exp(s - m)
        den = jnp.sum(p, axis=-1, keepdims=True) + jnp.exp(sink - m)
        ps.append(p)
        invs.append(1.0 / den)
    o_all = _mm(jnp.concatenate(ps, axis=0), v_dup[hk])
    yield
    o = [o_all[g * SUB:(g + 1) * SUB] * invs[g] for g in range(group)]
    return jnp.concatenate([jnp.where(lo_q, o[0], o[1]), jnp.where(lo_q, o[2], o[3])], axis=1)


def _rwkv_chunk_intra(at, bt, kt, rt, v):
    row = _iota((CH, RWKV_WIDTH), 0)
    col = _iota((CH, RWKV_WIDTH), 1) & (HEAD_DIM - 1)
    strict = col < row
    incl = col <= row

    lhs = jnp.concatenate([at, rt], axis=0)
    g_b = _mm_nt(lhs, _bd64(bt))
    g_k = _mm_nt(lhs, _bd64(kt))
    yield
    a_ab = jnp.where(strict, g_b[:CH], 0.0)
    a_rb = jnp.where(incl, g_b[CH:], 0.0)
    a_ak = jnp.where(strict, g_k[:CH], 0.0)
    a_rk = jnp.where(incl, g_k[CH:], 0.0)

    p = _mm(a_ab, _bd64(a_ab))
    av2 = _mm(jnp.concatenate([a_ak, a_rk], axis=0), _bd64(v))
    akv, ark_v = av2[:CH], av2[CH:]
    w = jnp.where(col == row, 1.0, 0.0) + a_ab
    yield
    for _ in range(N_LEVELS - 2):
        r = _mm(jnp.concatenate([p, w], axis=0), _bd64(p))
        yield
        w = w + r[CH:]
        p = r[:CH]
    t_inv = w + _mm(w, _bd64(p))
    yield
    at2 = _mm(t_inv, _bd64(at))
    uv = _mm(t_inv, _bd64(akv))
    return a_rb, ark_v, at2, uv


def _rwkv_chunk_state(a_rb, ark_v, at2, uv, rt, v, bp, kp, gam, st):
    x = _mm_nt(jnp.concatenate([at2, rt], axis=0), st)
    yield
    u = uv + x[:CH]
    y = x[CH:] + ark_v + _mm(a_rb, _bd64(u))
    upd = _mm_tn(jnp.concatenate([u, v], axis=0), jnp.concatenate([bp, kp], axis=0))
    keep = (_iota(upd.shape, 0) >> 6) == (_iota(upd.shape, 1) >> 6)
    st_new = gam * st + jnp.where(keep, upd, 0.0)
    return y, st_new


def _gla_level_scores(ql, kl):
    keep_k = (_iota((4 * CH, GLA_KEY_WIDTH), 0) >> 6) == (_iota((4 * CH, GLA_KEY_WIDTH), 1) >> 5)
    kt = jnp.concatenate([kl, kl, kl, kl], axis=0)
    return _mm_nt(ql, jnp.where(keep_k, kt, 0.0))


def _gla_state_update(v, k, b, b_last):
    upd = _mm_tn(v, k * jnp.exp(b_last - b))
    keep_s = (_iota(upd.shape, 0) >> 6) == (_iota(upd.shape, 1) >> 5)
    return jnp.where(keep_s, upd, 0.0)


def _gla_chunk_intra(qs, k, v, g, mg_ref, lv):
    b = _sel_mm2(mg_ref[0:CH, :], g)
    yield
    b_last = b[CH - 1:CH]
    scores = _gla_level_scores(qs * jnp.exp(b), k * jnp.exp(-jnp.maximum(b, -GLA_SAFE_LOG)))
    upd = _gla_state_update(v, k, b, b_last)
    yield
    o_intra = _mm(jnp.where(lv >= 0, scores, 0.0), _bd64(v))
    return o_intra, qs * jnp.exp(b), jnp.exp(b_last), upd


def _gla_chunk_state(o_intra, q_dec, decay, upd, st):
    return o_intra + _mm_nt(q_dec, st), decay * st + upd


def _gla_chunk_general(qs, k, v, g, mg_ref, lv, st):
    e_all = _sel_mm2(mg_ref[...], g)
    b = e_all[:CH]
    b_last = b[CH - 1:CH]
    a = jnp.where(lv == 0, _gla_level_scores(qs, k), 0.0)
    for l in range(1, N_LEVELS + 1):
        f = jnp.exp(e_all[l * CH:(l + 1) * CH])
        a = jnp.where(lv == l, _gla_level_scores(qs * f, k * f), a)
    o = _mm(a, _bd64(v)) + _mm_nt(qs * jnp.exp(b), st)
    return o, jnp.exp(b_last) * st + _gla_state_update(v, k, b, b_last)


def _layer_kernel(sinks_ref, x_ref, xn_ref, mod_ref, gpre_ref, gpost_ref, win_ref, waux_ref, wout_ref, w2_ref,
                  gk2_ref, mu_ref, rv_ref, gkb_ref, mg_ref, lv_ref, lt2_ref, ones_ref,
                  out_ref,
                  pbuf_ref, gbuf_ref, floor_ref, galt_ref, sgalt_ref, hn_ref, mix_ref, srt_ref, sgt_ref,
                  kprev_ref, vprev_ref, prev_ref):
    i = pl.program_id(0)
    n_sub = TM // SUB
    n_chunk = TM // CH
    per_sub = SUB // CH
    shift = mod_ref[:, 0:D_MODEL]
    scale = mod_ref[:, D_MODEL:2 * D_MODEL]
    gate = mod_ref[:, 2 * D_MODEL:3 * D_MODEL]
    g_mod = gpre_ref[...] * (1.0 + scale)

    def normed(x):
        ms = jnp.mean(x * x, axis=-1, keepdims=True)
        return ((x * lax.rsqrt(ms + NORM_EPS)) * g_mod + shift).astype(MXU_DTYPE)

    def log_gates(p_ref):
        graw = _mm(p_ref[:, O_X3:O_X3 + 128], gk2_ref[...]) + gkb_ref[...]
        g = -_softplus(-graw) * (1.0 / GLA_GATE_NORMALIZER)
        lowest = jnp.sum(g[0:CH], axis=0, keepdims=True)
        for c in range(1, n_chunk):
            lowest = jnp.minimum(lowest, jnp.sum(g[c * CH:(c + 1) * CH], axis=0, keepdims=True))
        return g, jnp.min(lowest)

    @pl.when(i == 0)
    def _first():
        srt_ref[...] = jnp.zeros(srt_ref.shape, F32)
        sgt_ref[...] = jnp.zeros(sgt_ref.shape, F32)
        kprev_ref[...] = jnp.zeros(kprev_ref.shape, F32)
        vprev_ref[...] = jnp.zeros(vprev_ref.shape, F32)
        prev_ref[...] = jnp.zeros(prev_ref.shape, F32)
        galt_ref[...] = jnp.zeros(galt_ref.shape, F32)
        sgalt_ref[...] = jnp.zeros(sgalt_ref.shape, F32)
        h0 = normed(x_ref[...])
        pbuf_ref[0, :, 0:IN_WIDTH] = jnp.dot(h0, win_ref[...], preferred_element_type=F32)
        pbuf_ref[0, :, IN_WIDTH:AUG_WIDTH] = jnp.dot(h0, waux_ref[...], preferred_element_type=F32)
        g0, lowest0 = log_gates(pbuf_ref.at[0])
        gbuf_ref[0] = g0
        floor_ref[0] = lowest0

    slot = i % 2
    proj_ref = pbuf_ref.at[slot]
    next_ref = pbuf_ref.at[1 - slot]

    gates_bounded = floor_ref[slot] > -GLA_SAFE_LOG

    @pl.when(jnp.logical_not(gates_bounded))
    def _general_gla():
        sgalt_ref[...] = sgt_ref[...]

        def chunk(c, carry):
            rows = pl.ds(pl.multiple_of(c * CH, CH), CH)
            o, st_new = _gla_chunk_general(
                proj_ref[rows, O_GQ:O_GQ + GLA_KEY_WIDTH] * (GLA_KEY_DIM ** -0.5),
                proj_ref[rows, O_GK:O_GK + GLA_KEY_WIDTH], proj_ref[rows, O_GV:O_GV + GLA_VAL_WIDTH],
                gbuf_ref[slot, rows, :], mg_ref, lv_ref[...], sgalt_ref[...])
            galt_ref[rows, :] = o
            sgalt_ref[...] = st_new
            return carry

        lax.fori_loop(0, n_chunk, chunk, 0)

    def proj_task(res):
        hn_ref[...] = normed(xn_ref[...])
        yield
        for w_ref, base in ((waux_ref, IN_WIDTH), (win_ref, 0)):
            width = w_ref.shape[1]
            for c0 in range(0, width, PROJ_TILE):
                c1 = min(c0 + PROJ_TILE, width)
                next_ref[:, base + c0:base + c1] = jnp.dot(hn_ref[...], w_ref[:, c0:c1],
                                                           preferred_element_type=F32)
                yield
                yield
            if w_ref is waux_ref:
                g_next, lowest = log_gates(next_ref)
                gbuf_ref[1 - slot] = g_next
                floor_ref[1 - slot] = lowest

    w0, a0 = rv_ref[0:1, :], rv_ref[1:2, :]
    k_k, k_a, r_k = rv_ref[2:3, :], rv_ref[3:4, :], rv_ref[4:5, :]
    ln_w, ln_b, gnw = rv_ref[5:6, :], rv_ref[6:7, :], rv_ref[7:8, :]

    def prep_task(j):
        def gen(res):
            rows = pl.ds(j * SUB, SUB)
            if j == 0:
                prev_row, k_prev, v_prev = prev_ref[0:1, :], kprev_ref[...], vprev_ref[...]
            else:
                before = res[("prep", j - 1)]
                prev_row, k_prev, v_prev = before["last_row"], before["k_cur"], before["v_cur"]
            k_cur = proj_ref[rows, O_AK:O_AK + KV_WIDTH]
            v_cur = proj_ref[rows, O_AV:O_AV + KV_WIDTH]
            cat = jnp.concatenate([proj_ref[rows, O_RKV:O_RKV + 3 * RWKV_WIDTH],
                                   proj_ref[rows, O_X2:O_X2 + 128]], axis=1)
            rolled = pltpu.roll(cat, 1, axis=0)
            shifted = jnp.where(_iota(cat.shape, 0) == 0, prev_row, rolled)
            last_row = cat[SUB - 1:SUB, :]
            if j == n_sub - 1:
                prev_ref[0:1, :] = last_row
                kprev_ref[...] = k_cur
                vprev_ref[...] = v_cur
            rkv = cat[:, :3 * RWKV_WIDTH]
            rkv = rkv + (shifted[:, :3 * RWKV_WIDTH] - rkv) * mu_ref[...]
            r = rkv[:, 0:RWKV_WIDTH]
            k = rkv[:, RWKV_WIDTH:2 * RWKV_WIDTH]
            v = rkv[:, 2 * RWKV_WIDTH:3 * RWKV_WIDTH]
            lin = proj_ref[rows, O_X1:O_X1 + 128] + shifted[:, 3 * RWKV_WIDTH:]
            z = jnp.where(_iota(lin.shape, 1) < LORA, jnp.tanh(lin), lin)
            lora = _mm(z, w2_ref[...])
            kk = k * k_k
            kk_ss = _mm_sel(kk * kk, ones_ref[...], 1)
            yield
            logw = -_softplus(-(lora[:, :RWKV_WIDTH] + w0)) - 0.5
            ld = -jnp.exp(logw)
            iclr = _sigmoid(lora[:, RWKV_WIDTH:] + a0)
            kk = kk / jnp.maximum(jnp.sqrt(kk_ss), 1e-12)
            kmod = k * (1.0 + (iclr - 1.0) * k_a)
            av = -kk
            bv = kk * iclr
            c = _sel_mm2(lt2_ref[...], ld)
            key_offset = jnp.where(jnp.logical_and(i == 0, j == 0), SUB, 0)
            k_dup, v_dup, valid, distf = _attention_setup(k_cur, v_cur, k_prev, v_prev, key_offset)
            yield
            c_last = jnp.where(_iota(c.shape, 0) < CH, c[CH - 1:CH, :], c[2 * CH - 1:2 * CH, :])
            e_neg = jnp.exp(-c)
            e_rem = jnp.exp(c_last - c)
            return dict(
                last_row=last_row, k_cur=k_cur, v_cur=v_cur,
                k_dup=k_dup, v_dup=v_dup, valid=valid, distf=distf,
                r=r, v=v, kmod=kmod,
                at=av * jnp.exp(c - ld), bt=bv * e_neg, kt=kmod * e_neg, rt=r * jnp.exp(c),
                bp=bv * e_rem, kp=kmod * e_rem, gam=jnp.exp(c_last),
                g=gbuf_ref[slot, rows, :],
                qs=proj_ref[rows, O_GQ:O_GQ + GLA_KEY_WIDTH] * (GLA_KEY_DIM ** -0.5),
                gk=proj_ref[rows, O_GK:O_GK + GLA_KEY_WIDTH],
                gv=proj_ref[rows, O_GV:O_GV + GLA_VAL_WIDTH])
        return gen

    def attn_task(j, hk):
        def gen(res):
            p = res[("prep", j)]
            q_all = proj_ref[pl.ds(j * SUB, SUB), O_AQ:O_AQ + ATTN_WIDTH]
            return (yield from _attention_group(hk, q_all, p["k_dup"], p["v_dup"], p["valid"], p["distf"],
                                                sinks_ref))
        return gen

    def chunk_rows(c):
        cc = c % per_sub
        return slice(cc * CH, (cc + 1) * CH)

    def rwkv_task(c):
        def gen(res):
            p, sl = res[("prep", c // per_sub)], chunk_rows(c)
            return (yield from _rwkv_chunk_intra(p["at"][sl], p["bt"][sl], p["kt"][sl], p["rt"][sl], p["v"][sl]))
        return gen

    def gla_task(c):
        def gen(res):
            p, sl = res[("prep", c // per_sub)], chunk_rows(c)
            return (yield from _gla_chunk_intra(p["qs"][sl], p["gk"][sl], p["gv"][sl], p["g"][sl],
                                                mg_ref, lv_ref[...]))
        return gen

    def state_task(c):
        def gen(res):
            if c == 0:
                st, sg = srt_ref[...], sgt_ref[...]
            else:
                st, sg = res[("state", c - 1)][2:]
            p, sl = res[("prep", c // per_sub)], chunk_rows(c)
            o_c, sg_new = _gla_chunk_state(*res[("gla", c)], sg)
            a_rb, ark_v, at2, uv = res[("rwkv", c)]
            y_c, st_new = yield from _rwkv_chunk_state(a_rb, ark_v, at2, uv, p["rt"][sl], p["v"][sl], p["bp"][sl],
                                                       p["kp"][sl], p["gam"][sl.start:sl.start + 1], st)
            if c == n_chunk - 1:
                srt_ref[...] = st_new
                sgt_ref[...] = jnp.where(gates_bounded, sg_new, sgalt_ref[...])
            return y_c, o_c, st_new, sg_new
        return gen

    def finish_task(j):
        def gen(res):
            rows = pl.ds(j * SUB, SUB)
            p = res[("prep", j)]
            parts = [res[("state", j * per_sub + cc)] for cc in range(per_sub)]
            y = jnp.concatenate([q[0] for q in parts], axis=0)
            o = jnp.where(gates_bounded, jnp.concatenate([q[1] for q in parts], axis=0), galt_ref[rows, :])
            attn =jnp.concatenate([res[("attn", j, hk)] for hk in range(ATTN_KV_HEADS)], axis=1)
            mix_ref[rows, 0:ATTN_WIDTH] = (attn * _silu(proj_ref[rows, O_AG:O_AG + ATTN_WIDTH])).astype(mix_ref.dtype)
            ones_bd = ones_ref[...]
            sums = _mm_sel(jnp.concatenate([y, p["r"] * p["kmod"] * r_k, o * o], axis=0), ones_bd, 1)
            mean = sums[0:SUB] * (1.0 / HEAD_DIM)
            bonus = sums[SUB:2 * SUB] * p["v"]
            oms = sums[2 * SUB:3 * SUB] * (1.0 / HEAD_DIM)
            yield
            yc = y - mean
            var = _mm_sel(yc * yc, ones_bd, 1) * (1.0 / HEAD_DIM)
            gla = (o * lax.rsqrt(oms + 1e-5)) * gnw * _silu(proj_ref[rows, O_GG:O_GG + GLA_VAL_WIDTH])
            mix_ref[rows, ATTN_WIDTH + RWKV_WIDTH:D_MODEL] = gla.astype(mix_ref.dtype)
            yield
            yn = (yc * lax.rsqrt(var + RWKV_LN_EPS)) * ln_w + ln_b
            rwkv = (yn + bonus) * _silu(proj_ref[rows, O_RG:O_RG + RWKV_WIDTH])
            mix_ref[rows, ATTN_WIDTH:ATTN_WIDTH + RWKV_WIDTH] = rwkv.astype(mix_ref.dtype)
        return gen

    def out_task(half):
        def gen(res):
            rows = pl.ds(half * (TM // 2), TM // 2)
            y = jnp.dot(mix_ref[rows, :], wout_ref[...], preferred_element_type=F32)
            yield
            ms2 = jnp.mean(y * y, axis=-1, keepdims=True)
            out_ref[rows, :] = x_ref[rows, :] + gate * ((y * lax.rsqrt(ms2 + NORM_EPS)) * gpost_ref[...])
        return gen

    def run_pipeline():
        tasks = [("proj", proj_task, [], 0)]
        for j in range(n_sub):
            tasks.append((("prep", j), prep_task(j), [("prep", j - 1)] if j else [], 4 * j))
            for cc in range(per_sub):
                c = j * per_sub + cc
                tasks.append((("rwkv", c), rwkv_task(c), [("prep", j)], 3 + 2 * c))
            for hk in range(ATTN_KV_HEADS):
                tasks.append((("attn", j, hk), attn_task(j, hk), [("prep", j)], 4 * j + 3 + hk))
        for c in range(n_chunk):
            tasks.append((("gla", c), gla_task(c), [("prep", c // per_sub)], 8 + 2 * c))
        for c in range(n_chunk):
            deps = [("rwkv", c), ("gla", c)] + ([("state", c - 1)] if c else [])
            tasks.append((("state", c), state_task(c), deps, 0))
        for j in range(n_sub):
            deps = [("state", (j + 1) * per_sub - 1)] + [("attn", j, hk) for hk in range(ATTN_KV_HEADS)]
            tasks.append((("finish", j), finish_task(j), deps, 0))
        for half in range(2):
            deps = [("finish", j) for j in range(half * n_sub // 2, (half + 1) * n_sub // 2)]
            tasks.append((("out", half), out_task(half), deps, 0))
        _run_tasks(tasks)

    run_pipeline()


def _ada_kernel(c_ref, w_ref, b_ref, o_ref):
    c_act = _silu(c_ref[...])
    o_ref[0] = jnp.sum(w_ref[0] * c_act, axis=0, keepdims=True) + b_ref[0]


def _adaln(c, ada_w, ada_b):
    depth = ada_w.shape[0]
    tn = D_MODEL
    return pl.pallas_call(
        _ada_kernel,
        grid=(depth, 3 * D_MODEL // tn),
        in_specs=[pl.BlockSpec((D_MODEL, 1), lambda l, n: (0, 0)),
                  pl.BlockSpec((1, D_MODEL, tn), lambda l, n: (l, 0, n)),
                  pl.BlockSpec((1, 1, tn), lambda l, n: (l, 0, n))],
        out_specs=pl.BlockSpec((1, 1, tn), lambda l, n: (l, 0, n)),
        out_shape=jax.ShapeDtypeStruct((depth, 1, 3 * D_MODEL), F32),
        compiler_params=pltpu.CompilerParams(dimension_semantics=("arbitrary", "arbitrary"),
                                             vmem_limit_bytes=VMEM_LIMIT_BYTES),
        name="adaln",
    )(c.reshape(D_MODEL, 1), ada_w, ada_b.reshape(depth, 1, 3 * D_MODEL))


def _gla_level_matrices():
    mg = np.zeros(((N_LEVELS + 1) * CH, CH), np.float32)
    for t in range(CH):
        mg[t, :t + 1] = 1.0
    for l in range(1, N_LEVELS + 1):
        s = 2 ** (l - 1)
        for t in range(CH):
            m = (t // (2 * s)) * 2 * s + s
            if t >= m:
                mg[l * CH + t, m:t + 1] = 1.0
            else:
                mg[l * CH + t, t + 1:m] = 1.0
    lvl = np.full((CH, CH), -1, np.int32)
    for t in range(CH):
        lvl[t, t] = 0
        for s_ in range(t):
            lvl[t, s_] = int(np.floor(np.log2(t ^ s_))) + 1
    return mg, np.tile(lvl, (1, GLA_HEADS))


def _constants():
    mg, lv = _gla_level_matrices()
    tri = np.tril(np.ones((CH, CH), np.float32))
    lt2 = np.kron(np.eye(SUB // CH, dtype=np.float32), tri)
    ones_bd = np.kron(np.eye(RWKV_WIDTH // HEAD_DIM, dtype=np.float32), np.ones((HEAD_DIM, HEAD_DIM), np.float32))
    twice = lambda m: np.concatenate([m, m], axis=1)
    return (jnp.asarray(twice(mg), MXU_DTYPE), jnp.asarray(lv), jnp.asarray(twice(lt2), MXU_DTYPE),
            jnp.asarray(ones_bd, MXU_DTYPE))


def _layer_call(layer, x2d, sinks, stacked, consts):
    t = x2d.shape[0]
    n_blocks = t // TM

    def full(a):
        return pl.BlockSpec(a.shape, lambda i: (0,) * a.ndim)

    def of_layer(a):
        return pl.BlockSpec((None,) + a.shape[1:], lambda i: (layer,) + (0,) * (a.ndim - 1))

    row_spec = pl.BlockSpec((TM, D_MODEL), lambda i: (i, 0))
    next_spec = pl.BlockSpec((TM, D_MODEL), lambda i: (jnp.minimum(i + 1, n_blocks - 1), 0))
    operands = tuple(stacked) + tuple(consts)
    return pl.pallas_call(
        _layer_kernel,
        grid=(n_blocks,),
        in_specs=([pl.BlockSpec(memory_space=pltpu.SMEM), row_spec, next_spec]
                  + [of_layer(a) for a in stacked] + [full(a) for a in consts]),
        out_specs=row_spec,
        out_shape=jax.ShapeDtypeStruct((t, D_MODEL), F32),
        scratch_shapes=[
            pltpu.VMEM((2, TM, AUG_WIDTH), F32),
            pltpu.VMEM((2, TM, GLA_KEY_WIDTH), F32),
            pltpu.SMEM((2,), F32),
            pltpu.VMEM((TM, GLA_VAL_WIDTH), F32),
            pltpu.VMEM((GLA_VAL_WIDTH, GLA_KEY_WIDTH), F32),
            pltpu.VMEM((TM, D_MODEL), MXU_DTYPE),
            pltpu.VMEM((TM, D_MODEL), MXU_DTYPE),
            pltpu.VMEM((RWKV_WIDTH, RWKV_WIDTH), F32),
            pltpu.VMEM((GLA_VAL_WIDTH, GLA_KEY_WIDTH), F32),
            pltpu.VMEM((SUB, KV_WIDTH), F32),
            pltpu.VMEM((SUB, KV_WIDTH), F32),
            pltpu.VMEM((8, 3 * RWKV_WIDTH + 128), F32),
        ],
        compiler_params=pltpu.CompilerParams(dimension_semantics=("arbitrary",),
                                             vmem_limit_bytes=VMEM_LIMIT_BYTES),
        name="hybrid_layer",
    )(sinks, x2d, x2d, *operands)


def kernel(x, c, ada_w, ada_b, norm_pre, norm_post, w_in, w_out, attn_sinks, rwkv_mu_rkv, rwkv_mu_w, rwkv_mu_a, rwkv_w0, rwkv_w1, rwkv_w2, rwkv_a0, rwkv_a1, rwkv_a2, rwkv_k_k, rwkv_k_a, rwkv_r_k, rwkv_ln_w, rwkv_ln_b, gla_gk1, gla_gk2, gla_gk_b, gla_norm_w):
    batch, seq, _ = x.shape
    depth = w_in.shape[0]
    assert batch == 1 and seq % TM == 0

    mods = _adaln(c, ada_w, ada_b)

    mu_w = rwkv_mu_w[:, :, None]
    mu_a = rwkv_mu_a[:, :, None]
    x1 = jnp.concatenate([(1.0 - mu_w) * rwkv_w1, (1.0 - mu_a) * rwkv_a1], axis=-1)
    x2 = jnp.concatenate([mu_w * rwkv_w1, mu_a * rwkv_a1], axis=-1)
    x3 = jnp.pad(gla_gk1, ((0, 0), (0, 0), (0, 128 - GLA_GATE_LORA)))
    w_main = w_in.astype(MXU_DTYPE)
    w_aux = jnp.concatenate([x1, x2, x3], axis=-1).astype(MXU_DTYPE)
    wout = w_out.astype(MXU_DTYPE)
    zeros = jnp.zeros((depth, LORA, RWKV_WIDTH), F32)
    w2cat = jnp.concatenate([jnp.concatenate([rwkv_w2, zeros], axis=-1),
                             jnp.concatenate([zeros, rwkv_a2], axis=-1)], axis=1)
    w2cat = w2cat.astype(MXU_DTYPE)
    gk2p = jnp.pad(gla_gk2, ((0, 0), (0, 128 - GLA_GATE_LORA), (0, 0))).astype(MXU_DTYPE)
    rv = jnp.stack([rwkv_w0, rwkv_a0, rwkv_k_k, rwkv_k_a, rwkv_r_k.reshape(depth, RWKV_WIDTH),
                    rwkv_ln_w, rwkv_ln_b, jnp.tile(gla_norm_w, (1, GLA_HEADS))], axis=1)
    consts = _constants()
    stacked = (mods, norm_pre[:, None], norm_post[:, None], w_main, w_aux, wout, w2cat, gk2p,
               rwkv_mu_rkv[:, None], rv, gla_gk_b[:, None])

    xc = x[0]
    for l in range(depth):
        xc = _layer_call(l, xc, attn_sinks[l], stacked, consts)
    return xc[None]
```

```python
import numpy as np
import jax
import jax.numpy as jnp
from jax import lax
from jax.experimental import pallas as pl
from jax.experimental.pallas import tpu as pltpu

F32 = jnp.float32
MXU_DTYPE = jnp.bfloat16

D_MODEL = 1024
HEAD_DIM = 64
ATTN_WIDTH = 512
ATTN_Q_HEADS = 8
ATTN_KV_HEADS = 2
KV_WIDTH = 128
WINDOW = 128
RWKV_WIDTH = 256
RWKV_LN_EPS = 64e-5
LORA = 64
GLA_HEADS = 4
GLA_KEY_WIDTH = 128
GLA_KEY_DIM = 32
GLA_VAL_WIDTH = 256
GLA_GATE_LORA = 16
GLA_GATE_NORMALIZER = 16.0
NORM_EPS = 1e-6
IN_WIDTH = 3072

O_AQ, O_AK, O_AV, O_AG = 0, 512, 640, 768
O_RKV, O_RG = 1280, 2048
O_GQ, O_GK, O_GV, O_GG = 2304, 2432, 2560, 2816
O_X1, O_X2, O_X3 = 3072, 3200, 3328
AUG_WIDTH = 3456

TM = 512
SUB = 128
CH = 64
N_LEVELS = 6
PROJ_TILE = 256
GLA_SAFE_LOG = 60.0

VMEM_LIMIT_BYTES = 60000 * 1024


def _mm(a, b):
    return jnp.dot(a.astype(MXU_DTYPE), b.astype(MXU_DTYPE), preferred_element_type=F32)


def _mm_nt(a, b):
    return lax.dot_general(a.astype(MXU_DTYPE), b.astype(MXU_DTYPE), (((1,), (1,)), ((), ())),
                           preferred_element_type=F32)


def _mm_tn(a, b):
    return lax.dot_general(a.astype(MXU_DTYPE), b.astype(MXU_DTYPE), (((0,), (0,)), ((), ())),
                           preferred_element_type=F32)


def _split(x, n):
    parts, r = [], x
    for _ in range(n):
        p = r.astype(MXU_DTYPE)
        parts.append(p)
        r = r - p.astype(F32)
    return parts


def _sel_mm2(sel2, x):
    hi, lo = _split(x, 2)
    return jnp.dot(sel2, jnp.concatenate([hi, lo], axis=0), preferred_element_type=F32)


def _mm_sel(x, sel, n):
    acc = None
    for p in _split(x, n):
        t = jnp.dot(p, sel, preferred_element_type=F32)
        acc = t if acc is None else acc + t
    return acc


def _iota(shape, dim):
    return lax.broadcasted_iota(jnp.int32, shape, dim)


def _sigmoid(z):
    return 1.0 / (1.0 + jnp.exp(-z))


def _silu(z):
    return z * _sigmoid(z)


def _softplus(z):
    return jnp.maximum(z, 0.0) + jnp.log(1.0 + jnp.exp(-jnp.abs(z)))


def _block_diag(x, row_shift, col_shift):
    t = jnp.concatenate([x, x, x, x], axis=0)
    keep = (_iota(t.shape, 0) >> row_shift) == (_iota(t.shape, 1) >> col_shift)
    return jnp.where(keep, t, 0.0)


def _bd64(x):
    return _block_diag(x, 6, 6)


def _run_tasks(tasks):
    results, running, pending, rnd = {}, {}, list(tasks), 0
    while pending or running:
        for task in list(pending):
            name, make, deps, earliest = task
            if earliest <= rnd and all(d in results for d in deps):
                running[name] = make(results)
                pending.remove(task)
        for name in list(running):
            try:
                next(running[name])
            except StopIteration as stop:
                results[name] = stop.value
                del running[name]
        rnd += 1
        assert rnd < 1000, "task graph cannot make progress"
    return results


def _attention_setup(k_cur, v_cur, k_prev, v_prev, key_offset):
    kb = jnp.concatenate([k_prev, k_cur], axis=0)
    vb = jnp.concatenate([v_prev, v_cur], axis=0)
    lo_kv = _iota(kb.shape, 1) < HEAD_DIM
    kr = pltpu.roll(kb, HEAD_DIM, axis=1)
    vr = pltpu.roll(vb, HEAD_DIM, axis=1)
    k_dup = (jnp.where(lo_kv, kb, kr), jnp.where(lo_kv, kr, kb))
    v_dup = (jnp.where(lo_kv, vb, vr), jnp.where(lo_kv, vr, vb))
    row = _iota((SUB, 2 * SUB), 0)
    col = _iota((SUB, 2 * SUB), 1)
    dist = row - col + SUB
    valid = (dist >= 0) & (dist < WINDOW) & (col >= key_offset)
    return k_dup, v_dup, valid, dist.astype(F32)


def _attention_group(hk, q, k_dup, v_dup, valid, distf, sinks_ref):
    lo_q = _iota((SUB, 128), 1) < HEAD_DIM
    group = ATTN_Q_HEADS // ATTN_KV_HEADS
    base = hk * group * HEAD_DIM
    slab_a = q[:, base:base + 128]
    slab_b = q[:, base + 128:base + 256]
    lhs = jnp.concatenate([jnp.where(lo_q, slab_a, 0.0), jnp.where(lo_q, 0.0, slab_a),
                           jnp.where(lo_q, slab_b, 0.0), jnp.where(lo_q, 0.0, slab_b)], axis=0)
    s_all = _mm_nt(lhs, k_dup[hk])
    yield
    ps, invs = [], []
    for g in range(group):
        hq = hk * group + g
        slope = 2.0 ** (-8.0 * (hq + 1) / ATTN_Q_HEADS)
        s = s_all[g * SUB:(g + 1) * SUB] * (HEAD_DIM ** -0.5) - slope * distf
        s = jnp.where(valid, s, -jnp.inf)
        sink = sinks_ref[hq]
        m = jnp.maximum(jnp.max(s, axis=-1, keepdims=True), sink)
        p = jnp.exp(s - m)
        den = jnp.sum(p, axis=-1, keepdims=True) + jnp.exp(sink - m)
        ps.append(p)
        invs.append(1.0 / den)
    o_all = _mm(jnp.concatenate(ps, axis=0), v_dup[hk])
    yield
    o = [o_all[g * SUB:(g + 1) * SUB] * invs[g] for g in range(group)]
    return jnp.concatenate([jnp.where(lo_q, o[0], o[1]), jnp.where(lo_q, o[2], o[3])], axis=1)


def _rwkv_chunk_intra(at, bt, kt, rt, v):
    row = _iota((CH, RWKV_WIDTH), 0)
    col = _iota((CH, RWKV_WIDTH), 1) & (HEAD_DIM - 1)
    strict = col < row
    incl = col <= row

    lhs = jnp.concatenate([at, rt], axis=0)
    g_b = _mm_nt(lhs, _bd64(bt))
    g_k = _mm_nt(lhs, _bd64(kt))
    yield
    a_ab = jnp.where(strict, g_b[:CH], 0.0)
    a_rb = jnp.where(incl, g_b[CH:], 0.0)
    a_ak = jnp.where(strict, g_k[:CH], 0.0)
    a_rk = jnp.where(incl, g_k[CH:], 0.0)

    p = _mm(a_ab, _bd64(a_ab))
    av2 = _mm(jnp.concatenate([a_ak, a_rk], axis=0), _bd64(v))
    akv, ark_v = av2[:CH], av2[CH:]
    w = jnp.where(col == row, 1.0, 0.0) + a_ab
    yield
    for _ in range(N_LEVELS - 2):
        r = _mm(jnp.concatenate([p, w], axis=0), _bd64(p))
        yield
        w = w + r[CH:]
        p = r[:CH]
    t_inv = w + _mm(w, _bd64(p))
    yield
    at2 = _mm(t_inv, _bd64(at))
    uv = _mm(t_inv, _bd64(akv))
    return a_rb, ark_v, at2, uv


def _rwkv_chunk_state(a_rb, ark_v, at2, uv, rt, v, bp, kp, gam, st):
    x = _mm_nt(jnp.concatenate([at2, rt], axis=0), st)
    yield
    u = uv + x[:CH]
    y = x[CH:] + ark_v + _mm(a_rb, _bd64(u))
    upd = _mm_tn(jnp.concatenate([u, v], axis=0), jnp.concatenate([bp, kp], axis=0))
    keep = (_iota(upd.shape, 0) >> 6) == (_iota(upd.shape, 1) >> 6)
    st_new = gam * st + jnp.where(keep, upd, 0.0)
    return y, st_new


def _gla_level_scores(ql, kl):
    keep_k = (_iota((4 * CH, GLA_KEY_WIDTH), 0) >> 6) == (_iota((4 * CH, GLA_KEY_WIDTH), 1) >> 5)
    kt = jnp.concatenate([kl, kl, kl, kl], axis=0)
    return _mm_nt(ql, jnp.where(keep_k, kt, 0.0))


def _gla_state_update(v, k, b, b_last):
    upd = _mm_tn(v, k * jnp.exp(b_last - b))
    keep_s = (_iota(upd.shape, 0) >> 6) == (_iota(upd.shape, 1) >> 5)
    return jnp.where(keep_s, upd, 0.0)


def _gla_chunk_intra(qs, k, v, g, mg_ref, lv):
    b = _sel_mm2(mg_ref[0:CH, :], g)
    yield
    b_last = b[CH - 1:CH]
    scores = _gla_level_scores(qs * jnp.exp(b), k * jnp.exp(-jnp.maximum(b, -GLA_SAFE_LOG)))
    upd = _gla_state_update(v, k, b, b_last)
    yield
    o_intra = _mm(jnp.where(lv >= 0, scores, 0.0), _bd64(v))
    return o_intra, qs * jnp.exp(b), jnp.exp(b_last), upd


def _gla_chunk_state(o_intra, q_dec, decay, upd, st):
    return o_intra + _mm_nt(q_dec, st), decay * st + upd


def _gla_chunk_general(qs, k, v, g, mg_ref, lv, st):
    e_all = _sel_mm2(mg_ref[...], g)
    b = e_all[:CH]
    b_last = b[CH - 1:CH]
    a = jnp.where(lv == 0, _gla_level_scores(qs, k), 0.0)
    for l in range(1, N_LEVELS + 1):
        f = jnp.exp(e_all[l * CH:(l + 1) * CH])
        a = jnp.where(lv == l, _gla_level_scores(qs * f, k * f), a)
    o = _mm(a, _bd64(v)) + _mm_nt(qs * jnp.exp(b), st)
    return o, jnp.exp(b_last) * st + _gla_state_update(v, k, b, b_last)


N_LAYER_INPUTS = 18


def _layer_kernel(*refs):
    _layer_body(*refs)


def _layer_kernel_convert(*refs):
    ins, (wfn_ref, won_ref) = refs[:N_LAYER_INPUTS], refs[N_LAYER_INPUTS:N_LAYER_INPUTS + 2]
    out_ref, wbn_ref, wobn_ref = refs[N_LAYER_INPUTS + 2:N_LAYER_INPUTS + 5]
    _layer_body(*ins, out_ref, *refs[N_LAYER_INPUTS + 5:], convert=(wfn_ref, won_ref, wbn_ref, wobn_ref))


def _layer_body(sinks_ref, x_ref, xn_ref, mod_ref, gpre_ref, gpost_ref, win_ref, waux_ref, wout_ref, w2_ref,
                gk2_ref, mu_ref, rv_ref, gkb_ref, mg_ref, lv_ref, lt2_ref, ones_ref,
                out_ref,
                pbuf_ref, gbuf_ref, floor_ref, galt_ref, sgalt_ref, hn_ref, mix_ref, srt_ref, sgt_ref,
                kprev_ref, vprev_ref, prev_ref, convert=None):
    i = pl.program_id(0)
    n_sub = TM // SUB
    n_chunk = TM // CH
    per_sub = SUB // CH
    shift = mod_ref[:, 0:D_MODEL]
    scale = mod_ref[:, D_MODEL:2 * D_MODEL]
    gate = mod_ref[:, 2 * D_MODEL:3 * D_MODEL]
    g_mod = gpre_ref[...] * (1.0 + scale)

    def normed(x):
        ms = jnp.mean(x * x, axis=-1, keepdims=True)
        return ((x * lax.rsqrt(ms + NORM_EPS)) * g_mod + shift).astype(MXU_DTYPE)

    def log_gates(p_ref):
        graw = _mm(p_ref[:, O_X3:O_X3 + 128], gk2_ref[...]) + gkb_ref[...]
        g = -_softplus(-graw) * (1.0 / GLA_GATE_NORMALIZER)
        lowest = jnp.sum(g[0:CH], axis=0, keepdims=True)
        for c in range(1, n_chunk):
            lowest = jnp.minimum(lowest, jnp.sum(g[c * CH:(c + 1) * CH], axis=0, keepdims=True))
        return g, jnp.min(lowest)

    @pl.when(i == 0)
    def _first():
        srt_ref[...] = jnp.zeros(srt_ref.shape, F32)
        sgt_ref[...] = jnp.zeros(sgt_ref.shape, F32)
        kprev_ref[...] = jnp.zeros(kprev_ref.shape, F32)
        vprev_ref[...] = jnp.zeros(vprev_ref.shape, F32)
        prev_ref[...] = jnp.zeros(prev_ref.shape, F32)
        galt_ref[...] = jnp.zeros(galt_ref.shape, F32)
        sgalt_ref[...] = jnp.zeros(sgalt_ref.shape, F32)
        h0 = normed(x_ref[...])
        pbuf_ref[0, :, 0:IN_WIDTH] = jnp.dot(h0, win_ref[...], preferred_element_type=F32)
        pbuf_ref[0, :, IN_WIDTH:AUG_WIDTH] = jnp.dot(h0, waux_ref[...], preferred_element_type=F32)
        g0, lowest0 = log_gates(pbuf_ref.at[0])
        gbuf_ref[0] = g0
        floor_ref[0] = lowest0

    slot = i % 2
    proj_ref = pbuf_ref.at[slot]
    next_ref = pbuf_ref.at[1 - slot]

    gates_bounded = floor_ref[slot] > -GLA_SAFE_LOG

    @pl.when(jnp.logical_not(gates_bounded))
    def _general_gla():
        sgalt_ref[...] = sgt_ref[...]

        def chunk(c, carry):
            rows = pl.ds(pl.multiple_of(c * CH, CH), CH)
            o, st_new = _gla_chunk_general(
                proj_ref[rows, O_GQ:O_GQ + GLA_KEY_WIDTH] * (GLA_KEY_DIM ** -0.5),
                proj_ref[rows, O_GK:O_GK + GLA_KEY_WIDTH], proj_ref[rows, O_GV:O_GV + GLA_VAL_WIDTH],
                gbuf_ref[slot, rows, :], mg_ref, lv_ref[...], sgalt_ref[...])
            galt_ref[rows, :] = o
            sgalt_ref[...] = st_new
            return carry

        lax.fori_loop(0, n_chunk, chunk, 0)

    def proj_task(res):
        hn_ref[...] = normed(xn_ref[...])
        yield
        for w_ref, base in ((waux_ref, IN_WIDTH), (win_ref, 0)):
            width = w_ref.shape[1]
            for c0 in range(0, width, PROJ_TILE):
                c1 = min(c0 + PROJ_TILE, width)
                next_ref[:, base + c0:base + c1] = jnp.dot(hn_ref[...], w_ref[:, c0:c1],
                                                           preferred_element_type=F32)
                yield
                yield
            if w_ref is waux_ref:
                g_next, lowest = log_gates(next_ref)
                gbuf_ref[1 - slot] = g_next
                floor_ref[1 - slot] = lowest

    w0, a0 = rv_ref[0:1, :], rv_ref[1:2, :]
    k_k, k_a, r_k = rv_ref[2:3, :], rv_ref[3:4, :], rv_ref[4:5, :]
    ln_w, ln_b, gnw = rv_ref[5:6, :], rv_ref[6:7, :], rv_ref[7:8, :]

    def prep_task(j):
        def gen(res):
            rows = pl.ds(j * SUB, SUB)
            if j == 0:
                prev_row, k_prev, v_prev = prev_ref[0:1, :], kprev_ref[...], vprev_ref[...]
            else:
                before = res[("prep", j - 1)]
                prev_row, k_prev, v_prev = before["last_row"], before["k_cur"], before["v_cur"]
            k_cur = proj_ref[rows, O_AK:O_AK + KV_WIDTH]
            v_cur = proj_ref[rows, O_AV:O_AV + KV_WIDTH]
            cat = jnp.concatenate([proj_ref[rows, O_RKV:O_RKV + 3 * RWKV_WIDTH],
                                   proj_ref[rows, O_X2:O_X2 + 128]], axis=1)
            rolled = pltpu.roll(cat, 1, axis=0)
            shifted = jnp.where(_iota(cat.shape, 0) == 0, prev_row, rolled)
            last_row = cat[SUB - 1:SUB, :]
            if j == n_sub - 1:
                prev_ref[0:1, :] = last_row
                kprev_ref[...] = k_cur
                vprev_ref[...] = v_cur
            rkv = cat[:, :3 * RWKV_WIDTH]
            rkv = rkv + (shifted[:, :3 * RWKV_WIDTH] - rkv) * mu_ref[...]
            r = rkv[:, 0:RWKV_WIDTH]
            k = rkv[:, RWKV_WIDTH:2 * RWKV_WIDTH]
            v = rkv[:, 2 * RWKV_WIDTH:3 * RWKV_WIDTH]
            lin = proj_ref[rows, O_X1:O_X1 + 128] + shifted[:, 3 * RWKV_WIDTH:]
            z = jnp.where(_iota(lin.shape, 1) < LORA, jnp.tanh(lin), lin)
            lora = _mm(z, w2_ref[...])
            kk = k * k_k
            kk_ss = _mm_sel(kk * kk, ones_ref[...], 1)
            yield
            logw = -_softplus(-(lora[:, :RWKV_WIDTH] + w0)) - 0.5
            ld = -jnp.exp(logw)
            iclr = _sigmoid(lora[:, RWKV_WIDTH:] + a0)
            kk = kk / jnp.maximum(jnp.sqrt(kk_ss), 1e-12)
            kmod = k * (1.0 + (iclr - 1.0) * k_a)
            av = -kk
            bv = kk * iclr
            c = _sel_mm2(lt2_ref[...], ld)
            key_offset = jnp.where(jnp.logical_and(i == 0, j == 0), SUB, 0)
            k_dup, v_dup, valid, distf = _attention_setup(k_cur, v_cur, k_prev, v_prev, key_offset)
            yield
            c_last = jnp.where(_iota(c.shape, 0) < CH, c[CH - 1:CH, :], c[2 * CH - 1:2 * CH, :])
            e_neg = jnp.exp(-c)
            e_rem = jnp.exp(c_last - c)
            return dict(
                last_row=last_row, k_cur=k_cur, v_cur=v_cur,
                k_dup=k_dup, v_dup=v_dup, valid=valid, distf=distf,
                r=r, v=v, kmod=kmod,
                at=av * jnp.exp(c - ld), bt=bv * e_neg, kt=kmod * e_neg, rt=r * jnp.exp(c),
                bp=bv * e_rem, kp=kmod * e_rem, gam=jnp.exp(c_last),
                g=gbuf_ref[slot, rows, :],
                qs=proj_ref[rows, O_GQ:O_GQ + GLA_KEY_WIDTH] * (GLA_KEY_DIM ** -0.5),
                gk=proj_ref[rows, O_GK:O_GK + GLA_KEY_WIDTH],
                gv=proj_ref[rows, O_GV:O_GV + GLA_VAL_WIDTH])
        return gen

    def attn_task(j, hk):
        def gen(res):
            p = res[("prep", j)]
            q_all = proj_ref[pl.ds(j * SUB, SUB), O_AQ:O_AQ + ATTN_WIDTH]
            return (yield from _attention_group(hk, q_all, p["k_dup"], p["v_dup"], p["valid"], p["distf"],
                                                sinks_ref))
        return gen

    def chunk_rows(c):
        cc = c % per_sub
        return slice(cc * CH, (cc + 1) * CH)

    def rwkv_task(c):
        def gen(res):
            p, sl = res[("prep", c // per_sub)], chunk_rows(c)
            return (yield from _rwkv_chunk_intra(p["at"][sl], p["bt"][sl], p["kt"][sl], p["rt"][sl], p["v"][sl]))
        return gen

    def gla_task(c):
        def gen(res):
            p, sl = res[("prep", c // per_sub)], chunk_rows(c)
            return (yield from _gla_chunk_intra(p["qs"][sl], p["gk"][sl], p["gv"][sl], p["g"][sl],
                                                mg_ref, lv_ref[...]))
        return gen

    def state_task(c):
        def gen(res):
            if c == 0:
                st, sg = srt_ref[...], sgt_ref[...]
            else:
                st, sg = res[("state", c - 1)][2:]
            p, sl = res[("prep", c // per_sub)], chunk_rows(c)
            o_c, sg_new = _gla_chunk_state(*res[("gla", c)], sg)
            a_rb, ark_v, at2, uv = res[("rwkv", c)]
            y_c, st_new = yield from _rwkv_chunk_state(a_rb, ark_v, at2, uv, p["rt"][sl], p["v"][sl], p["bp"][sl],
                                                       p["kp"][sl], p["gam"][sl.start:sl.start + 1], st)
            if c == n_chunk - 1:
                srt_ref[...] = st_new
                sgt_ref[...] = jnp.where(gates_bounded, sg_new, sgalt_ref[...])
            return y_c, o_c, st_new, sg_new
        return gen

    def finish_task(j):
        def gen(res):
            rows = pl.ds(j * SUB, SUB)
            p = res[("prep", j)]
            parts = [res[("state", j * per_sub + cc)] for cc in range(per_sub)]
            y = jnp.concatenate([q[0] for q in parts], axis=0)
            o = jnp.where(gates_bounded, jnp.concatenate([q[1] for q in parts], axis=0), galt_ref[rows, :])
            attn =jnp.concatenate([res[("attn", j, hk)] for hk in range(ATTN_KV_HEADS)], axis=1)
            mix_ref[rows, 0:ATTN_WIDTH] = (attn * _silu(proj_ref[rows, O_AG:O_AG + ATTN_WIDTH])).astype(mix_ref.dtype)
            ones_bd = ones_ref[...]
            sums = _mm_sel(jnp.concatenate([y, p["r"] * p["kmod"] * r_k, o * o], axis=0), ones_bd, 1)
            mean = sums[0:SUB] * (1.0 / HEAD_DIM)
            bonus = sums[SUB:2 * SUB] * p["v"]
            oms = sums[2 * SUB:3 * SUB] * (1.0 / HEAD_DIM)
            yield
            yc = y - mean
            var = _mm_sel(yc * yc, ones_bd, 1) * (1.0 / HEAD_DIM)
            gla = (o * lax.rsqrt(oms + 1e-5)) * gnw * _silu(proj_ref[rows, O_GG:O_GG + GLA_VAL_WIDTH])
            mix_ref[rows, ATTN_WIDTH + RWKV_WIDTH:D_MODEL] = gla.astype(mix_ref.dtype)
            yield
            yn = (yc * lax.rsqrt(var + RWKV_LN_EPS)) * ln_w + ln_b
            rwkv = (yn + bonus) * _silu(proj_ref[rows, O_RG:O_RG + RWKV_WIDTH])
            mix_ref[rows, ATTN_WIDTH:ATTN_WIDTH + RWKV_WIDTH] = rwkv.astype(mix_ref.dtype)
        return gen

    def out_task(half):
        def gen(res):
            rows = pl.ds(half * (TM // 2), TM // 2)
            y = jnp.dot(mix_ref[rows, :], wout_ref[...], preferred_element_type=F32)
            yield
            ms2 = jnp.mean(y * y, axis=-1, keepdims=True)
            out_ref[rows, :] = x_ref[rows, :] + gate * ((y * lax.rsqrt(ms2 + NORM_EPS)) * gpost_ref[...])
        return gen

    def convert_task(res):
        wfn_ref, won_ref, wbn_ref, wobn_ref = convert
        wbn_ref[...] = wfn_ref[...].astype(wbn_ref.dtype)
        wobn_ref[...] = won_ref[...].astype(wobn_ref.dtype)
        return None
        yield

    def run_pipeline():
        tasks = [("proj", proj_task, [], 0)]
        if convert is not None:
            tasks.append(("convert", convert_task, [], 6))
        for j in range(n_sub):
            tasks.append((("prep", j), prep_task(j), [("prep", j - 1)] if j else [], 4 * j))
            for cc in range(per_sub):
                c = j * per_sub + cc
                tasks.append((("rwkv", c), rwkv_task(c), [("prep", j)], 3 + 2 * c))
            for hk in range(ATTN_KV_HEADS):
                tasks.append((("attn", j, hk), attn_task(j, hk), [("prep", j)], 4 * j + 3 + hk))
        for c in range(n_chunk):
            tasks.append((("gla", c), gla_task(c), [("prep", c // per_sub)], 8 + 2 * c))
        for c in range(n_chunk):
            deps = [("rwkv", c), ("gla", c)] + ([("state", c - 1)] if c else [])
            tasks.append((("state", c), state_task(c), deps, 0))
        for j in range(n_sub):
            deps = [("state", (j + 1) * per_sub - 1)] + [("attn", j, hk) for hk in range(ATTN_KV_HEADS)]
            tasks.append((("finish", j), finish_task(j), deps, 0))
        for half in range(2):
            deps = [("finish", j) for j in range(half * n_sub // 2, (half + 1) * n_sub // 2)]
            tasks.append((("out", half), out_task(half), deps, 0))
        _run_tasks(tasks)

    run_pipeline()


def _ada_kernel(c_ref, w_ref, b_ref, o_ref):
    c_act = _silu(c_ref[...])
    o_ref[0] = jnp.sum(w_ref[0] * c_act, axis=0, keepdims=True) + b_ref[0]


def _adaln(c, ada_w, ada_b):
    depth = ada_w.shape[0]
    tn = D_MODEL
    return pl.pallas_call(
        _ada_kernel,
        grid=(depth, 3 * D_MODEL // tn),
        in_specs=[pl.BlockSpec((D_MODEL, 1), lambda l, n: (0, 0)),
                  pl.BlockSpec((1, D_MODEL, tn), lambda l, n: (l, 0, n)),
                  pl.BlockSpec((1, 1, tn), lambda l, n: (l, 0, n))],
        out_specs=pl.BlockSpec((1, 1, tn), lambda l, n: (l, 0, n)),
        out_shape=jax.ShapeDtypeStruct((depth, 1, 3 * D_MODEL), F32),
        compiler_params=pltpu.CompilerParams(dimension_semantics=("arbitrary", "arbitrary"),
                                             vmem_limit_bytes=VMEM_LIMIT_BYTES),
        name="adaln",
    )(c.reshape(D_MODEL, 1), ada_w, ada_b.reshape(depth, 1, 3 * D_MODEL))


def _gla_level_matrices():
    mg = np.zeros(((N_LEVELS + 1) * CH, CH), np.float32)
    for t in range(CH):
        mg[t, :t + 1] = 1.0
    for l in range(1, N_LEVELS + 1):
        s = 2 ** (l - 1)
        for t in range(CH):
            m = (t // (2 * s)) * 2 * s + s
            if t >= m:
                mg[l * CH + t, m:t + 1] = 1.0
            else:
                mg[l * CH + t, t + 1:m] = 1.0
    lvl = np.full((CH, CH), -1, np.int32)
    for t in range(CH):
        lvl[t, t] = 0
        for s_ in range(t):
            lvl[t, s_] = int(np.floor(np.log2(t ^ s_))) + 1
    return mg, np.tile(lvl, (1, GLA_HEADS))


def _constants():
    mg, lv = _gla_level_matrices()
    tri = np.tril(np.ones((CH, CH), np.float32))
    lt2 = np.kron(np.eye(SUB // CH, dtype=np.float32), tri)
    ones_bd = np.kron(np.eye(RWKV_WIDTH // HEAD_DIM, dtype=np.float32), np.ones((HEAD_DIM, HEAD_DIM), np.float32))
    twice = lambda m: np.concatenate([m, m], axis=1)
    return (jnp.asarray(twice(mg), MXU_DTYPE), jnp.asarray(lv), jnp.asarray(twice(lt2), MXU_DTYPE),
            jnp.asarray(ones_bd, MXU_DTYPE))


def _layer_call(layer, x2d, sinks, stacked, consts, next_f32):
    t = x2d.shape[0]
    n_blocks = t // TM

    def full(a):
        return pl.BlockSpec(a.shape, lambda i: (0,) * a.ndim)

    def of_layer(a):
        return pl.BlockSpec((None,) + a.shape[1:], lambda i: (layer,) + (0,) * (a.ndim - 1))

    row_spec = pl.BlockSpec((TM, D_MODEL), lambda i: (i, 0))
    next_spec = pl.BlockSpec((TM, D_MODEL), lambda i: (jnp.minimum(i + 1, n_blocks - 1), 0))
    operands = tuple(stacked) + tuple(consts)
    in_specs = ([pl.BlockSpec(memory_space=pltpu.SMEM), row_spec, next_spec]
                + [of_layer(a) if a.ndim == 3 else full(a) for a in stacked] + [full(a) for a in consts])
    assert len(in_specs) == N_LAYER_INPUTS
    out_specs, out_shape, body = row_spec, jax.ShapeDtypeStruct((t, D_MODEL), F32), _layer_kernel
    if next_f32 is not None:
        rows = D_MODEL // n_blocks
        assert rows * n_blocks == D_MODEL and rows % 16 == 0
        operands += tuple(next_f32)
        in_specs += [pl.BlockSpec((None, rows, w.shape[2]), lambda i: (layer + 1, i, 0)) for w in next_f32]
        out_specs = [row_spec] + [pl.BlockSpec((rows, w.shape[2]), lambda i: (i, 0)) for w in next_f32]
        out_shape = [out_shape] + [jax.ShapeDtypeStruct(w.shape[1:], MXU_DTYPE) for w in next_f32]
        body = _layer_kernel_convert
    return pl.pallas_call(
        body,
        grid=(n_blocks,),
        in_specs=in_specs,
        out_specs=out_specs,
        out_shape=out_shape,
        scratch_shapes=[
            pltpu.VMEM((2, TM, AUG_WIDTH), F32),
            pltpu.VMEM((2, TM, GLA_KEY_WIDTH), F32),
            pltpu.SMEM((2,), F32),
            pltpu.VMEM((TM, GLA_VAL_WIDTH), F32),
            pltpu.VMEM((GLA_VAL_WIDTH, GLA_KEY_WIDTH), F32),
            pltpu.VMEM((TM, D_MODEL), MXU_DTYPE),
            pltpu.VMEM((TM, D_MODEL), MXU_DTYPE),
            pltpu.VMEM((RWKV_WIDTH, RWKV_WIDTH), F32),
            pltpu.VMEM((GLA_VAL_WIDTH, GLA_KEY_WIDTH), F32),
            pltpu.VMEM((SUB, KV_WIDTH), F32),
            pltpu.VMEM((SUB, KV_WIDTH), F32),
            pltpu.VMEM((8, 3 * RWKV_WIDTH + 128), F32),
        ],
        compiler_params=pltpu.CompilerParams(dimension_semantics=("arbitrary",),
                                             vmem_limit_bytes=VMEM_LIMIT_BYTES),
        name="hybrid_layer",
    )(sinks, x2d, x2d, *operands)


def kernel(x, c, ada_w, ada_b, norm_pre, norm_post, w_in, w_out, attn_sinks, rwkv_mu_rkv, rwkv_mu_w, rwkv_mu_a, rwkv_w0, rwkv_w1, rwkv_w2, rwkv_a0, rwkv_a1, rwkv_a2, rwkv_k_k, rwkv_k_a, rwkv_r_k, rwkv_ln_w, rwkv_ln_b, gla_gk1, gla_gk2, gla_gk_b, gla_norm_w):
    batch, seq, _ = x.shape
    depth = w_in.shape[0]
    assert batch == 1 and seq % TM == 0

    mods = _adaln(c, ada_w, ada_b)

    mu_w = rwkv_mu_w[:, :, None]
    mu_a = rwkv_mu_a[:, :, None]
    x1 = jnp.concatenate([(1.0 - mu_w) * rwkv_w1, (1.0 - mu_a) * rwkv_a1], axis=-1)
    x2 = jnp.concatenate([mu_w * rwkv_w1, mu_a * rwkv_a1], axis=-1)
    x3 = jnp.pad(gla_gk1, ((0, 0), (0, 0), (0, 128 - GLA_GATE_LORA)))
    w_aux = jnp.concatenate([x1, x2, x3], axis=-1).astype(MXU_DTYPE)
    zeros = jnp.zeros((depth, LORA, RWKV_WIDTH), F32)
    w2cat = jnp.concatenate([jnp.concatenate([rwkv_w2, zeros], axis=-1),
                             jnp.concatenate([zeros, rwkv_a2], axis=-1)], axis=1)
    w2cat = w2cat.astype(MXU_DTYPE)
    gk2p = jnp.pad(gla_gk2, ((0, 0), (0, 128 - GLA_GATE_LORA), (0, 0))).astype(MXU_DTYPE)
    rv = jnp.stack([rwkv_w0, rwkv_a0, rwkv_k_k, rwkv_k_a, rwkv_r_k.reshape(depth, RWKV_WIDTH),
                    rwkv_ln_w, rwkv_ln_b, jnp.tile(gla_norm_w, (1, GLA_HEADS))], axis=1)
    consts = _constants()
    w_main, wout = w_in[0].astype(MXU_DTYPE), w_out[0].astype(MXU_DTYPE)
    xc = x[0]
    for l in range(depth):
        stacked = (mods, norm_pre[:, None], norm_post[:, None], w_main, w_aux, wout, w2cat, gk2p,
                   rwkv_mu_rkv[:, None], rv, gla_gk_b[:, None])
        if l + 1 < depth:
            xc, w_main, wout = _layer_call(l, xc, attn_sinks[l], stacked, consts, (w_in, w_out))
        else:
            xc = _layer_call(l, xc, attn_sinks[l], stacked, consts, None)
    return xc[None]
```

```python
import numpy as np
import jax
import jax.numpy as jnp
from jax import lax
from jax.experimental import pallas as pl
from jax.experimental.pallas import tpu as pltpu

F32 = jnp.float32
MXU_DTYPE = jnp.bfloat16

D_MODEL = 1024
HEAD_DIM = 64
ATTN_WIDTH = 512
ATTN_Q_HEADS = 8
ATTN_KV_HEADS = 2
KV_WIDTH = 128
WINDOW = 128
RWKV_WIDTH = 256
RWKV_LN_EPS = 64e-5
LORA = 64
GLA_HEADS = 4
GLA_KEY_WIDTH = 128
GLA_KEY_DIM = 32
GLA_VAL_WIDTH = 256
GLA_GATE_LORA = 16
GLA_GATE_NORMALIZER = 16.0
NORM_EPS = 1e-6
IN_WIDTH = 3072

O_AQ, O_AK, O_AV, O_AG = 0, 512, 640, 768
O_RKV, O_RG = 1280, 2048
O_GQ, O_GK, O_GV, O_GG = 2304, 2432, 2560, 2816
O_X1, O_X2, O_X3 = 3072, 3200, 3328
AUG_WIDTH = 3456

TM = 512
SUB = 128
CH = 64
N_LEVELS = 6
PROJ_TILE = 256
GLA_SAFE_LOG = 60.0

VMEM_LIMIT_BYTES = 60000 * 1024


def _mm(a, b):
    return jnp.dot(a.astype(MXU_DTYPE), b.astype(MXU_DTYPE), preferred_element_type=F32)


def _mm_nt(a, b):
    return lax.dot_general(a.astype(MXU_DTYPE), b.astype(MXU_DTYPE), (((1,), (1,)), ((), ())),
                           preferred_element_type=F32)


def _mm_tn(a, b):
    return lax.dot_general(a.astype(MXU_DTYPE), b.astype(MXU_DTYPE), (((0,), (0,)), ((), ())),
                           preferred_element_type=F32)


def _split(x, n):
    parts, r = [], x
    for _ in range(n):
        p = r.astype(MXU_DTYPE)
        parts.append(p)
        r = r - p.astype(F32)
    return parts


def _sel_mm2(sel2, x):
    hi, lo = _split(x, 2)
    return jnp.dot(sel2, jnp.concatenate([hi, lo], axis=0), preferred_element_type=F32)


def _mm_sel(x, sel, n):
    acc = None
    for p in _split(x, n):
        t = jnp.dot(p, sel, preferred_element_type=F32)
        acc = t if acc is None else acc + t
    return acc


def _iota(shape, dim):
    return lax.broadcasted_iota(jnp.int32, shape, dim)


def _sigmoid(z):
    return 1.0 / (1.0 + jnp.exp(-z))


def _silu(z):
    return z * _sigmoid(z)


def _softplus(z):
    return jnp.maximum(z, 0.0) + jnp.log(1.0 + jnp.exp(-jnp.abs(z)))


def _block_diag(x, row_shift, col_shift):
    t = jnp.concatenate([x, x, x, x], axis=0)
    keep = (_iota(t.shape, 0) >> row_shift) == (_iota(t.shape, 1) >> col_shift)
    return jnp.where(keep, t, 0.0)


def _bd64(x):
    return _block_diag(x, 6, 6)


def _run_tasks(tasks):
    results, running, pending, rnd = {}, {}, list(tasks), 0
    while pending or running:
        for task in list(pending):
            name, make, deps, earliest = task
            if earliest <= rnd and all(d in results for d in deps):
                running[name] = make(results)
                pending.remove(task)
        for name in list(running):
            try:
                next(running[name])
            except StopIteration as stop:
                results[name] = stop.value
                del running[name]
        rnd += 1
        assert rnd < 1000, "task graph cannot make progress"
    return results


def _attention_setup(k_cur, v_cur, k_prev, v_prev, key_offset):
    kb = jnp.concatenate([k_prev, k_cur], axis=0)
    vb = jnp.concatenate([v_prev, v_cur], axis=0)
    lo_kv = _iota(kb.shape, 1) < HEAD_DIM
    kr = pltpu.roll(kb, HEAD_DIM, axis=1)
    vr = pltpu.roll(vb, HEAD_DIM, axis=1)
    k_dup = (jnp.where(lo_kv, kb, kr), jnp.where(lo_kv, kr, kb))
    v_dup = (jnp.where(lo_kv, vb, vr), jnp.where(lo_kv, vr, vb))
    row = _iota((SUB, 2 * SUB), 0)
    col = _iota((SUB, 2 * SUB), 1)
    dist = row - col + SUB
    valid = (dist >= 0) & (dist < WINDOW) & (col >= key_offset)
    return k_dup, v_dup, valid, dist.astype(F32)


def _attention_group(hk, q, k_dup, v_dup, valid, distf, sinks_ref):
    lo_q = _iota((SUB, 128), 1) < HEAD_DIM
    group = ATTN_Q_HEADS // ATTN_KV_HEADS
    base = hk * group * HEAD_DIM
    slab_a = q[:, base:base + 128]
    slab_b = q[:, base + 128:base + 256]
    lhs = jnp.concatenate([jnp.where(lo_q, slab_a, 0.0), jnp.where(lo_q, 0.0, slab_a),
                           jnp.where(lo_q, slab_b, 0.0), jnp.where(lo_q, 0.0, slab_b)], axis=0)
    s_all = _mm_nt(lhs, k_dup[hk])
    yield
    ps, invs = [], []
    for g in range(group):
        hq = hk * group + g
        slope = 2.0 ** (-8.0 * (hq + 1) / ATTN_Q_HEADS)
        s = s_all[g * SUB:(g + 1) * SUB] * (HEAD_DIM ** -0.5) - slope * distf
        s = jnp.where(valid, s, -jnp.inf)
        sink = sinks_ref[hq]
        m = jnp.maximum(jnp.max(s, axis=-1, keepdims=True), sink)
        p = jnp.exp(s - m)
        den = jnp.sum(p, axis=-1, keepdims=True) + jnp.exp(sink - m)
        ps.append(p)
        invs.append(1.0 / den)
    o_all = _mm(jnp.concatenate(ps, axis=0), v_dup[hk])
    yield
    o = [o_all[g * SUB:(g + 1) * SUB] * invs[g] for g in range(group)]
    return jnp.concatenate([jnp.where(lo_q, o[0], o[1]), jnp.where(lo_q, o[2], o[3])], axis=1)


def _rwkv_chunk_intra(at, bt, kt, rt, v):
    row = _iota((CH, RWKV_WIDTH), 0)
    col = _iota((CH, RWKV_WIDTH), 1) & (HEAD_DIM - 1)
    strict = col < row
    incl = col <= row

    lhs = jnp.concatenate([at, rt], axis=0)
    g_b = _mm_nt(lhs, _bd64(bt))
    g_k = _mm_nt(lhs, _bd64(kt))
    yield
    a_ab = jnp.where(strict, g_b[:CH], 0.0)
    a_rb = jnp.where(incl, g_b[CH:], 0.0)
    a_ak = jnp.where(strict, g_k[:CH], 0.0)
    a_rk = jnp.where(incl, g_k[CH:], 0.0)

    p = _mm(a_ab, _bd64(a_ab))
    av2 = _mm(jnp.concatenate([a_ak, a_rk], axis=0), _bd64(v))
    akv, ark_v = av2[:CH], av2[CH:]
    w = jnp.where(col == row, 1.0, 0.0) + a_ab
    yield
    for _ in range(N_LEVELS - 2):
        r = _mm(jnp.concatenate([p, w], axis=0), _bd64(p))
        yield
        w = w + r[CH:]
        p = r[:CH]
    t_inv = w + _mm(w, _bd64(p))
    yield
    at2 = _mm(t_inv, _bd64(at))
    uv = _mm(t_inv, _bd64(akv))
    return a_rb, ark_v, at2, uv


def _rwkv_chunk_state(a_rb, ark_v, at2, uv, rt, v, bp, kp, gam, st):
    x = _mm_nt(jnp.concatenate([at2, rt], axis=0), st)
    yield
    u = uv + x[:CH]
    y = x[CH:] + ark_v + _mm(a_rb, _bd64(u))
    upd = _mm_tn(jnp.concatenate([u, v], axis=0), jnp.concatenate([bp, kp], axis=0))
    keep = (_iota(upd.shape, 0) >> 6) == (_iota(upd.shape, 1) >> 6)
    st_new = gam * st + jnp.where(keep, upd, 0.0)
    return y, st_new


def _gla_level_scores(ql, kl):
    keep_k = (_iota((4 * CH, GLA_KEY_WIDTH), 0) >> 6) == (_iota((4 * CH, GLA_KEY_WIDTH), 1) >> 5)
    kt = jnp.concatenate([kl, kl, kl, kl], axis=0)
    return _mm_nt(ql, jnp.where(keep_k, kt, 0.0))


def _gla_state_update(v, k, b, b_last):
    upd = _mm_tn(v, k * jnp.exp(b_last - b))
    keep_s = (_iota(upd.shape, 0) >> 6) == (_iota(upd.shape, 1) >> 5)
    return jnp.where(keep_s, upd, 0.0)


def _gla_chunk_intra(qs, k, v, g, mg_ref, lv):
    b = _sel_mm2(mg_ref[0:CH, :], g)
    yield
    b_last = b[CH - 1:CH]
    scores = _gla_level_scores(qs * jnp.exp(b), k * jnp.exp(-jnp.maximum(b, -GLA_SAFE_LOG)))
    upd = _gla_state_update(v, k, b, b_last)
    yield
    o_intra = _mm(jnp.where(lv >= 0, scores, 0.0), _bd64(v))
    return o_intra, qs * jnp.exp(b), jnp.exp(b_last), upd


def _gla_chunk_state(o_intra, q_dec, decay, upd, st):
    return o_intra + _mm_nt(q_dec, st), decay * st + upd


def _gla_chunk_general(qs, k, v, g, mg_ref, lv, st):
    e_all = _sel_mm2(mg_ref[...], g)
    b = e_all[:CH]
    b_last = b[CH - 1:CH]
    a = jnp.where(lv == 0, _gla_level_scores(qs, k), 0.0)
    for l in range(1, N_LEVELS + 1):
        f = jnp.exp(e_all[l * CH:(l + 1) * CH])
        a = jnp.where(lv == l, _gla_level_scores(qs * f, k * f), a)
    o = _mm(a, _bd64(v)) + _mm_nt(qs * jnp.exp(b), st)
    return o, jnp.exp(b_last) * st + _gla_state_update(v, k, b, b_last)


N_LAYER_INPUTS = 18


def _layer_kernel(*refs):
    _layer_body(*refs)


N_PREPARE_INPUTS = 5
N_PREPARE_OUTPUTS = 3


def _layer_kernel_prepare(*refs):
    a, b = N_LAYER_INPUTS, N_LAYER_INPUTS + N_PREPARE_INPUTS
    out_ref = refs[b]
    _layer_body(*refs[:a], out_ref, *refs[b + 1 + N_PREPARE_OUTPUTS:],
                prepare=(refs[a:b], refs[b + 1:b + 1 + N_PREPARE_OUTPUTS]))


def _layer_body(sinks_ref, x_ref, xn_ref, mod_ref, gpre_ref, gpost_ref, win_ref, waux_ref, wout_ref, w2_ref,
                gk2_ref, mu_ref, rv_ref, gkb_ref, mg_ref, lv_ref, lt2_ref, ones_ref,
                out_ref,
                pbuf_ref, gbuf_ref, floor_ref, galt_ref, sgalt_ref, hn_ref, mix_ref, srt_ref, sgt_ref,
                kprev_ref, vprev_ref, prev_ref, prepare=None):
    i = pl.program_id(0)
    n_sub = TM // SUB
    n_chunk = TM // CH
    per_sub = SUB // CH
    shift = mod_ref[:, 0:D_MODEL]
    scale = mod_ref[:, D_MODEL:2 * D_MODEL]
    gate = mod_ref[:, 2 * D_MODEL:3 * D_MODEL]
    g_mod = gpre_ref[...] * (1.0 + scale)

    def normed(x):
        ms = jnp.mean(x * x, axis=-1, keepdims=True)
        return ((x * lax.rsqrt(ms + NORM_EPS)) * g_mod + shift).astype(MXU_DTYPE)

    def log_gates(p_ref):
        graw = _mm(p_ref[:, O_X3:O_X3 + 128], gk2_ref[...]) + gkb_ref[...]
        g = -_softplus(-graw) * (1.0 / GLA_GATE_NORMALIZER)
        lowest = jnp.sum(g[0:CH], axis=0, keepdims=True)
        for c in range(1, n_chunk):
            lowest = jnp.minimum(lowest, jnp.sum(g[c * CH:(c + 1) * CH], axis=0, keepdims=True))
        return g, jnp.min(lowest)

    @pl.when(i == 0)
    def _first():
        srt_ref[...] = jnp.zeros(srt_ref.shape, F32)
        sgt_ref[...] = jnp.zeros(sgt_ref.shape, F32)
        kprev_ref[...] = jnp.zeros(kprev_ref.shape, F32)
        vprev_ref[...] = jnp.zeros(vprev_ref.shape, F32)
        prev_ref[...] = jnp.zeros(prev_ref.shape, F32)
        galt_ref[...] = jnp.zeros(galt_ref.shape, F32)
        sgalt_ref[...] = jnp.zeros(sgalt_ref.shape, F32)
        if prepare is not None:
            prepare[1][2][...] = prepare[0][4][...]
        h0 = normed(x_ref[...])
        pbuf_ref[0, :, 0:IN_WIDTH] = jnp.dot(h0, win_ref[...], preferred_element_type=F32)
        pbuf_ref[0, :, IN_WIDTH:AUG_WIDTH] = jnp.dot(h0, waux_ref[...], preferred_element_type=F32)
        g0, lowest0 = log_gates(pbuf_ref.at[0])
        gbuf_ref[0] = g0
        floor_ref[0] = lowest0

    slot = i % 2
    proj_ref = pbuf_ref.at[slot]
    next_ref = pbuf_ref.at[1 - slot]

    gates_bounded = floor_ref[slot] > -GLA_SAFE_LOG

    @pl.when(jnp.logical_not(gates_bounded))
    def _general_gla():
        sgalt_ref[...] = sgt_ref[...]

        def chunk(c, carry):
            rows = pl.ds(pl.multiple_of(c * CH, CH), CH)
            o, st_new = _gla_chunk_general(
                proj_ref[rows, O_GQ:O_GQ + GLA_KEY_WIDTH] * (GLA_KEY_DIM ** -0.5),
                proj_ref[rows, O_GK:O_GK + GLA_KEY_WIDTH], proj_ref[rows, O_GV:O_GV + GLA_VAL_WIDTH],
                gbuf_ref[slot, rows, :], mg_ref, lv_ref[...], sgalt_ref[...])
            galt_ref[rows, :] = o
            sgalt_ref[...] = st_new
            return carry

        lax.fori_loop(0, n_chunk, chunk, 0)

    def proj_task(res):
        hn_ref[...] = normed(xn_ref[...])
        yield
        for w_ref, base in ((waux_ref, IN_WIDTH), (win_ref, 0)):
            width = w_ref.shape[1]
            for c0 in range(0, width, PROJ_TILE):
                c1 = min(c0 + PROJ_TILE, width)
                next_ref[:, base + c0:base + c1] = jnp.dot(hn_ref[...], w_ref[:, c0:c1],
                                                           preferred_element_type=F32)
                yield
                yield
            if w_ref is waux_ref:
                g_next, lowest = log_gates(next_ref)
                gbuf_ref[1 - slot] = g_next
                floor_ref[1 - slot] = lowest

    w0, a0 = rv_ref[0:1, :], rv_ref[1:2, :]
    k_k, k_a, r_k = rv_ref[2:3, :], rv_ref[3:4, :], rv_ref[4:5, :]
    ln_w, ln_b, gnw = rv_ref[5:6, :], rv_ref[6:7, :], rv_ref[7:8, :]

    def prep_task(j):
        def gen(res):
            rows = pl.ds(j * SUB, SUB)
            if j == 0:
                prev_row, k_prev, v_prev = prev_ref[0:1, :], kprev_ref[...], vprev_ref[...]
            else:
                before = res[("prep", j - 1)]
                prev_row, k_prev, v_prev = before["last_row"], before["k_cur"], before["v_cur"]
            k_cur = proj_ref[rows, O_AK:O_AK + KV_WIDTH]
            v_cur = proj_ref[rows, O_AV:O_AV + KV_WIDTH]
            cat = jnp.concatenate([proj_ref[rows, O_RKV:O_RKV + 3 * RWKV_WIDTH],
                                   proj_ref[rows, O_X2:O_X2 + 128]], axis=1)
            rolled = pltpu.roll(cat, 1, axis=0)
            shifted = jnp.where(_iota(cat.shape, 0) == 0, prev_row, rolled)
            last_row = cat[SUB - 1:SUB, :]
            if j == n_sub - 1:
                prev_ref[0:1, :] = last_row
                kprev_ref[...] = k_cur
                vprev_ref[...] = v_cur
            rkv = cat[:, :3 * RWKV_WIDTH]
            rkv = rkv + (shifted[:, :3 * RWKV_WIDTH] - rkv) * mu_ref[...]
            r = rkv[:, 0:RWKV_WIDTH]
            k = rkv[:, RWKV_WIDTH:2 * RWKV_WIDTH]
            v = rkv[:, 2 * RWKV_WIDTH:3 * RWKV_WIDTH]
            lin = proj_ref[rows, O_X1:O_X1 + 128] + shifted[:, 3 * RWKV_WIDTH:]
            z = jnp.where(_iota(lin.shape, 1) < LORA, jnp.tanh(lin), lin)
            lora = _mm(z, w2_ref[...])
            kk = k * k_k
            kk_ss = _mm_sel(kk * kk, ones_ref[...], 1)
            yield
            logw = -_softplus(-(lora[:, :RWKV_WIDTH] + w0)) - 0.5
            ld = -jnp.exp(logw)
            iclr = _sigmoid(lora[:, RWKV_WIDTH:] + a0)
            kk = kk / jnp.maximum(jnp.sqrt(kk_ss), 1e-12)
            kmod = k * (1.0 + (iclr - 1.0) * k_a)
            av = -kk
            bv = kk * iclr
            c = _sel_mm2(lt2_ref[...], ld)
            key_offset = jnp.where(jnp.logical_and(i == 0, j == 0), SUB, 0)
            k_dup, v_dup, valid, distf = _attention_setup(k_cur, v_cur, k_prev, v_prev, key_offset)
            yield
            c_last = jnp.where(_iota(c.shape, 0) < CH, c[CH - 1:CH, :], c[2 * CH - 1:2 * CH, :])
            e_neg = jnp.exp(-c)
            e_rem = jnp.exp(c_last - c)
            return dict(
                last_row=last_row, k_cur=k_cur, v_cur=v_cur,
                k_dup=k_dup, v_dup=v_dup, valid=valid, distf=distf,
                r=r, v=v, kmod=kmod,
                at=av * jnp.exp(c - ld), bt=bv * e_neg, kt=kmod * e_neg, rt=r * jnp.exp(c),
                bp=bv * e_rem, kp=kmod * e_rem, gam=jnp.exp(c_last),
                g=gbuf_ref[slot, rows, :],
                qs=proj_ref[rows, O_GQ:O_GQ + GLA_KEY_WIDTH] * (GLA_KEY_DIM ** -0.5),
                gk=proj_ref[rows, O_GK:O_GK + GLA_KEY_WIDTH],
                gv=proj_ref[rows, O_GV:O_GV + GLA_VAL_WIDTH])
        return gen

    def attn_task(j, hk):
        def gen(res):
            p = res[("prep", j)]
            q_all = proj_ref[pl.ds(j * SUB, SUB), O_AQ:O_AQ + ATTN_WIDTH]
            return (yield from _attention_group(hk, q_all, p["k_dup"], p["v_dup"], p["valid"], p["distf"],
                                                sinks_ref))
        return gen

    def chunk_rows(c):
        cc = c % per_sub
        return slice(cc * CH, (cc + 1) * CH)

    def rwkv_task(c):
        def gen(res):
            p, sl = res[("prep", c // per_sub)], chunk_rows(c)
            return (yield from _rwkv_chunk_intra(p["at"][sl], p["bt"][sl], p["kt"][sl], p["rt"][sl], p["v"][sl]))
        return gen

    def gla_task(c):
        def gen(res):
            p, sl = res[("prep", c // per_sub)], chunk_rows(c)
            return (yield from _gla_chunk_intra(p["qs"][sl], p["gk"][sl], p["gv"][sl], p["g"][sl],
                                                mg_ref, lv_ref[...]))
        return gen

    def state_task(c):
        def gen(res):
            if c == 0:
                st, sg = srt_ref[...], sgt_ref[...]
            else:
                st, sg = res[("state", c - 1)][2:]
            p, sl = res[("prep", c // per_sub)], chunk_rows(c)
            o_c, sg_new = _gla_chunk_state(*res[("gla", c)], sg)
            a_rb, ark_v, at2, uv = res[("rwkv", c)]
            y_c, st_new = yield from _rwkv_chunk_state(a_rb, ark_v, at2, uv, p["rt"][sl], p["v"][sl], p["bp"][sl],
                                                       p["kp"][sl], p["gam"][sl.start:sl.start + 1], st)
            if c == n_chunk - 1:
                srt_ref[...] = st_new
                sgt_ref[...] = jnp.where(gates_bounded, sg_new, sgalt_ref[...])
            return y_c, o_c, st_new, sg_new
        return gen

    def finish_task(j):
        def gen(res):
            rows = pl.ds(j * SUB, SUB)
            p = res[("prep", j)]
            parts = [res[("state", j * per_sub + cc)] for cc in range(per_sub)]
            y = jnp.concatenate([q[0] for q in parts], axis=0)
            o = jnp.where(gates_bounded, jnp.concatenate([q[1] for q in parts], axis=0), galt_ref[rows, :])
            attn =jnp.concatenate([res[("attn", j, hk)] for hk in range(ATTN_KV_HEADS)], axis=1)
            mix_ref[rows, 0:ATTN_WIDTH] = (attn * _silu(proj_ref[rows, O_AG:O_AG + ATTN_WIDTH])).astype(mix_ref.dtype)
            ones_bd = ones_ref[...]
            sums = _mm_sel(jnp.concatenate([y, p["r"] * p["kmod"] * r_k, o * o], axis=0), ones_bd, 1)
            mean = sums[0:SUB] * (1.0 / HEAD_DIM)
            bonus = sums[SUB:2 * SUB] * p["v"]
            oms = sums[2 * SUB:3 * SUB] * (1.0 / HEAD_DIM)
            yield
            yc = y - mean
            var = _mm_sel(yc * yc, ones_bd, 1) * (1.0 / HEAD_DIM)
            gla = (o * lax.rsqrt(oms + 1e-5)) * gnw * _silu(proj_ref[rows, O_GG:O_GG + GLA_VAL_WIDTH])
            mix_ref[rows, ATTN_WIDTH + RWKV_WIDTH:D_MODEL] = gla.astype(mix_ref.dtype)
            yield
            yn = (yc * lax.rsqrt(var + RWKV_LN_EPS)) * ln_w + ln_b
            rwkv = (yn + bonus) * _silu(proj_ref[rows, O_RG:O_RG + RWKV_WIDTH])
            mix_ref[rows, ATTN_WIDTH:ATTN_WIDTH + RWKV_WIDTH] = rwkv.astype(mix_ref.dtype)
        return gen

    def out_task(half):
        def gen(res):
            rows = pl.ds(half * (TM // 2), TM // 2)
            y = jnp.dot(mix_ref[rows, :], wout_ref[...], preferred_element_type=F32)
            yield
            ms2 = jnp.mean(y * y, axis=-1, keepdims=True)
            out_ref[rows, :] = x_ref[rows, :] + gate * ((y * lax.rsqrt(ms2 + NORM_EPS)) * gpost_ref[...])
        return gen

    def prepare_task(res):
        (wfn_ref, won_ref, ccol_ref, adaw_ref, adab_ref), (wbn_ref, wobn_ref, modn_ref) = prepare
        wbn_ref[...] = wfn_ref[...].astype(wbn_ref.dtype)
        wobn_ref[...] = won_ref[...].astype(wobn_ref.dtype)
        rows = adaw_ref.shape[0]
        c_act = _silu(ccol_ref[pl.ds(pl.multiple_of(i * rows, rows), rows), :])
        part = jnp.sum(adaw_ref[...] * c_act, axis=0, keepdims=True)
        modn_ref[...] += part
        return None
        yield

    def run_pipeline():
        tasks = [("proj", proj_task, [], 0)]
        if prepare is not None:
            tasks.append(("prepare", prepare_task, [], 6))
        for j in range(n_sub):
            tasks.append((("prep", j), prep_task(j), [("prep", j - 1)] if j else [], 4 * j))
            for cc in range(per_sub):
                c = j * per_sub + cc
                tasks.append((("rwkv", c), rwkv_task(c), [("prep", j)], 3 + 2 * c))
            for hk in range(ATTN_KV_HEADS):
                tasks.append((("attn", j, hk), attn_task(j, hk), [("prep", j)], 4 * j + 3 + hk))
        for c in range(n_chunk):
            tasks.append((("gla", c), gla_task(c), [("prep", c // per_sub)], 8 + 2 * c))
        for c in range(n_chunk):
            deps = [("rwkv", c), ("gla", c)] + ([("state", c - 1)] if c else [])
            tasks.append((("state", c), state_task(c), deps, 0))
        for j in range(n_sub):
            deps = [("state", (j + 1) * per_sub - 1)] + [("attn", j, hk) for hk in range(ATTN_KV_HEADS)]
            tasks.append((("finish", j), finish_task(j), deps, 0))
        for half in range(2):
            deps = [("finish", j) for j in range(half * n_sub // 2, (half + 1) * n_sub // 2)]
            tasks.append((("out", half), out_task(half), deps, 0))
        _run_tasks(tasks)

    run_pipeline()


def _ada_kernel(c_ref, w_ref, b_ref, o_ref):
    c_act = _silu(c_ref[...])
    o_ref[0] = jnp.sum(w_ref[0] * c_act, axis=0, keepdims=True) + b_ref[0]


def _adaln(c_col, ada_w, ada_b, n_layers):
    depth = ada_w.shape[0]
    tn = D_MODEL
    return pl.pallas_call(
        _ada_kernel,
        grid=(n_layers, 3 * D_MODEL // tn),
        in_specs=[pl.BlockSpec((D_MODEL, 1), lambda l, n: (0, 0)),
                  pl.BlockSpec((1, D_MODEL, tn), lambda l, n: (l, 0, n)),
                  pl.BlockSpec((1, 1, tn), lambda l, n: (l, 0, n))],
        out_specs=pl.BlockSpec((1, 1, tn), lambda l, n: (l, 0, n)),
        out_shape=jax.ShapeDtypeStruct((n_layers, 1, 3 * D_MODEL), F32),
        compiler_params=pltpu.CompilerParams(dimension_semantics=("arbitrary", "arbitrary"),
                                             vmem_limit_bytes=VMEM_LIMIT_BYTES),
        name="adaln",
    )(c_col, ada_w, ada_b.reshape(depth, 1, 3 * D_MODEL))


def _gla_level_matrices():
    mg = np.zeros(((N_LEVELS + 1) * CH, CH), np.float32)
    for t in range(CH):
        mg[t, :t + 1] = 1.0
    for l in range(1, N_LEVELS + 1):
        s = 2 ** (l - 1)
        for t in range(CH):
            m = (t // (2 * s)) * 2 * s + s
            if t >= m:
                mg[l * CH + t, m:t + 1] = 1.0
            else:
                mg[l * CH + t, t + 1:m] = 1.0
    lvl = np.full((CH, CH), -1, np.int32)
    for t in range(CH):
        lvl[t, t] = 0
        for s_ in range(t):
            lvl[t, s_] = int(np.floor(np.log2(t ^ s_))) + 1
    return mg, np.tile(lvl, (1, GLA_HEADS))


def _constants():
    mg, lv = _gla_level_matrices()
    tri = np.tril(np.ones((CH, CH), np.float32))
    lt2 = np.kron(np.eye(SUB // CH, dtype=np.float32), tri)
    ones_bd = np.kron(np.eye(RWKV_WIDTH // HEAD_DIM, dtype=np.float32), np.ones((HEAD_DIM, HEAD_DIM), np.float32))
    twice = lambda m: np.concatenate([m, m], axis=1)
    return (jnp.asarray(twice(mg), MXU_DTYPE), jnp.asarray(lv), jnp.asarray(twice(lt2), MXU_DTYPE),
            jnp.asarray(ones_bd, MXU_DTYPE))


def _layer_call(layer, x2d, sinks, stacked, consts, nxt):
    t = x2d.shape[0]
    n_blocks = t // TM

    def full(a):
        return pl.BlockSpec(a.shape, lambda i: (0,) * a.ndim)

    def of_layer(a):
        return pl.BlockSpec((None,) + a.shape[1:], lambda i: (layer,) + (0,) * (a.ndim - 1))

    row_spec = pl.BlockSpec((TM, D_MODEL), lambda i: (i, 0))
    next_spec = pl.BlockSpec((TM, D_MODEL), lambda i: (jnp.minimum(i + 1, n_blocks - 1), 0))
    operands = tuple(stacked) + tuple(consts)
    in_specs = ([pl.BlockSpec(memory_space=pltpu.SMEM), row_spec, next_spec]
                + [of_layer(a) if a.ndim == 3 else full(a) for a in stacked] + [full(a) for a in consts])
    assert len(in_specs) == N_LAYER_INPUTS
    out_specs, out_shape, body = row_spec, jax.ShapeDtypeStruct((t, D_MODEL), F32), _layer_kernel
    if nxt is not None:
        w_in, w_out, c_col, ada_w, ada_b3 = nxt
        rows = D_MODEL // n_blocks
        assert rows * n_blocks == D_MODEL and rows % 16 == 0

        def next_rows(w):
            return pl.BlockSpec((None, rows, w.shape[2]), lambda i: (layer + 1, i, 0))

        operands += nxt
        in_specs += [next_rows(w_in), next_rows(w_out), full(c_col), next_rows(ada_w),
                     pl.BlockSpec((None,) + ada_b3.shape[1:], lambda i: (layer + 1, 0, 0))]
        assert len(in_specs) == N_LAYER_INPUTS + N_PREPARE_INPUTS
        out_specs = [row_spec, pl.BlockSpec((rows, w_in.shape[2]), lambda i: (i, 0)),
                     pl.BlockSpec((rows, w_out.shape[2]), lambda i: (i, 0)),
                     pl.BlockSpec(ada_b3.shape[1:], lambda i: (0, 0))]
        out_shape = [out_shape, jax.ShapeDtypeStruct(w_in.shape[1:], MXU_DTYPE),
                     jax.ShapeDtypeStruct(w_out.shape[1:], MXU_DTYPE), jax.ShapeDtypeStruct(ada_b3.shape[1:], F32)]
        body = _layer_kernel_prepare
    return pl.pallas_call(
        body,
        grid=(n_blocks,),
        in_specs=in_specs,
        out_specs=out_specs,
        out_shape=out_shape,
        scratch_shapes=[
            pltpu.VMEM((2, TM, AUG_WIDTH), F32),
            pltpu.VMEM((2, TM, GLA_KEY_WIDTH), F32),
            pltpu.SMEM((2,), F32),
            pltpu.VMEM((TM, GLA_VAL_WIDTH), F32),
            pltpu.VMEM((GLA_VAL_WIDTH, GLA_KEY_WIDTH), F32),
            pltpu.VMEM((TM, D_MODEL), MXU_DTYPE),
            pltpu.VMEM((TM, D_MODEL), MXU_DTYPE),
            pltpu.VMEM((RWKV_WIDTH, RWKV_WIDTH), F32),
            pltpu.VMEM((GLA_VAL_WIDTH, GLA_KEY_WIDTH), F32),
            pltpu.VMEM((SUB, KV_WIDTH), F32),
            pltpu.VMEM((SUB, KV_WIDTH), F32),
            pltpu.VMEM((8, 3 * RWKV_WIDTH + 128), F32),
        ],
        compiler_params=pltpu.CompilerParams(dimension_semantics=("arbitrary",),
                                             vmem_limit_bytes=VMEM_LIMIT_BYTES),
        name="hybrid_layer",
    )(sinks, x2d, x2d, *operands)


def kernel(x, c, ada_w, ada_b, norm_pre, norm_post, w_in, w_out, attn_sinks, rwkv_mu_rkv, rwkv_mu_w, rwkv_mu_a, rwkv_w0, rwkv_w1, rwkv_w2, rwkv_a0, rwkv_a1, rwkv_a2, rwkv_k_k, rwkv_k_a, rwkv_r_k, rwkv_ln_w, rwkv_ln_b, gla_gk1, gla_gk2, gla_gk_b, gla_norm_w):
    batch, seq, _ = x.shape
    depth = w_in.shape[0]
    assert batch == 1 and seq % TM == 0

    c_col = c.reshape(D_MODEL, 1)
    ada_b3 = ada_b.reshape(depth, 1, 3 * D_MODEL)
    mod = _adaln(c_col, ada_w, ada_b, 1)[0]

    mu_w = rwkv_mu_w[:, :, None]
    mu_a = rwkv_mu_a[:, :, None]
    x1 = jnp.concatenate([(1.0 - mu_w) * rwkv_w1, (1.0 - mu_a) * rwkv_a1], axis=-1)
    x2 = jnp.concatenate([mu_w * rwkv_w1, mu_a * rwkv_a1], axis=-1)
    x3 = jnp.pad(gla_gk1, ((0, 0), (0, 0), (0, 128 - GLA_GATE_LORA)))
    w_aux = jnp.concatenate([x1, x2, x3], axis=-1).astype(MXU_DTYPE)
    zeros = jnp.zeros((depth, LORA, RWKV_WIDTH), F32)
    w2cat = jnp.concatenate([jnp.concatenate([rwkv_w2, zeros], axis=-1),
                             jnp.concatenate([zeros, rwkv_a2], axis=-1)], axis=1)
    w2cat = w2cat.astype(MXU_DTYPE)
    gk2p = jnp.pad(gla_gk2, ((0, 0), (0, 128 - GLA_GATE_LORA), (0, 0))).astype(MXU_DTYPE)
    rv = jnp.stack([rwkv_w0, rwkv_a0, rwkv_k_k, rwkv_k_a, rwkv_r_k.reshape(depth, RWKV_WIDTH),
                    rwkv_ln_w, rwkv_ln_b, jnp.tile(gla_norm_w, (1, GLA_HEADS))], axis=1)
    consts = _constants()
    w_main, wout = w_in[0].astype(MXU_DTYPE), w_out[0].astype(MXU_DTYPE)
    xc = x[0]
    for l in range(depth):
        stacked = (mod, norm_pre[:, None], norm_post[:, None], w_main, w_aux, wout, w2cat, gk2p,
                   rwkv_mu_rkv[:, None], rv, gla_gk_b[:, None])
        if l + 1 < depth:
            xc, w_main, wout, mod = _layer_call(l, xc, attn_sinks[l], stacked, consts,
                                                (w_in, w_out, c_col, ada_w, ada_b3))
        else:
            xc = _layer_call(l, xc, attn_sinks[l], stacked, consts, None)
    return xc[None]
```

```python
import numpy as np
import jax
import jax.numpy as jnp
from jax import lax
from jax.experimental import pallas as pl
from jax.experimental.pallas import tpu as pltpu

F32 = jnp.float32
MXU_DTYPE = jnp.bfloat16

D_MODEL = 1024
HEAD_DIM = 64
ATTN_WIDTH = 512
ATTN_Q_HEADS = 8
ATTN_KV_HEADS = 2
KV_WIDTH = 128
WINDOW = 128
RWKV_WIDTH = 256
RWKV_LN_EPS = 64e-5
LORA = 64
GLA_HEADS = 4
GLA_KEY_WIDTH = 128
GLA_KEY_DIM = 32
GLA_VAL_WIDTH = 256
GLA_GATE_LORA = 16
GLA_GATE_NORMALIZER = 16.0
NORM_EPS = 1e-6
IN_WIDTH = 3072

O_AQ, O_AK, O_AV, O_AG = 0, 512, 640, 768
O_RKV, O_RG = 1280, 2048
O_GQ, O_GK, O_GV, O_GG = 2304, 2432, 2560, 2816
O_X1, O_X2, O_X3 = 3072, 3200, 3328
AUG_WIDTH = 3456

TM = 512
SUB = 128
CH = 64
N_LEVELS = 6
PROJ_TILE = 256
GLA_SAFE_LOG = 60.0

VMEM_LIMIT_BYTES = 60000 * 1024


def _mm(a, b):
    return jnp.dot(a.astype(MXU_DTYPE), b.astype(MXU_DTYPE), preferred_element_type=F32)


def _mm_nt(a, b):
    return lax.dot_general(a.astype(MXU_DTYPE), b.astype(MXU_DTYPE), (((1,), (1,)), ((), ())),
                           preferred_element_type=F32)


def _mm_tn(a, b):
    return lax.dot_general(a.astype(MXU_DTYPE), b.astype(MXU_DTYPE), (((0,), (0,)), ((), ())),
                           preferred_element_type=F32)


def _split(x, n):
    parts, r = [], x
    for _ in range(n):
        p = r.astype(MXU_DTYPE)
        parts.append(p)
        r = r - p.astype(F32)
    return parts


def _sel_mm2(sel2, x):
    hi, lo = _split(x, 2)
    return jnp.dot(sel2, jnp.concatenate([hi, lo], axis=0), preferred_element_type=F32)


def _mm_sel(x, sel, n):
    acc = None
    for p in _split(x, n):
        t = jnp.dot(p, sel, preferred_element_type=F32)
        acc = t if acc is None else acc + t
    return acc


def _iota(shape, dim):
    return lax.broadcasted_iota(jnp.int32, shape, dim)


def _sigmoid(z):
    return 1.0 / (1.0 + jnp.exp(-z))


def _silu(z):
    return z * _sigmoid(z)


def _softplus(z):
    return jnp.maximum(z, 0.0) + jnp.log(1.0 + jnp.exp(-jnp.abs(z)))


def _block_diag(x, row_shift, col_shift):
    t = jnp.concatenate([x, x, x, x], axis=0)
    keep = (_iota(t.shape, 0) >> row_shift) == (_iota(t.shape, 1) >> col_shift)
    return jnp.where(keep, t, 0.0)


def _bd64(x):
    return _block_diag(x, 6, 6)


def _run_tasks(tasks):
    results, running, pending, rnd = {}, {}, list(tasks), 0
    while pending or running:
        for task in list(pending):
            name, make, deps, earliest = task
            if earliest <= rnd and all(d in results for d in deps):
                running[name] = make(results)
                pending.remove(task)
        for name in list(running):
            try:
                next(running[name])
            except StopIteration as stop:
                results[name] = stop.value
                del running[name]
        rnd += 1
        assert rnd < 1000, "task graph cannot make progress"
    return results


def _attention_setup(k_cur, v_cur, k_prev, v_prev, key_offset):
    kb = jnp.concatenate([k_prev, k_cur], axis=0)
    vb = jnp.concatenate([v_prev, v_cur], axis=0)
    lo_kv = _iota(kb.shape, 1) < HEAD_DIM
    kr = pltpu.roll(kb, HEAD_DIM, axis=1)
    vr = pltpu.roll(vb, HEAD_DIM, axis=1)
    k_dup = (jnp.where(lo_kv, kb, kr), jnp.where(lo_kv, kr, kb))
    v_dup = (jnp.where(lo_kv, vb, vr), jnp.where(lo_kv, vr, vb))
    row = _iota((SUB, 2 * SUB), 0)
    col = _iota((SUB, 2 * SUB), 1)
    dist = row - col + SUB
    valid = (dist >= 0) & (dist < WINDOW) & (col >= key_offset)
    return k_dup, v_dup, valid, dist.astype(F32)


def _attention_group(hk, q, k_dup, v_dup, valid, distf, sinks_ref):
    lo_q = _iota((SUB, 128), 1) < HEAD_DIM
    group = ATTN_Q_HEADS // ATTN_KV_HEADS
    base = hk * group * HEAD_DIM
    slab_a = q[:, base:base + 128]
    slab_b = q[:, base + 128:base + 256]
    lhs = jnp.concatenate([jnp.where(lo_q, slab_a, 0.0), jnp.where(lo_q, 0.0, slab_a),
                           jnp.where(lo_q, slab_b, 0.0), jnp.where(lo_q, 0.0, slab_b)], axis=0)
    s_all = _mm_nt(lhs, k_dup[hk])
    yield
    ps, invs = [], []
    for g in range(group):
        hq = hk * group + g
        slope = 2.0 ** (-8.0 * (hq + 1) / ATTN_Q_HEADS)
        s = s_all[g * SUB:(g + 1) * SUB] * (HEAD_DIM ** -0.5) - slope * distf
        s = jnp.where(valid, s, -jnp.inf)
        sink = sinks_ref[hq]
        m = jnp.maximum(jnp.max(s, axis=-1, keepdims=True), sink)
        p = jnp.exp(s - m)
        den = jnp.sum(p, axis=-1, keepdims=True) + jnp.exp(sink - m)
        ps.append(p)
        invs.append(1.0 / den)
    o_all = _mm(jnp.concatenate(ps, axis=0), v_dup[hk])
    yield
    o = [o_all[g * SUB:(g + 1) * SUB] * invs[g] for g in range(group)]
    return jnp.concatenate([jnp.where(lo_q, o[0], o[1]), jnp.where(lo_q, o[2], o[3])], axis=1)


def _rwkv_chunk_intra(at, bt, kt, rt, v):
    row = _iota((CH, RWKV_WIDTH), 0)
    col = _iota((CH, RWKV_WIDTH), 1) & (HEAD_DIM - 1)
    strict = col < row
    incl = col <= row

    lhs = jnp.concatenate([at, rt], axis=0)
    g_b = _mm_nt(lhs, _bd64(bt))
    g_k = _mm_nt(lhs, _bd64(kt))
    yield
    a_ab = jnp.where(strict, g_b[:CH], 0.0)
    a_rb = jnp.where(incl, g_b[CH:], 0.0)
    a_ak = jnp.where(strict, g_k[:CH], 0.0)
    a_rk = jnp.where(incl, g_k[CH:], 0.0)

    p = _mm(a_ab, _bd64(a_ab))
    av2 = _mm(jnp.concatenate([a_ak, a_rk], axis=0), _bd64(v))
    akv, ark_v = av2[:CH], av2[CH:]
    w = jnp.where(col == row, 1.0, 0.0) + a_ab
    yield
    for _ in range(N_LEVELS - 2):
        r = _mm(jnp.concatenate([p, w], axis=0), _bd64(p))
        yield
        w = w + r[CH:]
        p = r[:CH]
    t_inv = w + _mm(w, _bd64(p))
    yield
    at2 = _mm(t_inv, _bd64(at))
    uv = _mm(t_inv, _bd64(akv))
    return a_rb, ark_v, at2, uv


def _rwkv_chunk_state(a_rb, ark_v, at2, uv, rt, v, bp, kp, gam, st):
    x = _mm_nt(jnp.concatenate([at2, rt], axis=0), st)
    yield
    u = uv + x[:CH]
    y = x[CH:] + ark_v + _mm(a_rb, _bd64(u))
    upd = _mm_tn(jnp.concatenate([u, v], axis=0), jnp.concatenate([bp, kp], axis=0))
    keep = (_iota(upd.shape, 0) >> 6) == (_iota(upd.shape, 1) >> 6)
    st_new = gam * st + jnp.where(keep, upd, 0.0)
    return y, st_new


def _gla_level_scores(ql, kl):
    keep_k = (_iota((4 * CH, GLA_KEY_WIDTH), 0) >> 6) == (_iota((4 * CH, GLA_KEY_WIDTH), 1) >> 5)
    kt = jnp.concatenate([kl, kl, kl, kl], axis=0)
    return _mm_nt(ql, jnp.where(keep_k, kt, 0.0))


def _gla_state_update(v, k, b, b_last):
    upd = _mm_tn(v, k * jnp.exp(b_last - b))
    keep_s = (_iota(upd.shape, 0) >> 6) == (_iota(upd.shape, 1) >> 5)
    return jnp.where(keep_s, upd, 0.0)


def _gla_chunk_intra(qs, k, v, g, mg_ref, lv):
    b = _sel_mm2(mg_ref[0:CH, :], g)
    yield
    b_last = b[CH - 1:CH]
    scores = _gla_level_scores(qs * jnp.exp(b), k * jnp.exp(-jnp.maximum(b, -GLA_SAFE_LOG)))
    upd = _gla_state_update(v, k, b, b_last)
    yield
    o_intra = _mm(jnp.where(lv >= 0, scores, 0.0), _bd64(v))
    return o_intra, qs * jnp.exp(b), jnp.exp(b_last), upd


def _gla_chunk_state(o_intra, q_dec, decay, upd, st):
    return o_intra + _mm_nt(q_dec, st), decay * st + upd


def _gla_chunk_general(qs, k, v, g, mg_ref, lv, st):
    e_all = _sel_mm2(mg_ref[...], g)
    b = e_all[:CH]
    b_last = b[CH - 1:CH]
    a = jnp.where(lv == 0, _gla_level_scores(qs, k), 0.0)
    for l in range(1, N_LEVELS + 1):
        f = jnp.exp(e_all[l * CH:(l + 1) * CH])
        a = jnp.where(lv == l, _gla_level_scores(qs * f, k * f), a)
    o = _mm(a, _bd64(v)) + _mm_nt(qs * jnp.exp(b), st)
    return o, jnp.exp(b_last) * st + _gla_state_update(v, k, b, b_last)


N_LAYER_INPUTS = 18


def _layer_kernel(*refs):
    _layer_body(*refs)


N_PREPARE_INPUTS = 5
N_PREPARE_OUTPUTS = 3


def _layer_kernel_prepare(*refs):
    a, b = N_LAYER_INPUTS, N_LAYER_INPUTS + N_PREPARE_INPUTS
    out_ref = refs[b]
    _layer_body(*refs[:a], out_ref, *refs[b + 1 + N_PREPARE_OUTPUTS:],
                prepare=(refs[a:b], refs[b + 1:b + 1 + N_PREPARE_OUTPUTS]))


def _layer_body(sinks_ref, x_ref, xn_ref, mod_ref, gpre_ref, gpost_ref, win_ref, waux_ref, wout_ref, w2_ref,
                gk2_ref, mu_ref, rv_ref, gkb_ref, mg_ref, lv_ref, lt2_ref, ones_ref,
                out_ref,
                pbuf_ref, gbuf_ref, floor_ref, galt_ref, sgalt_ref, hn_ref, mix_ref, srt_ref, sgt_ref,
                kprev_ref, vprev_ref, prev_ref, prepare=None):
    i = pl.program_id(0)
    n_sub = TM // SUB
    n_chunk = TM // CH
    per_sub = SUB // CH
    shift = mod_ref[:, 0:D_MODEL]
    scale = mod_ref[:, D_MODEL:2 * D_MODEL]
    gate = mod_ref[:, 2 * D_MODEL:3 * D_MODEL]
    g_mod = gpre_ref[...] * (1.0 + scale)

    def normed(x):
        ms = jnp.mean(x * x, axis=-1, keepdims=True)
        return ((x * lax.rsqrt(ms + NORM_EPS)) * g_mod + shift).astype(MXU_DTYPE)

    def log_gates(p_ref):
        graw = _mm(p_ref[:, O_X3:O_X3 + 128], gk2_ref[...]) + gkb_ref[...]
        g = -_softplus(-graw) * (1.0 / GLA_GATE_NORMALIZER)
        lowest = jnp.sum(g[0:CH], axis=0, keepdims=True)
        for c in range(1, n_chunk):
            lowest = jnp.minimum(lowest, jnp.sum(g[c * CH:(c + 1) * CH], axis=0, keepdims=True))
        return g, jnp.min(lowest)

    @pl.when(i == 0)
    def _first():
        srt_ref[...] = jnp.zeros(srt_ref.shape, F32)
        sgt_ref[...] = jnp.zeros(sgt_ref.shape, F32)
        kprev_ref[...] = jnp.zeros(kprev_ref.shape, F32)
        vprev_ref[...] = jnp.zeros(vprev_ref.shape, F32)
        prev_ref[...] = jnp.zeros(prev_ref.shape, F32)
        galt_ref[...] = jnp.zeros(galt_ref.shape, F32)
        sgalt_ref[...] = jnp.zeros(sgalt_ref.shape, F32)
        if prepare is not None:
            prepare[1][2][...] = prepare[0][4][...]
        h0 = normed(x_ref[...])
        pbuf_ref[0, :, 0:IN_WIDTH] = jnp.dot(h0, win_ref[...], preferred_element_type=F32)
        pbuf_ref[0, :, IN_WIDTH:AUG_WIDTH] = jnp.dot(h0, waux_ref[...], preferred_element_type=F32)
        g0, lowest0 = log_gates(pbuf_ref.at[0])
        gbuf_ref[0] = g0
        floor_ref[0] = lowest0

    slot = i % 2
    proj_ref = pbuf_ref.at[slot]
    next_ref = pbuf_ref.at[1 - slot]

    gates_bounded = floor_ref[slot] > -GLA_SAFE_LOG

    @pl.when(jnp.logical_not(gates_bounded))
    def _general_gla():
        sgalt_ref[...] = sgt_ref[...]

        def chunk(c, carry):
            rows = pl.ds(pl.multiple_of(c * CH, CH), CH)
            o, st_new = _gla_chunk_general(
                proj_ref[rows, O_GQ:O_GQ + GLA_KEY_WIDTH] * (GLA_KEY_DIM ** -0.5),
                proj_ref[rows, O_GK:O_GK + GLA_KEY_WIDTH], proj_ref[rows, O_GV:O_GV + GLA_VAL_WIDTH],
                gbuf_ref[slot, rows, :], mg_ref, lv_ref[...], sgalt_ref[...])
            galt_ref[rows, :] = o
            sgalt_ref[...] = st_new
            return carry

        lax.fori_loop(0, n_chunk, chunk, 0)

    def proj_task(res):
        hn_ref[...] = normed(xn_ref[...])
        yield
        for w_ref, base in ((waux_ref, IN_WIDTH), (win_ref, 0)):
            width = w_ref.shape[1]
            for c0 in range(0, width, PROJ_TILE):
                c1 = min(c0 + PROJ_TILE, width)
                next_ref[:, base + c0:base + c1] = jnp.dot(hn_ref[...], w_ref[:, c0:c1],
                                                           preferred_element_type=F32)
                yield
                yield
            if w_ref is waux_ref:
                g_next, lowest = log_gates(next_ref)
                gbuf_ref[1 - slot] = g_next
                floor_ref[1 - slot] = lowest

    w0, a0 = rv_ref[0:1, :], rv_ref[1:2, :]
    k_k, k_a, r_k = rv_ref[2:3, :], rv_ref[3:4, :], rv_ref[4:5, :]
    ln_w, ln_b, gnw = rv_ref[5:6, :], rv_ref[6:7, :], rv_ref[7:8, :]

    def prep_task(j):
        def gen(res):
            rows = pl.ds(j * SUB, SUB)
            if j == 0:
                prev_row, k_prev, v_prev = prev_ref[0:1, :], kprev_ref[...], vprev_ref[...]
            else:
                before = res[("prep", j - 1)]
                prev_row, k_prev, v_prev = before["last_row"], before["k_cur"], before["v_cur"]
            k_cur = proj_ref[rows, O_AK:O_AK + KV_WIDTH]
            v_cur = proj_ref[rows, O_AV:O_AV + KV_WIDTH]
            cat = jnp.concatenate([proj_ref[rows, O_RKV:O_RKV + 3 * RWKV_WIDTH],
                                   proj_ref[rows, O_X2:O_X2 + 128]], axis=1)
            rolled = pltpu.roll(cat, 1, axis=0)
            shifted = jnp.where(_iota(cat.shape, 0) == 0, prev_row, rolled)
            last_row = cat[SUB - 1:SUB, :]
            if j == n_sub - 1:
                prev_ref[0:1, :] = last_row
                kprev_ref[...] = k_cur
                vprev_ref[...] = v_cur
            rkv = cat[:, :3 * RWKV_WIDTH]
            rkv = rkv + (shifted[:, :3 * RWKV_WIDTH] - rkv) * mu_ref[...]
            r = rkv[:, 0:RWKV_WIDTH]
            k = rkv[:, RWKV_WIDTH:2 * RWKV_WIDTH]
            v = rkv[:, 2 * RWKV_WIDTH:3 * RWKV_WIDTH]
            lin = proj_ref[rows, O_X1:O_X1 + 128] + shifted[:, 3 * RWKV_WIDTH:]
            z = jnp.where(_iota(lin.shape, 1) < LORA, jnp.tanh(lin), lin)
            lora = _mm(z, w2_ref[...])
            kk = k * k_k
            kk_ss = _mm_sel(kk * kk, ones_ref[...], 1)
            yield
            logw = -_softplus(-(lora[:, :RWKV_WIDTH] + w0)) - 0.5
            ld = -jnp.exp(logw)
            iclr = _sigmoid(lora[:, RWKV_WIDTH:] + a0)
            kk = kk / jnp.maximum(jnp.sqrt(kk_ss), 1e-12)
            kmod = k * (1.0 + (iclr - 1.0) * k_a)
            av = -kk
            bv = kk * iclr
            c = _sel_mm2(lt2_ref[...], ld)
            key_offset = jnp.where(jnp.logical_and(i == 0, j == 0), SUB, 0)
            k_dup, v_dup, valid, distf = _attention_setup(k_cur, v_cur, k_prev, v_prev, key_offset)
            yield
            c_last = jnp.where(_iota(c.shape, 0) < CH, c[CH - 1:CH, :], c[2 * CH - 1:2 * CH, :])
            e_neg = jnp.exp(-c)
            e_rem = jnp.exp(c_last - c)
            return dict(
                last_row=last_row, k_cur=k_cur, v_cur=v_cur,
                k_dup=k_dup, v_dup=v_dup, valid=valid, distf=distf,
                r=r, v=v, kmod=kmod,
                at=av * jnp.exp(c - ld), bt=bv * e_neg, kt=kmod * e_neg, rt=r * jnp.exp(c),
                bp=bv * e_rem, kp=kmod * e_rem, gam=jnp.exp(c_last),
                g=gbuf_ref[slot, rows, :],
                qs=proj_ref[rows, O_GQ:O_GQ + GLA_KEY_WIDTH] * (GLA_KEY_DIM ** -0.5),
                gk=proj_ref[rows, O_GK:O_GK + GLA_KEY_WIDTH],
                gv=proj_ref[rows, O_GV:O_GV + GLA_VAL_WIDTH])
        return gen

    def attn_task(j, hk):
        def gen(res):
            p = res[("prep", j)]
            q_all = proj_ref[pl.ds(j * SUB, SUB), O_AQ:O_AQ + ATTN_WIDTH]
            return (yield from _attention_group(hk, q_all, p["k_dup"], p["v_dup"], p["valid"], p["distf"],
                                                sinks_ref))
        return gen

    def chunk_rows(c):
        cc = c % per_sub
        return slice(cc * CH, (cc + 1) * CH)

    def rwkv_task(c):
        def gen(res):
            p, sl = res[("prep", c // per_sub)], chunk_rows(c)
            return (yield from _rwkv_chunk_intra(p["at"][sl], p["bt"][sl], p["kt"][sl], p["rt"][sl], p["v"][sl]))
        return gen

    def gla_task(c):
        def gen(res):
            p, sl = res[("prep", c // per_sub)], chunk_rows(c)
            return (yield from _gla_chunk_intra(p["qs"][sl], p["gk"][sl], p["gv"][sl], p["g"][sl],
                                                mg_ref, lv_ref[...]))
        return gen

    def state_task(c):
        def gen(res):
            if c == 0:
                st, sg = srt_ref[...], sgt_ref[...]
            else:
                st, sg = res[("state", c - 1)][2:]
            p, sl = res[("prep", c // per_sub)], chunk_rows(c)
            o_c, sg_new = _gla_chunk_state(*res[("gla", c)], sg)
            a_rb, ark_v, at2, uv = res[("rwkv", c)]
            y_c, st_new = yield from _rwkv_chunk_state(a_rb, ark_v, at2, uv, p["rt"][sl], p["v"][sl], p["bp"][sl],
                                                       p["kp"][sl], p["gam"][sl.start:sl.start + 1], st)
            if c == n_chunk - 1:
                srt_ref[...] = st_new
                sgt_ref[...] = jnp.where(gates_bounded, sg_new, sgalt_ref[...])
            return y_c, o_c, st_new, sg_new
        return gen

    def finish_task(j):
        def gen(res):
            rows = pl.ds(j * SUB, SUB)
            p = res[("prep", j)]
            parts = [res[("state", j * per_sub + cc)] for cc in range(per_sub)]
            y = jnp.concatenate([q[0] for q in parts], axis=0)
            o = jnp.where(gates_bounded, jnp.concatenate([q[1] for q in parts], axis=0), galt_ref[rows, :])
            attn =jnp.concatenate([res[("attn", j, hk)] for hk in range(ATTN_KV_HEADS)], axis=1)
            mix_ref[rows, 0:ATTN_WIDTH] = (attn * _silu(proj_ref[rows, O_AG:O_AG + ATTN_WIDTH])).astype(mix_ref.dtype)
            ones_bd = ones_ref[...]
            sums = _mm_sel(jnp.concatenate([y, p["r"] * p["kmod"] * r_k, o * o], axis=0), ones_bd, 1)
            mean = sums[0:SUB] * (1.0 / HEAD_DIM)
            bonus = sums[SUB:2 * SUB] * p["v"]
            oms = sums[2 * SUB:3 * SUB] * (1.0 / HEAD_DIM)
            yield
            yc = y - mean
            var = _mm_sel(yc * yc, ones_bd, 1) * (1.0 / HEAD_DIM)
            gla = (o * lax.rsqrt(oms + 1e-5)) * gnw * _silu(proj_ref[rows, O_GG:O_GG + GLA_VAL_WIDTH])
            mix_ref[rows, ATTN_WIDTH + RWKV_WIDTH:D_MODEL] = gla.astype(mix_ref.dtype)
            yield
            yn = (yc * lax.rsqrt(var + RWKV_LN_EPS)) * ln_w + ln_b
            rwkv = (yn + bonus) * _silu(proj_ref[rows, O_RG:O_RG + RWKV_WIDTH])
            mix_ref[rows, ATTN_WIDTH:ATTN_WIDTH + RWKV_WIDTH] = rwkv.astype(mix_ref.dtype)
        return gen

    def out_task(half):
        def gen(res):
            rows = pl.ds(half * (TM // 2), TM // 2)
            y = jnp.dot(mix_ref[rows, :], wout_ref[...], preferred_element_type=F32)
            yield
            ms2 = jnp.mean(y * y, axis=-1, keepdims=True)
            out_ref[rows, :] = x_ref[rows, :] + gate * ((y * lax.rsqrt(ms2 + NORM_EPS)) * gpost_ref[...])
        return gen

    def prepare_task(res):
        (wfn_ref, won_ref, ccol_ref, adaw_ref, adab_ref), (wbn_ref, wobn_ref, modn_ref) = prepare
        wbn_ref[...] = wfn_ref[...].astype(wbn_ref.dtype)
        wobn_ref[...] = won_ref[...].astype(wobn_ref.dtype)
        rows = adaw_ref.shape[0]
        c_act = _silu(ccol_ref[pl.ds(pl.multiple_of(i * rows, rows), rows), :])
        part = jnp.sum(adaw_ref[...] * c_act, axis=0, keepdims=True)
        modn_ref[...] += part
        return None
        yield

    def run_pipeline():
        tasks = [("proj", proj_task, [], 0)]
        if prepare is not None:
            tasks.append(("prepare", prepare_task, [], 6))
        for j in range(n_sub):
            tasks.append((("prep", j), prep_task(j), [("prep", j - 1)] if j else [], 4 * j))
            for cc in range(per_sub):
                c = j * per_sub + cc
                tasks.append((("rwkv", c), rwkv_task(c), [("prep", j)], 3 + 2 * c))
            for hk in range(ATTN_KV_HEADS):
                tasks.append((("attn", j, hk), attn_task(j, hk), [("prep", j)], 4 * j + 3 + hk))
        for c in range(n_chunk):
            tasks.append((("gla", c), gla_task(c), [("prep", c // per_sub)], 9 + 2 * c))
        for c in range(n_chunk):
            deps = [("rwkv", c), ("gla", c)] + ([("state", c - 1)] if c else [])
            tasks.append((("state", c), state_task(c), deps, 0))
        for j in range(n_sub):
            deps = [("state", (j + 1) * per_sub - 1)] + [("attn", j, hk) for hk in range(ATTN_KV_HEADS)]
            tasks.append((("finish", j), finish_task(j), deps, 0))
        for half in range(2):
            deps = [("finish", j) for j in range(half * n_sub // 2, (half + 1) * n_sub // 2)]
            tasks.append((("out", half), out_task(half), deps, 0))
        _run_tasks(tasks)

    run_pipeline()


def _ada_kernel(c_ref, w_ref, b_ref, o_ref):
    c_act = _silu(c_ref[...])
    o_ref[0] = jnp.sum(w_ref[0] * c_act, axis=0, keepdims=True) + b_ref[0]


def _adaln(c_col, ada_w, ada_b, n_layers):
    depth = ada_w.shape[0]
    tn = D_MODEL
    return pl.pallas_call(
        _ada_kernel,
        grid=(n_layers, 3 * D_MODEL // tn),
        in_specs=[pl.BlockSpec((D_MODEL, 1), lambda l, n: (0, 0)),
                  pl.BlockSpec((1, D_MODEL, tn), lambda l, n: (l, 0, n)),
                  pl.BlockSpec((1, 1, tn), lambda l, n: (l, 0, n))],
        out_specs=pl.BlockSpec((1, 1, tn), lambda l, n: (l, 0, n)),
        out_shape=jax.ShapeDtypeStruct((n_layers, 1, 3 * D_MODEL), F32),
        compiler_params=pltpu.CompilerParams(dimension_semantics=("arbitrary", "arbitrary"),
                                             vmem_limit_bytes=VMEM_LIMIT_BYTES),
        name="adaln",
    )(c_col, ada_w, ada_b.reshape(depth, 1, 3 * D_MODEL))


def _gla_level_matrices():
    mg = np.zeros(((N_LEVELS + 1) * CH, CH), np.float32)
    for t in range(CH):
        mg[t, :t + 1] = 1.0
    for l in range(1, N_LEVELS + 1):
        s = 2 ** (l - 1)
        for t in range(CH):
            m = (t // (2 * s)) * 2 * s + s
            if t >= m:
                mg[l * CH + t, m:t + 1] = 1.0
            else:
                mg[l * CH + t, t + 1:m] = 1.0
    lvl = np.full((CH, CH), -1, np.int32)
    for t in range(CH):
        lvl[t, t] = 0
        for s_ in range(t):
            lvl[t, s_] = int(np.floor(np.log2(t ^ s_))) + 1
    return mg, np.tile(lvl, (1, GLA_HEADS))


def _constants():
    mg, lv = _gla_level_matrices()
    tri = np.tril(np.ones((CH, CH), np.float32))
    lt2 = np.kron(np.eye(SUB // CH, dtype=np.float32), tri)
    ones_bd = np.kron(np.eye(RWKV_WIDTH // HEAD_DIM, dtype=np.float32), np.ones((HEAD_DIM, HEAD_DIM), np.float32))
    twice = lambda m: np.concatenate([m, m], axis=1)
    return (jnp.asarray(twice(mg), MXU_DTYPE), jnp.asarray(lv), jnp.asarray(twice(lt2), MXU_DTYPE),
            jnp.asarray(ones_bd, MXU_DTYPE))


def _layer_call(layer, x2d, sinks, stacked, consts, nxt):
    t = x2d.shape[0]
    n_blocks = t // TM

    def full(a):
        return pl.BlockSpec(a.shape, lambda i: (0,) * a.ndim)

    def of_layer(a):
        return pl.BlockSpec((None,) + a.shape[1:], lambda i: (layer,) + (0,) * (a.ndim - 1))

    row_spec = pl.BlockSpec((TM, D_MODEL), lambda i: (i, 0))
    next_spec = pl.BlockSpec((TM, D_MODEL), lambda i: (jnp.minimum(i + 1, n_blocks - 1), 0))
    operands = tuple(stacked) + tuple(consts)
    in_specs = ([pl.BlockSpec(memory_space=pltpu.SMEM), row_spec, next_spec]
                + [of_layer(a) if a.ndim == 3 else full(a) for a in stacked] + [full(a) for a in consts])
    assert len(in_specs) == N_LAYER_INPUTS
    out_specs, out_shape, body = row_spec, jax.ShapeDtypeStruct((t, D_MODEL), F32), _layer_kernel
    if nxt is not None:
        w_in, w_out, c_col, ada_w, ada_b3 = nxt
        rows = D_MODEL // n_blocks
        assert rows * n_blocks == D_MODEL and rows % 16 == 0

        def next_rows(w):
            return pl.BlockSpec((None, rows, w.shape[2]), lambda i: (layer + 1, i, 0))

        operands += nxt
        in_specs += [next_rows(w_in), next_rows(w_out), full(c_col), next_rows(ada_w),
                     pl.BlockSpec((None,) + ada_b3.shape[1:], lambda i: (layer + 1, 0, 0))]
        assert len(in_specs) == N_LAYER_INPUTS + N_PREPARE_INPUTS
        out_specs = [row_spec, pl.BlockSpec((rows, w_in.shape[2]), lambda i: (i, 0)),
                     pl.BlockSpec((rows, w_out.shape[2]), lambda i: (i, 0)),
                     pl.BlockSpec(ada_b3.shape[1:], lambda i: (0, 0))]
        out_shape = [out_shape, jax.ShapeDtypeStruct(w_in.shape[1:], MXU_DTYPE),
                     jax.ShapeDtypeStruct(w_out.shape[1:], MXU_DTYPE), jax.ShapeDtypeStruct(ada_b3.shape[1:], F32)]
        body = _layer_kernel_prepare
    return pl.pallas_call(
        body,
        grid=(n_blocks,),
        in_specs=in_specs,
        out_specs=out_specs,
        out_shape=out_shape,
        scratch_shapes=[
            pltpu.VMEM((2, TM, AUG_WIDTH), F32),
            pltpu.VMEM((2, TM, GLA_KEY_WIDTH), F32),
            pltpu.SMEM((2,), F32),
            pltpu.VMEM((TM, GLA_VAL_WIDTH), F32),
            pltpu.VMEM((GLA_VAL_WIDTH, GLA_KEY_WIDTH), F32),
            pltpu.VMEM((TM, D_MODEL), MXU_DTYPE),
            pltpu.VMEM((TM, D_MODEL), MXU_DTYPE),
            pltpu.VMEM((RWKV_WIDTH, RWKV_WIDTH), F32),
            pltpu.VMEM((GLA_VAL_WIDTH, GLA_KEY_WIDTH), F32),
            pltpu.VMEM((SUB, KV_WIDTH), F32),
            pltpu.VMEM((SUB, KV_WIDTH), F32),
            pltpu.VMEM((8, 3 * RWKV_WIDTH + 128), F32),
        ],
        compiler_params=pltpu.CompilerParams(dimension_semantics=("arbitrary",),
                                             vmem_limit_bytes=VMEM_LIMIT_BYTES),
        name="hybrid_layer",
    )(sinks, x2d, x2d, *operands)


def kernel(x, c, ada_w, ada_b, norm_pre, norm_post, w_in, w_out, attn_sinks, rwkv_mu_rkv, rwkv_mu_w, rwkv_mu_a, rwkv_w0, rwkv_w1, rwkv_w2, rwkv_a0, rwkv_a1, rwkv_a2, rwkv_k_k, rwkv_k_a, rwkv_r_k, rwkv_ln_w, rwkv_ln_b, gla_gk1, gla_gk2, gla_gk_b, gla_norm_w):
    batch, seq, _ = x.shape
    depth = w_in.shape[0]
    assert batch == 1 and seq % TM == 0

    c_col = c.reshape(D_MODEL, 1)
    ada_b3 = ada_b.reshape(depth, 1, 3 * D_MODEL)
    mod = _adaln(c_col, ada_w, ada_b, 1)[0]

    mu_w = rwkv_mu_w[:, :, None]
    mu_a = rwkv_mu_a[:, :, None]
    x1 = jnp.concatenate([(1.0 - mu_w) * rwkv_w1, (1.0 - mu_a) * rwkv_a1], axis=-1)
    x2 = jnp.concatenate([mu_w * rwkv_w1, mu_a * rwkv_a1], axis=-1)
    x3 = jnp.pad(gla_gk1, ((0, 0), (0, 0), (0, 128 - GLA_GATE_LORA)))
    w_aux = jnp.concatenate([x1, x2, x3], axis=-1).astype(MXU_DTYPE)
    zeros = jnp.zeros((depth, LORA, RWKV_WIDTH), F32)
    w2cat = jnp.concatenate([jnp.concatenate([rwkv_w2, zeros], axis=-1),
                             jnp.concatenate([zeros, rwkv_a2], axis=-1)], axis=1)
    w2cat = w2cat.astype(MXU_DTYPE)
    gk2p = jnp.pad(gla_gk2, ((0, 0), (0, 128 - GLA_GATE_LORA), (0, 0))).astype(MXU_DTYPE)
    rv = jnp.stack([rwkv_w0, rwkv_a0, rwkv_k_k, rwkv_k_a, rwkv_r_k.reshape(depth, RWKV_WIDTH),
                    rwkv_ln_w, rwkv_ln_b, jnp.tile(gla_norm_w, (1, GLA_HEADS))], axis=1)
    consts = _constants()
    w_main, wout = w_in[0].astype(MXU_DTYPE), w_out[0].astype(MXU_DTYPE)
    xc = x[0]
    for l in range(depth):
        stacked = (mod, norm_pre[:, None], norm_post[:, None], w_main, w_aux, wout, w2cat, gk2p,
                   rwkv_mu_rkv[:, None], rv, gla_gk_b[:, None])
        if l + 1 < depth:
            xc, w_main, wout, mod = _layer_call(l, xc, attn_sinks[l], stacked, consts,
                                                (w_in, w_out, c_col, ada_w, ada_b3))
        else:
            xc = _layer_call(l, xc, attn_sinks[l], stacked, consts, None)
    return xc[None]
```

```python
import numpy as np
import jax
import jax.numpy as jnp
from jax import lax
from jax.experimental import pallas as pl
from jax.experimental.pallas import tpu as pltpu

F32 = jnp.float32
MXU_DTYPE = jnp.bfloat16

D_MODEL = 1024
HEAD_DIM = 64
ATTN_WIDTH = 512
ATTN_Q_HEADS = 8
ATTN_KV_HEADS = 2
KV_WIDTH = 128
WINDOW = 128
RWKV_WIDTH = 256
RWKV_LN_EPS = 64e-5
LORA = 64
GLA_HEADS = 4
GLA_KEY_WIDTH = 128
GLA_KEY_DIM = 32
GLA_VAL_WIDTH = 256
GLA_GATE_LORA = 16
GLA_GATE_NORMALIZER = 16.0
NORM_EPS = 1e-6
IN_WIDTH = 3072

O_AQ, O_AK, O_AV, O_AG = 0, 512, 640, 768
O_RKV, O_RG = 1280, 2048
O_GQ, O_GK, O_GV, O_GG = 2304, 2432, 2560, 2816
O_X1, O_X2, O_X3 = 3072, 3200, 3328
AUG_WIDTH = 3456

TM = 512
SUB = 128
CH = 64
N_LEVELS = 6
PROJ_TILE = 256
GLA_SAFE_LOG = 60.0

VMEM_LIMIT_BYTES = 60000 * 1024


def _mm(a, b):
    return jnp.dot(a.astype(MXU_DTYPE), b.astype(MXU_DTYPE), preferred_element_type=F32)


def _mm_nt(a, b):
    return lax.dot_general(a.astype(MXU_DTYPE), b.astype(MXU_DTYPE), (((1,), (1,)), ((), ())),
                           preferred_element_type=F32)


def _mm_tn(a, b):
    return lax.dot_general(a.astype(MXU_DTYPE), b.astype(MXU_DTYPE), (((0,), (0,)), ((), ())),
                           preferred_element_type=F32)


def _split(x, n):
    parts, r = [], x
    for _ in range(n):
        p = r.astype(MXU_DTYPE)
        parts.append(p)
        r = r - p.astype(F32)
    return parts


def _sel_mm2(sel2, x):
    hi, lo = _split(x, 2)
    return jnp.dot(sel2, jnp.concatenate([hi, lo], axis=0), preferred_element_type=F32)


def _mm_sel(x, sel, n):
    acc = None
    for p in _split(x, n):
        t = jnp.dot(p, sel, preferred_element_type=F32)
        acc = t if acc is None else acc + t
    return acc


def _iota(shape, dim):
    return lax.broadcasted_iota(jnp.int32, shape, dim)


def _sigmoid(z):
    return 1.0 / (1.0 + jnp.exp(-z))


def _silu(z):
    return z * _sigmoid(z)


def _softplus(z):
    return jnp.maximum(z, 0.0) + jnp.log(1.0 + jnp.exp(-jnp.abs(z)))


def _block_diag(x, row_shift, col_shift):
    t = jnp.concatenate([x, x, x, x], axis=0)
    keep = (_iota(t.shape, 0) >> row_shift) == (_iota(t.shape, 1) >> col_shift)
    return jnp.where(keep, t, 0.0)


def _bd64(x):
    return _block_diag(x, 6, 6)


def _run_tasks(tasks):
    results, running, pending, rnd = {}, {}, list(tasks), 0
    while pending or running:
        for task in list(pending):
            name, make, deps, earliest = task
            if earliest <= rnd and all(d in results for d in deps):
                running[name] = make(results)
                pending.remove(task)
        for name in list(running):
            try:
                next(running[name])
            except StopIteration as stop:
                results[name] = stop.value
                del running[name]
        rnd += 1
        assert rnd < 1000, "task graph cannot make progress"
    return results


def _attention_setup(k_cur, v_cur, k_prev, v_prev, key_offset):
    kb = jnp.concatenate([k_prev, k_cur], axis=0)
    vb = jnp.concatenate([v_prev, v_cur], axis=0)
    lo_kv = _iota(kb.shape, 1) < HEAD_DIM
    kr = pltpu.roll(kb, HEAD_DIM, axis=1)
    vr = pltpu.roll(vb, HEAD_DIM, axis=1)
    k_dup = (jnp.where(lo_kv, kb, kr), jnp.where(lo_kv, kr, kb))
    v_dup = (jnp.where(lo_kv, vb, vr), jnp.where(lo_kv, vr, vb))
    row = _iota((SUB, 2 * SUB), 0)
    col = _iota((SUB, 2 * SUB), 1)
    dist = row - col + SUB
    valid = (dist >= 0) & (dist < WINDOW) & (col >= key_offset)
    return k_dup, v_dup, valid, dist.astype(F32)


def _attention_group(hk, q, k_dup, v_dup, valid, distf, sinks_ref):
    lo_q = _iota((SUB, 128), 1) < HEAD_DIM
    group = ATTN_Q_HEADS // ATTN_KV_HEADS
    base = hk * group * HEAD_DIM
    slab_a = q[:, base:base + 128]
    slab_b = q[:, base + 128:base + 256]
    lhs = jnp.concatenate([jnp.where(lo_q, slab_a, 0.0), jnp.where(lo_q, 0.0, slab_a),
                           jnp.where(lo_q, slab_b, 0.0), jnp.where(lo_q, 0.0, slab_b)], axis=0)
    s_all = _mm_nt(lhs, k_dup[hk])
    yield
    ps, invs = [], []
    for g in range(group):
        hq = hk * group + g
        slope = 2.0 ** (-8.0 * (hq + 1) / ATTN_Q_HEADS)
        s = s_all[g * SUB:(g + 1) * SUB] * (HEAD_DIM ** -0.5) - slope * distf
        s = jnp.where(valid, s, -jnp.inf)
        sink = sinks_ref[hq]
        m = jnp.maximum(jnp.max(s, axis=-1, keepdims=True), sink)
        p = jnp.exp(s - m)
        den = jnp.sum(p, axis=-1, keepdims=True) + jnp.exp(sink - m)
        ps.append(p)
        invs.append(1.0 / den)
    o_all = _mm(jnp.concatenate(ps, axis=0), v_dup[hk])
    yield
    o = [o_all[g * SUB:(g + 1) * SUB] * invs[g] for g in range(group)]
    return jnp.concatenate([jnp.where(lo_q, o[0], o[1]), jnp.where(lo_q, o[2], o[3])], axis=1)


def _rwkv_chunk_intra(at, bt, kt, rt, v):
    row = _iota((CH, RWKV_WIDTH), 0)
    col = _iota((CH, RWKV_WIDTH), 1) & (HEAD_DIM - 1)
    strict = col < row
    incl = col <= row

    lhs = jnp.concatenate([at, rt], axis=0)
    g_b = _mm_nt(lhs, _bd64(bt))
    g_k = _mm_nt(lhs, _bd64(kt))
    yield
    a_ab = jnp.where(strict, g_b[:CH], 0.0)
    a_rb = jnp.where(incl, g_b[CH:], 0.0)
    a_ak = jnp.where(strict, g_k[:CH], 0.0)
    a_rk = jnp.where(incl, g_k[CH:], 0.0)

    p = _mm(a_ab, _bd64(a_ab))
    av2 = _mm(jnp.concatenate([a_ak, a_rk], axis=0), _bd64(v))
    akv, ark_v = av2[:CH], av2[CH:]
    w = jnp.where(col == row, 1.0, 0.0) + a_ab
    yield
    for _ in range(N_LEVELS - 2):
        r = _mm(jnp.concatenate([p, w], axis=0), _bd64(p))
        yield
        w = w + r[CH:]
        p = r[:CH]
    t_inv = w + _mm(w, _bd64(p))
    yield
    at2 = _mm(t_inv, _bd64(at))
    uv = _mm(t_inv, _bd64(akv))
    return a_rb, ark_v, at2, uv


def _rwkv_chunk_state(a_rb, ark_v, at2, uv, rt, v, bp, kp, gam, st):
    x = _mm_nt(jnp.concatenate([at2, rt], axis=0), st)
    yield
    u = uv + x[:CH]
    y = x[CH:] + ark_v + _mm(a_rb, _bd64(u))
    upd = _mm_tn(jnp.concatenate([u, v], axis=0), jnp.concatenate([bp, kp], axis=0))
    keep = (_iota(upd.shape, 0) >> 6) == (_iota(upd.shape, 1) >> 6)
    st_new = gam * st + jnp.where(keep, upd, 0.0)
    return y, st_new


def _gla_level_scores(ql, kl):
    keep_k = (_iota((4 * CH, GLA_KEY_WIDTH), 0) >> 6) == (_iota((4 * CH, GLA_KEY_WIDTH), 1) >> 5)
    kt = jnp.concatenate([kl, kl, kl, kl], axis=0)
    return _mm_nt(ql, jnp.where(keep_k, kt, 0.0))


def _gla_state_update(v, k, b, b_last):
    upd = _mm_tn(v, k * jnp.exp(b_last - b))
    keep_s = (_iota(upd.shape, 0) >> 6) == (_iota(upd.shape, 1) >> 5)
    return jnp.where(keep_s, upd, 0.0)


def _gla_chunk_intra(qs, k, v, g, mg_ref, lv):
    b = _sel_mm2(mg_ref[0:CH, :], g)
    yield
    b_last = b[CH - 1:CH]
    scores = _gla_level_scores(qs * jnp.exp(b), k * jnp.exp(-jnp.maximum(b, -GLA_SAFE_LOG)))
    upd = _gla_state_update(v, k, b, b_last)
    yield
    o_intra = _mm(jnp.where(lv >= 0, scores, 0.0), _bd64(v))
    return o_intra, qs * jnp.exp(b), jnp.exp(b_last), upd


def _gla_chunk_state(o_intra, q_dec, decay, upd, st):
    return o_intra + _mm_nt(q_dec, st), decay * st + upd


def _gla_chunk_general(qs, k, v, g, mg_ref, lv, st):
    e_all = _sel_mm2(mg_ref[...], g)
    b = e_all[:CH]
    b_last = b[CH - 1:CH]
    a = jnp.where(lv == 0, _gla_level_scores(qs, k), 0.0)
    for l in range(1, N_LEVELS + 1):
        f = jnp.exp(e_all[l * CH:(l + 1) * CH])
        a = jnp.where(lv == l, _gla_level_scores(qs * f, k * f), a)
    o = _mm(a, _bd64(v)) + _mm_nt(qs * jnp.exp(b), st)
    return o, jnp.exp(b_last) * st + _gla_state_update(v, k, b, b_last)


N_LAYER_INPUTS = 18


def _layer_kernel(*refs):
    _layer_body(*refs)


N_PREPARE_INPUTS = 5
N_PREPARE_OUTPUTS = 3


def _layer_kernel_prepare(*refs):
    a, b = N_LAYER_INPUTS, N_LAYER_INPUTS + N_PREPARE_INPUTS
    out_ref = refs[b]
    _layer_body(*refs[:a], out_ref, *refs[b + 1 + N_PREPARE_OUTPUTS:],
                prepare=(refs[a:b], refs[b + 1:b + 1 + N_PREPARE_OUTPUTS]))


def _layer_body(sinks_ref, x_ref, xn_ref, mod_ref, gpre_ref, gpost_ref, win_ref, waux_ref, wout_ref, w2_ref,
                gk2_ref, mu_ref, rv_ref, gkb_ref, mg_ref, lv_ref, lt2_ref, ones_ref,
                out_ref,
                pbuf_ref, gbuf_ref, floor_ref, galt_ref, sgalt_ref, hn_ref, mix_ref, srt_ref, sgt_ref,
                kprev_ref, vprev_ref, prev_ref, prepare=None):
    i = pl.program_id(0)
    n_sub = TM // SUB
    n_chunk = TM // CH
    per_sub = SUB // CH
    shift = mod_ref[:, 0:D_MODEL]
    scale = mod_ref[:, D_MODEL:2 * D_MODEL]
    gate = mod_ref[:, 2 * D_MODEL:3 * D_MODEL]
    g_mod = gpre_ref[...] * (1.0 + scale)

    def normed(x):
        ms = jnp.mean(x * x, axis=-1, keepdims=True)
        return ((x * lax.rsqrt(ms + NORM_EPS)) * g_mod + shift).astype(MXU_DTYPE)

    def log_gates(p_ref):
        graw = _mm(p_ref[:, O_X3:O_X3 + 128], gk2_ref[...]) + gkb_ref[...]
        g = -_softplus(-graw) * (1.0 / GLA_GATE_NORMALIZER)
        lowest = jnp.sum(g[0:CH], axis=0, keepdims=True)
        for c in range(1, n_chunk):
            lowest = jnp.minimum(lowest, jnp.sum(g[c * CH:(c + 1) * CH], axis=0, keepdims=True))
        return g, jnp.min(lowest)

    @pl.when(i == 0)
    def _first():
        srt_ref[...] = jnp.zeros(srt_ref.shape, F32)
        sgt_ref[...] = jnp.zeros(sgt_ref.shape, F32)
        kprev_ref[...] = jnp.zeros(kprev_ref.shape, F32)
        vprev_ref[...] = jnp.zeros(vprev_ref.shape, F32)
        prev_ref[...] = jnp.zeros(prev_ref.shape, F32)
        galt_ref[...] = jnp.zeros(galt_ref.shape, F32)
        sgalt_ref[...] = jnp.zeros(sgalt_ref.shape, F32)
        if prepare is not None:
            prepare[1][2][...] = prepare[0][4][...]
        h0 = normed(x_ref[...])
        pbuf_ref[0, :, 0:IN_WIDTH] = jnp.dot(h0, win_ref[...], preferred_element_type=F32)
        pbuf_ref[0, :, IN_WIDTH:AUG_WIDTH] = jnp.dot(h0, waux_ref[...], preferred_element_type=F32)
        g0, lowest0 = log_gates(pbuf_ref.at[0])
        gbuf_ref[0] = g0
        floor_ref[0] = lowest0

    slot = i % 2
    proj_ref = pbuf_ref.at[slot]
    next_ref = pbuf_ref.at[1 - slot]

    gates_bounded = floor_ref[slot] > -GLA_SAFE_LOG

    @pl.when(jnp.logical_not(gates_bounded))
    def _general_gla():
        sgalt_ref[...] = sgt_ref[...]

        def chunk(c, carry):
            rows = pl.ds(pl.multiple_of(c * CH, CH), CH)
            o, st_new = _gla_chunk_general(
                proj_ref[rows, O_GQ:O_GQ + GLA_KEY_WIDTH] * (GLA_KEY_DIM ** -0.5),
                proj_ref[rows, O_GK:O_GK + GLA_KEY_WIDTH], proj_ref[rows, O_GV:O_GV + GLA_VAL_WIDTH],
                gbuf_ref[slot, rows, :], mg_ref, lv_ref[...], sgalt_ref[...])
            galt_ref[rows, :] = o
            sgalt_ref[...] = st_new
            return carry

        lax.fori_loop(0, n_chunk, chunk, 0)

    def proj_task(res):
        hn_ref[...] = normed(xn_ref[...])
        yield
        for w_ref, base in ((waux_ref, IN_WIDTH), (win_ref, 0)):
            width = w_ref.shape[1]
            for c0 in range(0, width, PROJ_TILE):
                c1 = min(c0 + PROJ_TILE, width)
                next_ref[:, base + c0:base + c1] = jnp.dot(hn_ref[...], w_ref[:, c0:c1],
                                                           preferred_element_type=F32)
                yield
                yield
            if w_ref is waux_ref:
                g_next, lowest = log_gates(next_ref)
                gbuf_ref[1 - slot] = g_next
                floor_ref[1 - slot] = lowest

    w0, a0 = rv_ref[0:1, :], rv_ref[1:2, :]
    k_k, k_a, r_k = rv_ref[2:3, :], rv_ref[3:4, :], rv_ref[4:5, :]
    ln_w, ln_b, gnw = rv_ref[5:6, :], rv_ref[6:7, :], rv_ref[7:8, :]

    def prep_task(j):
        def gen(res):
            rows = pl.ds(j * SUB, SUB)
            if j == 0:
                prev_row, k_prev, v_prev = prev_ref[0:1, :], kprev_ref[...], vprev_ref[...]
            else:
                before = res[("prep", j - 1)]
                prev_row, k_prev, v_prev = before["last_row"], before["k_cur"], before["v_cur"]
            k_cur = proj_ref[rows, O_AK:O_AK + KV_WIDTH]
            v_cur = proj_ref[rows, O_AV:O_AV + KV_WIDTH]
            cat = jnp.concatenate([proj_ref[rows, O_RKV:O_RKV + 3 * RWKV_WIDTH],
                                   proj_ref[rows, O_X2:O_X2 + 128]], axis=1)
            rolled = pltpu.roll(cat, 1, axis=0)
            shifted = jnp.where(_iota(cat.shape, 0) == 0, prev_row, rolled)
            last_row = cat[SUB - 1:SUB, :]
            if j == n_sub - 1:
                prev_ref[0:1, :] = last_row
                kprev_ref[...] = k_cur
                vprev_ref[...] = v_cur
            rkv = cat[:, :3 * RWKV_WIDTH]
            rkv = rkv + (shifted[:, :3 * RWKV_WIDTH] - rkv) * mu_ref[...]
            r = rkv[:, 0:RWKV_WIDTH]
            k = rkv[:, RWKV_WIDTH:2 * RWKV_WIDTH]
            v = rkv[:, 2 * RWKV_WIDTH:3 * RWKV_WIDTH]
            lin = proj_ref[rows, O_X1:O_X1 + 128] + shifted[:, 3 * RWKV_WIDTH:]
            z = jnp.where(_iota(lin.shape, 1) < LORA, jnp.tanh(lin), lin)
            lora = _mm(z, w2_ref[...])
            kk = k * k_k
            kk_ss = _mm_sel(kk * kk, ones_ref[...], 1)
            yield
            logw = -_softplus(-(lora[:, :RWKV_WIDTH] + w0)) - 0.5
            ld = -jnp.exp(logw)
            iclr = _sigmoid(lora[:, RWKV_WIDTH:] + a0)
            kk = kk / jnp.maximum(jnp.sqrt(kk_ss), 1e-12)
            kmod = k * (1.0 + (iclr - 1.0) * k_a)
            av = -kk
            bv = kk * iclr
            c = _sel_mm2(lt2_ref[...], ld)
            key_offset = jnp.where(jnp.logical_and(i == 0, j == 0), SUB, 0)
            k_dup, v_dup, valid, distf = _attention_setup(k_cur, v_cur, k_prev, v_prev, key_offset)
            yield
            c_last = jnp.where(_iota(c.shape, 0) < CH, c[CH - 1:CH, :], c[2 * CH - 1:2 * CH, :])
            e_neg = jnp.exp(-c)
            e_rem = jnp.exp(c_last - c)
            return dict(
                last_row=last_row, k_cur=k_cur, v_cur=v_cur,
                k_dup=k_dup, v_dup=v_dup, valid=valid, distf=distf,
                r=r, v=v, kmod=kmod,
                at=av * jnp.exp(c - ld), bt=bv * e_neg, kt=kmod * e_neg, rt=r * jnp.exp(c),
                bp=bv * e_rem, kp=kmod * e_rem, gam=jnp.exp(c_last),
                g=gbuf_ref[slot, rows, :],
                qs=proj_ref[rows, O_GQ:O_GQ + GLA_KEY_WIDTH] * (GLA_KEY_DIM ** -0.5),
                gk=proj_ref[rows, O_GK:O_GK + GLA_KEY_WIDTH],
                gv=proj_ref[rows, O_GV:O_GV + GLA_VAL_WIDTH])
        return gen

    def attn_task(j, hk):
        def gen(res):
            p = res[("prep", j)]
            q_all = proj_ref[pl.ds(j * SUB, SUB), O_AQ:O_AQ + ATTN_WIDTH]
            return (yield from _attention_group(hk, q_all, p["k_dup"], p["v_dup"], p["valid"], p["distf"],
                                                sinks_ref))
        return gen

    def chunk_rows(c):
        cc = c % per_sub
        return slice(cc * CH, (cc + 1) * CH)

    def rwkv_task(c):
        def gen(res):
            p, sl = res[("prep", c // per_sub)], chunk_rows(c)
            return (yield from _rwkv_chunk_intra(p["at"][sl], p["bt"][sl], p["kt"][sl], p["rt"][sl], p["v"][sl]))
        return gen

    def gla_task(c):
        def gen(res):
            p, sl = res[("prep", c // per_sub)], chunk_rows(c)
            return (yield from _gla_chunk_intra(p["qs"][sl], p["gk"][sl], p["gv"][sl], p["g"][sl],
                                                mg_ref, lv_ref[...]))
        return gen

    def state_task(c):
        def gen(res):
            if c == 0:
                st, sg = srt_ref[...], sgt_ref[...]
            else:
                st, sg = res[("state", c - 1)][2:]
            p, sl = res[("prep", c // per_sub)], chunk_rows(c)
            o_c, sg_new = _gla_chunk_state(*res[("gla", c)], sg)
            a_rb, ark_v, at2, uv = res[("rwkv", c)]
            y_c, st_new = yield from _rwkv_chunk_state(a_rb, ark_v, at2, uv, p["rt"][sl], p["v"][sl], p["bp"][sl],
                                                       p["kp"][sl], p["gam"][sl.start:sl.start + 1], st)
            if c == n_chunk - 1:
                srt_ref[...] = st_new
                sgt_ref[...] = jnp.where(gates_bounded, sg_new, sgalt_ref[...])
            return y_c, o_c, st_new, sg_new
        return gen

    def finish_task(j):
        def gen(res):
            rows = pl.ds(j * SUB, SUB)
            p = res[("prep", j)]
            parts = [res[("state", j * per_sub + cc)] for cc in range(per_sub)]
            y = jnp.concatenate([q[0] for q in parts], axis=0)
            o = jnp.where(gates_bounded, jnp.concatenate([q[1] for q in parts], axis=0), galt_ref[rows, :])
            attn =jnp.concatenate([res[("attn", j, hk)] for hk in range(ATTN_KV_HEADS)], axis=1)
            mix_ref[rows, 0:ATTN_WIDTH] = (attn * _silu(proj_ref[rows, O_AG:O_AG + ATTN_WIDTH])).astype(mix_ref.dtype)
            ones_bd = ones_ref[...]
            sums = _mm_sel(jnp.concatenate([y, p["r"] * p["kmod"] * r_k, o * o], axis=0), ones_bd, 1)
            mean = sums[0:SUB] * (1.0 / HEAD_DIM)
            bonus = sums[SUB:2 * SUB] * p["v"]
            oms = sums[2 * SUB:3 * SUB] * (1.0 / HEAD_DIM)
            yield
            yc = y - mean
            var = _mm_sel(yc * yc, ones_bd, 1) * (1.0 / HEAD_DIM)
            gla = (o * lax.rsqrt(oms + 1e-5)) * gnw * _silu(proj_ref[rows, O_GG:O_GG + GLA_VAL_WIDTH])
            mix_ref[rows, ATTN_WIDTH + RWKV_WIDTH:D_MODEL] = gla.astype(mix_ref.dtype)
            yield
            yn = (yc * lax.rsqrt(var + RWKV_LN_EPS)) * ln_w + ln_b
            rwkv = (yn + bonus) * _silu(proj_ref[rows, O_RG:O_RG + RWKV_WIDTH])
            mix_ref[rows, ATTN_WIDTH:ATTN_WIDTH + RWKV_WIDTH] = rwkv.astype(mix_ref.dtype)
        return gen

    def out_task(half):
        def gen(res):
            rows = pl.ds(half * (TM // 2), TM // 2)
            y = jnp.dot(mix_ref[rows, :], wout_ref[...], preferred_element_type=F32)
            yield
            ms2 = jnp.mean(y * y, axis=-1, keepdims=True)
            out_ref[rows, :] = x_ref[rows, :] + gate * ((y * lax.rsqrt(ms2 + NORM_EPS)) * gpost_ref[...])
        return gen

    def prepare_task(res):
        (wfn_ref, won_ref, ccol_ref, adaw_ref, adab_ref), (wbn_ref, wobn_ref, modn_ref) = prepare
        wbn_ref[...] = wfn_ref[...].astype(wbn_ref.dtype)
        wobn_ref[...] = won_ref[...].astype(wobn_ref.dtype)
        rows = adaw_ref.shape[0]
        c_act = _silu(ccol_ref[pl.ds(pl.multiple_of(i * rows, rows), rows), :])
        part = jnp.sum(adaw_ref[...] * c_act, axis=0, keepdims=True)
        modn_ref[...] += part
        return None
        yield

    def run_pipeline():
        tasks = [("proj", proj_task, [], 0)]
        if prepare is not None:
            tasks.append(("prepare", prepare_task, [], 6))
        for j in range(n_sub):
            tasks.append((("prep", j), prep_task(j), [("prep", j - 1)] if j else [], 4 * j))
            for cc in range(per_sub):
                c = j * per_sub + cc
                tasks.append((("rwkv", c), rwkv_task(c), [("prep", j)], 3 + 2 * c))
            for hk in range(ATTN_KV_HEADS):
                tasks.append((("attn", j, hk), attn_task(j, hk), [("prep", j)], 4 * j + 3 + hk))
        for c in range(n_chunk):
            tasks.append((("gla", c), gla_task(c), [("prep", c // per_sub)], 8 + 2 * c))
        for c in range(n_chunk):
            deps = [("rwkv", c), ("gla", c)] + ([("state", c - 1)] if c else [])
            tasks.append((("state", c), state_task(c), deps, 0))
        for j in range(n_sub):
            deps = [("state", (j + 1) * per_sub - 1)] + [("attn", j, hk) for hk in range(ATTN_KV_HEADS)]
            tasks.append((("finish", j), finish_task(j), deps, 0))
        for half in range(2):
            deps = [("finish", j) for j in range(half * n_sub // 2, (half + 1) * n_sub // 2)]
            tasks.append((("out", half), out_task(half), deps, 0))
        _run_tasks(tasks)

    run_pipeline()


def _first_prepare_kernel(wfn_ref, won_ref, ccol_ref, adaw_ref, adab_ref, wbn_ref, wobn_ref, modn_ref):
    i = pl.program_id(0)

    @pl.when(i == 0)
    def _():
        modn_ref[...] = adab_ref[...]

    wbn_ref[...] = wfn_ref[...].astype(wbn_ref.dtype)
    wobn_ref[...] = won_ref[...].astype(wobn_ref.dtype)
    rows = adaw_ref.shape[0]
    c_act = _silu(ccol_ref[pl.ds(pl.multiple_of(i * rows, rows), rows), :])
    modn_ref[...] += jnp.sum(adaw_ref[...] * c_act, axis=0, keepdims=True)


def _first_prepare(w_in, w_out, c_col, ada_w, ada_b3, steps=8):
    rows = D_MODEL // steps

    def rows_of(w):
        return pl.BlockSpec((None, rows, w.shape[2]), lambda i: (0, i, 0))

    return pl.pallas_call(
        _first_prepare_kernel,
        grid=(steps,),
        in_specs=[rows_of(w_in), rows_of(w_out), pl.BlockSpec(c_col.shape, lambda i: (0, 0)), rows_of(ada_w),
                  pl.BlockSpec((None,) + ada_b3.shape[1:], lambda i: (0, 0, 0))],
        out_specs=[pl.BlockSpec((rows, w_in.shape[2]), lambda i: (i, 0)),
                   pl.BlockSpec((rows, w_out.shape[2]), lambda i: (i, 0)),
                   pl.BlockSpec(ada_b3.shape[1:], lambda i: (0, 0))],
        out_shape=[jax.ShapeDtypeStruct(w_in.shape[1:], MXU_DTYPE), jax.ShapeDtypeStruct(w_out.shape[1:], MXU_DTYPE),
                   jax.ShapeDtypeStruct(ada_b3.shape[1:], F32)],
        compiler_params=pltpu.CompilerParams(dimension_semantics=("arbitrary",)),
        name="first_prepare",
    )(w_in, w_out, c_col, ada_w, ada_b3)


def _gla_level_matrices():
    mg = np.zeros(((N_LEVELS + 1) * CH, CH), np.float32)
    for t in range(CH):
        mg[t, :t + 1] = 1.0
    for l in range(1, N_LEVELS + 1):
        s = 2 ** (l - 1)
        for t in range(CH):
            m = (t // (2 * s)) * 2 * s + s
            if t >= m:
                mg[l * CH + t, m:t + 1] = 1.0
            else:
                mg[l * CH + t, t + 1:m] = 1.0
    lvl = np.full((CH, CH), -1, np.int32)
    for t in range(CH):
        lvl[t, t] = 0
        for s_ in range(t):
            lvl[t, s_] = int(np.floor(np.log2(t ^ s_))) + 1
    return mg, np.tile(lvl, (1, GLA_HEADS))


def _constants():
    mg, lv = _gla_level_matrices()
    tri = np.tril(np.ones((CH, CH), np.float32))
    lt2 = np.kron(np.eye(SUB // CH, dtype=np.float32), tri)
    ones_bd = np.kron(np.eye(RWKV_WIDTH // HEAD_DIM, dtype=np.float32), np.ones((HEAD_DIM, HEAD_DIM), np.float32))
    twice = lambda m: np.concatenate([m, m], axis=1)
    return (jnp.asarray(twice(mg), MXU_DTYPE), jnp.asarray(lv), jnp.asarray(twice(lt2), MXU_DTYPE),
            jnp.asarray(ones_bd, MXU_DTYPE))


def _layer_call(layer, x2d, sinks, stacked, consts, nxt):
    t = x2d.shape[0]
    n_blocks = t // TM

    def full(a):
        return pl.BlockSpec(a.shape, lambda i: (0,) * a.ndim)

    def of_layer(a):
        return pl.BlockSpec((None,) + a.shape[1:], lambda i: (layer,) + (0,) * (a.ndim - 1))

    row_spec = pl.BlockSpec((TM, D_MODEL), lambda i: (i, 0))
    next_spec = pl.BlockSpec((TM, D_MODEL), lambda i: (jnp.minimum(i + 1, n_blocks - 1), 0))
    operands = tuple(stacked) + tuple(consts)
    in_specs = ([pl.BlockSpec(memory_space=pltpu.SMEM), row_spec, next_spec]
                + [of_layer(a) if a.ndim == 3 else full(a) for a in stacked] + [full(a) for a in consts])
    assert len(in_specs) == N_LAYER_INPUTS
    out_specs, out_shape, body = row_spec, jax.ShapeDtypeStruct((t, D_MODEL), F32), _layer_kernel
    if nxt is not None:
        w_in, w_out, c_col, ada_w, ada_b3 = nxt
        rows = D_MODEL // n_blocks
        assert rows * n_blocks == D_MODEL and rows % 16 == 0

        def next_rows(w):
            return pl.BlockSpec((None, rows, w.shape[2]), lambda i: (layer + 1, i, 0))

        operands += nxt
        in_specs += [next_rows(w_in), next_rows(w_out), full(c_col), next_rows(ada_w),
                     pl.BlockSpec((None,) + ada_b3.shape[1:], lambda i: (layer + 1, 0, 0))]
        assert len(in_specs) == N_LAYER_INPUTS + N_PREPARE_INPUTS
        out_specs = [row_spec, pl.BlockSpec((rows, w_in.shape[2]), lambda i: (i, 0)),
                     pl.BlockSpec((rows, w_out.shape[2]), lambda i: (i, 0)),
                     pl.BlockSpec(ada_b3.shape[1:], lambda i: (0, 0))]
        out_shape = [out_shape, jax.ShapeDtypeStruct(w_in.shape[1:], MXU_DTYPE),
                     jax.ShapeDtypeStruct(w_out.shape[1:], MXU_DTYPE), jax.ShapeDtypeStruct(ada_b3.shape[1:], F32)]
        body = _layer_kernel_prepare
    return pl.pallas_call(
        body,
        grid=(n_blocks,),
        in_specs=in_specs,
        out_specs=out_specs,
        out_shape=out_shape,
        scratch_shapes=[
            pltpu.VMEM((2, TM, AUG_WIDTH), F32),
            pltpu.VMEM((2, TM, GLA_KEY_WIDTH), F32),
            pltpu.SMEM((2,), F32),
            pltpu.VMEM((TM, GLA_VAL_WIDTH), F32),
            pltpu.VMEM((GLA_VAL_WIDTH, GLA_KEY_WIDTH), F32),
            pltpu.VMEM((TM, D_MODEL), MXU_DTYPE),
            pltpu.VMEM((TM, D_MODEL), MXU_DTYPE),
            pltpu.VMEM((RWKV_WIDTH, RWKV_WIDTH), F32),
            pltpu.VMEM((GLA_VAL_WIDTH, GLA_KEY_WIDTH), F32),
            pltpu.VMEM((SUB, KV_WIDTH), F32),
            pltpu.VMEM((SUB, KV_WIDTH), F32),
            pltpu.VMEM((8, 3 * RWKV_WIDTH + 128), F32),
        ],
        compiler_params=pltpu.CompilerParams(dimension_semantics=("arbitrary",),
                                             vmem_limit_bytes=VMEM_LIMIT_BYTES),
        name="hybrid_layer",
    )(sinks, x2d, x2d, *operands)


def kernel(x, c, ada_w, ada_b, norm_pre, norm_post, w_in, w_out, attn_sinks, rwkv_mu_rkv, rwkv_mu_w, rwkv_mu_a, rwkv_w0, rwkv_w1, rwkv_w2, rwkv_a0, rwkv_a1, rwkv_a2, rwkv_k_k, rwkv_k_a, rwkv_r_k, rwkv_ln_w, rwkv_ln_b, gla_gk1, gla_gk2, gla_gk_b, gla_norm_w):
    batch, seq, _ = x.shape
    depth = w_in.shape[0]
    assert batch == 1 and seq % TM == 0

    c_col = c.reshape(D_MODEL, 1)
    ada_b3 = ada_b.reshape(depth, 1, 3 * D_MODEL)

    mu_w = rwkv_mu_w[:, :, None]
    mu_a = rwkv_mu_a[:, :, None]
    aux_width = AUG_WIDTH - IN_WIDTH

    def placed(w, col):
        return jnp.pad(w, ((0, 0), (0, 0), (col, aux_width - col - w.shape[2])))

    w_aux = (placed((1.0 - mu_w) * rwkv_w1, O_X1 - IN_WIDTH) + placed((1.0 - mu_a) * rwkv_a1, O_X1 - IN_WIDTH + LORA)
             + placed(mu_w * rwkv_w1, O_X2 - IN_WIDTH) + placed(mu_a * rwkv_a1, O_X2 - IN_WIDTH + LORA)
             + placed(gla_gk1, O_X3 - IN_WIDTH)).astype(MXU_DTYPE)
    zeros = jnp.zeros((depth, LORA, RWKV_WIDTH), F32)
    w2cat = jnp.concatenate([jnp.concatenate([rwkv_w2, zeros], axis=-1),
                             jnp.concatenate([zeros, rwkv_a2], axis=-1)], axis=1)
    w2cat = w2cat.astype(MXU_DTYPE)
    gk2p = jnp.pad(gla_gk2, ((0, 0), (0, 128 - GLA_GATE_LORA), (0, 0))).astype(MXU_DTYPE)
    rv = jnp.stack([rwkv_w0, rwkv_a0, rwkv_k_k, rwkv_k_a, rwkv_r_k.reshape(depth, RWKV_WIDTH),
                    rwkv_ln_w, rwkv_ln_b, jnp.tile(gla_norm_w, (1, GLA_HEADS))], axis=1)
    consts = _constants()
    w_main, wout, mod = _first_prepare(w_in, w_out, c_col, ada_w, ada_b3)
    xc = x[0]
    for l in range(depth):
        stacked = (mod, norm_pre[:, None], norm_post[:, None], w_main, w_aux, wout, w2cat, gk2p,
                   rwkv_mu_rkv[:, None], rv, gla_gk_b[:, None])
        if l + 1 < depth:
            xc, w_main, wout, mod = _layer_call(l, xc, attn_sinks[l], stacked, consts,
                                                (w_in, w_out, c_col, ada_w, ada_b3))
        else:
            xc = _layer_call(l, xc, attn_sinks[l], stacked, consts, None)
    return xc[None]
```
